```python
import jax, jax.numpy as jnp
from jax import lax
import numpy as np

D_MODEL = 2048
BATCH = 4
SEQ = 2048
DEPTH = 4

HEAD_DIM = 64
RWKV_HEADS = 16
NSA_HEADS = 16
D_RWKV = RWKV_HEADS * HEAD_DIM
D_NSA = NSA_HEADS * HEAD_DIM
D_MIX = D_RWKV + D_NSA
DECAY_LORA = 96
AAA_LORA = 96
MV_LORA = 64
GATE_LORA = 256
GN_EPS = 64e-5
NSA_KV_HEADS = 4
NSA_KV = NSA_KV_HEADS * HEAD_DIM
CMP_BLOCK = 32
CMP_STRIDE = 16
CMP_HIDDEN = 256
SEL_BLOCK = 64
N_SELECT = 8
WINDOW = 512
Q_BLOCK = 128
ROPE_THETA = 10000.0
NEG = -1e30
FORCE = 1e4
D_FF = 4 * D_MODEL
LN_EPS = 1e-5
DN_ALPHA = (2 * DEPTH) ** 0.25
DN_BETA = (8 * DEPTH) ** -0.25
RWKV_COLS = 3 * D_RWKV + DECAY_LORA + AAA_LORA + GATE_LORA
NSA_COLS = D_NSA + 6 * NSA_KV + 3 * NSA_HEADS
IN_COLS = RWKV_COLS + NSA_COLS

kernel_name = "hymba_rwkv7_nsa_deepnorm_adaln"


def _layer_norm(x, g, b):
    xf = x.astype(jnp.float32)
    mu = xf.mean(-1, keepdims=True)
    var = jnp.mean(jnp.square(xf - mu), -1, keepdims=True)
    return ((xf - mu) * lax.rsqrt(var + LN_EPS) * g + b).astype(x.dtype)


def _rope(x, pos):
    half = HEAD_DIM // 2
    inv = ROPE_THETA ** (-jnp.arange(half, dtype=jnp.float32) / half)
    ang = pos.astype(jnp.float32)[:, None] * inv[None]
    cos = jnp.cos(ang)[None, :, None, :]
    sin = jnp.sin(ang)[None, :, None, :]
    x1, x2 = x[..., :half], x[..., half:]
    return jnp.concatenate([x1 * cos - x2 * sin, x2 * cos + x1 * sin], -1).astype(x.dtype)


def _token_shift(u):
    return jnp.pad(u, ((0, 0), (1, 0), (0, 0)))[:, :-1]


def _rwkv7_mix(u, v_first, mu, w0, w2, a0, a2, g2, k_k, k_a, r_k, lnx_g, lnx_b, v_res):
    B, T, _ = u.shape
    H, N = RWKV_HEADS, HEAD_DIM
    u = u + (_token_shift(u) - u) * mu
    r, k, v, xw, xa, xg = jnp.split(u, np.cumsum([D_RWKV] * 3 + [DECAY_LORA, AAA_LORA]).tolist(), axis=-1)
    w = -jax.nn.softplus(-(w0 + jnp.tanh(xw) @ w2)) - 0.5
    decay = jnp.exp(-jnp.exp(w.astype(jnp.float32)))
    a = jax.nn.sigmoid(a0 + xa @ a2)
    g = jax.nn.sigmoid(xg) @ g2
    if v_res is None:
        v_first = v
    else:
        v0, v1, v2 = v_res
        v = v + (v_first - v) * jax.nn.sigmoid(v0 + (v @ v1) @ v2)
    heads = lambda z: z.reshape(B, T, H, N).astype(jnp.float32)
    kk = heads(k * k_k)
    kk = kk / jnp.maximum(jnp.sqrt(jnp.sum(kk * kk, -1, keepdims=True)), 1e-12)
    k = k * (1 + (a - 1) * k_a)
    rh, kh, vh, ah, dh = heads(r), heads(k), heads(v), heads(a), heads(decay)

    def step(S, inp):
        r_t, w_t, k_t, v_t, kk_t, a_t = inp
        sa = jnp.einsum('bhvk,bhk->bhv', S, -kk_t)
        S = S * w_t[:, :, None, :] + sa[..., None] * (kk_t * a_t)[:, :, None, :] + v_t[..., None] * k_t[:, :, None, :]
        return S, jnp.einsum('bhvk,bhk->bhv', S, r_t)

    xs = tuple(jnp.moveaxis(z, 1, 0) for z in (rh, dh, kh, vh, kk, ah))
    _, o = lax.scan(step, jnp.zeros((B, H, N, N), jnp.float32), xs)
    o = jnp.moveaxis(o, 0, 1)
    mu_o = o.mean(-1, keepdims=True)
    var_o = jnp.mean(jnp.square(o - mu_o), -1, keepdims=True)
    o = ((o - mu_o) * lax.rsqrt(var_o + GN_EPS)).reshape(B, T, D_RWKV) * lnx_g + lnx_b
    bonus = jnp.sum(rh * kh * r_k, -1, keepdims=True) * vh
    o = (o + bonus.reshape(B, T, D_RWKV)) * g
    return o.astype(u.dtype), v_first


def _overlap_matrix(n_cmp, n_sel):
    pos = np.arange(n_cmp)[:, None] * CMP_STRIDE + np.arange(CMP_BLOCK)[None]
    m = ((pos // SEL_BLOCK)[..., None] == np.arange(n_sel)).sum(1) / CMP_BLOCK
    return jnp.asarray(m, jnp.float32)


def _compress(kv, pos_emb, w1, w2):
    B, T, G, N = kv.shape
    n_cmp = (T - CMP_BLOCK) // CMP_STRIDE + 1
    idx = np.arange(n_cmp)[:, None] * CMP_STRIDE + np.arange(CMP_BLOCK)[None]
    blocks = kv[:, idx] + pos_emb[:, None, :]
    blocks = jnp.transpose(blocks, (0, 3, 1, 2, 4)).reshape(B, G, n_cmp, CMP_BLOCK * N)
    return jax.nn.gelu(blocks @ w1) @ w2


def _nsa_mix(u, cmp_pos, cmp_w1, cmp_w2):
    B, T, _ = u.shape
    H, G, N = NSA_HEADS, NSA_KV_HEADS, HEAD_DIM
    HPG = H // G
    scale = N ** -0.5
    q, k_c, v_c, k_s, v_s, k_w, v_w, gate = jnp.split(u, np.cumsum([D_NSA] + [NSA_KV] * 6).tolist(), axis=-1)
    pos = jnp.arange(T)
    q = _rope(q.reshape(B, T, H, N), pos)
    k_c, k_s, k_w = (_rope(z.reshape(B, T, G, N), pos) for z in (k_c, k_s, k_w))
    v_c, v_s, v_w = (z.reshape(B, T, G, N) for z in (v_c, v_s, v_w))
    qg = q.reshape(B, T, G, HPG, N).transpose(0, 2, 3, 1, 4)

    kc = _compress(k_c, cmp_pos[0], cmp_w1[0], cmp_w2[0])
    vc = _compress(v_c, cmp_pos[1], cmp_w1[1], cmp_w2[1])
    n_cmp = kc.shape[2]
    cmask = (jnp.arange(n_cmp) * CMP_STRIDE + CMP_BLOCK - 1)[None, :] <= pos[:, None]
    s = jnp.einsum('bghtd,bgcd->bghtc', qg, kc).astype(jnp.float32) * scale
    p_cmp = jax.nn.softmax(jnp.where(cmask, s, NEG), -1) * cmask
    o_cmp = jnp.einsum('bghtc,bgcd->bghtd', p_cmp.astype(vc.dtype), vc)

    n_sel = T // SEL_BLOCK
    k_top = min(N_SELECT, n_sel)
    imp = jnp.einsum('bgtc,cs->bgts', p_cmp.sum(2), _overlap_matrix(n_cmp, n_sel))
    blk = jnp.arange(n_sel)[None]
    cur = (pos // SEL_BLOCK)[:, None]
    forced = (blk == 0) | (blk == cur) | (blk == cur - 1)
    causal = blk * SEL_BLOCK <= pos[:, None]
    imp = jnp.where(forced, FORCE, jnp.where(causal, imp, -1.0))
    _, sel_idx = lax.top_k(imp, k_top)

    ks_blocks = k_s.reshape(B, n_sel, SEL_BLOCK, G, N).transpose(0, 3, 1, 2, 4)
    vs_blocks = v_s.reshape(B, n_sel, SEL_BLOCK, G, N).transpose(0, 3, 1, 2, 4)
    kw_pad = jnp.pad(k_w, ((0, 0), (WINDOW, 0), (0, 0), (0, 0)))
    vw_pad = jnp.pad(v_w, ((0, 0), (WINDOW, 0), (0, 0), (0, 0)))
    n_q = T // Q_BLOCK
    q_blocks = qg.reshape(B, G, HPG, n_q, Q_BLOCK, N).transpose(3, 0, 1, 2, 4, 5)
    idx_blocks = sel_idx.reshape(B, G, n_q, Q_BLOCK, k_top).transpose(2, 0, 1, 3, 4)
    b_ix = jnp.arange(B)[:, None, None, None]
    g_ix = jnp.arange(G)[None, :, None, None]

    def block_attn(args):
        qi, qb, ib = args
        t = qi * Q_BLOCK + jnp.arange(Q_BLOCK)
        ksel = ks_blocks[b_ix, g_ix, ib]
        vsel = vs_blocks[b_ix, g_ix, ib]
        kpos = ib[..., None] * SEL_BLOCK + jnp.arange(SEL_BLOCK)
        smask = kpos <= t[:, None, None]
        ss = jnp.einsum('bghqd,bgqkld->bghqkl', qb, ksel).astype(jnp.float32) * scale
        ss = jnp.where(smask[:, :, None], ss, NEG).reshape(B, G, HPG, Q_BLOCK, -1)
        ps = jax.nn.softmax(ss, -1).reshape(B, G, HPG, Q_BLOCK, k_top, SEL_BLOCK)
        o_s = jnp.einsum('bghqkl,bgqkld->bghqd', ps.astype(vsel.dtype), vsel)
        start = qi * Q_BLOCK
        kw = lax.dynamic_slice_in_dim(kw_pad, start, Q_BLOCK + WINDOW, axis=1)
        vw = lax.dynamic_slice_in_dim(vw_pad, start, Q_BLOCK + WINDOW, axis=1)
        spos = start - WINDOW + jnp.arange(Q_BLOCK + WINDOW)
        wmask = (spos[None] <= t[:, None]) & (t[:, None] - spos[None] < WINDOW) & (spos[None] >= 0)
        sw = jnp.einsum('bghqd,bsgd->bghqs', qb, kw).astype(jnp.float32) * scale
        pw = jax.nn.softmax(jnp.where(wmask, sw, NEG), -1)
        o_w = jnp.einsum('bghqs,bsgd->bghqd', pw.astype(vw.dtype), vw)
        return o_s, o_w

    o_slc, o_win = lax.map(block_attn, (jnp.arange(n_q), q_blocks, idx_blocks))
    o_cmp = o_cmp.transpose(0, 3, 1, 2, 4).reshape(B, T, H, N)
    o_slc = o_slc.transpose(1, 0, 4, 2, 3, 5).reshape(B, T, H, N)
    o_win = o_win.transpose(1, 0, 4, 2, 3, 5).reshape(B, T, H, N)
    gate = jax.nn.sigmoid(gate).reshape(B, T, H, 3)
    o = gate[..., 0:1] * o_cmp + gate[..., 1:2] * o_slc + gate[..., 2:3] * o_win
    return o.reshape(B, T, D_NSA).astype(u.dtype)


def setup_inputs(seed: int = 0) -> dict:
    key = jax.random.key(seed)
    keys = iter(jax.random.split(key, 40))
    nrm = lambda shape, s: jax.random.normal(next(keys), shape, jnp.float32) * s
    unif = lambda shape, lo, hi: jax.random.uniform(next(keys), shape, jnp.float32, lo, hi)
    L, D = DEPTH, D_MODEL
    return {
        "x": nrm((BATCH, SEQ, D), 1.0),
        "c": nrm((BATCH, D), 1.0),
        "w_ada": nrm((L, D, 6 * D), 0.2 * D ** -0.5),
        "b_ada": nrm((L, 6 * D), 0.02),
        "w_in": nrm((L, D, IN_COLS), D ** -0.5),
        "rwkv_mu": unif((L, RWKV_COLS), 0.0, 1.0),
        "rwkv_w0": unif((L, D_RWKV), -6.5, -1.5),
        "rwkv_w2": nrm((L, DECAY_LORA, D_RWKV), 0.5 * DECAY_LORA ** -0.5),
        "rwkv_a0": nrm((L, D_RWKV), 0.1),
        "rwkv_a2": nrm((L, AAA_LORA, D_RWKV), 0.5 * AAA_LORA ** -0.5),
        "rwkv_g2": nrm((L, GATE_LORA, D_RWKV), GATE_LORA ** -0.5),
        "rwkv_k_k": 0.85 + nrm((L, D_RWKV), 0.05),
        "rwkv_k_a": 1.0 + nrm((L, D_RWKV), 0.05),
        "rwkv_r_k": nrm((L, RWKV_HEADS, HEAD_DIM), 0.1),
        "rwkv_lnx_g": 1.0 + nrm((L, D_RWKV), 0.05),
        "rwkv_lnx_b": nrm((L, D_RWKV), 0.02),
        "rwkv_v0": nrm((L - 1, D_RWKV), 0.1),
        "rwkv_v1": nrm((L - 1, D_RWKV, MV_LORA), D_RWKV ** -0.5),
        "rwkv_v2": nrm((L - 1, MV_LORA, D_RWKV), 0.5 * MV_LORA ** -0.5),
        "nsa_cmp_pos": nrm((L, 2, CMP_BLOCK, HEAD_DIM), 0.1),
        "nsa_cmp_w1": nrm((L, 2, CMP_BLOCK * HEAD_DIM, CMP_HIDDEN), (CMP_BLOCK * HEAD_DIM) ** -0.5),
        "nsa_cmp_w2": nrm((L, 2, CMP_HIDDEN, HEAD_DIM), CMP_HIDDEN ** -0.5),
        "w_out": nrm((L, D_MIX, D), DN_BETA * D_MIX ** -0.5),
        "ln1_g": 1.0 + nrm((L, D), 0.05),
        "ln1_b": nrm((L, D), 0.02),
        "mlp_w1": nrm((L, D, D_FF), D ** -0.5),
        "mlp_w2": nrm((L, D_FF, D), DN_BETA * D_FF ** -0.5),
        "ln2_g": 1.0 + nrm((L, D), 0.05),
        "ln2_b": nrm((L, D), 0.02),
    }


def reference(x, c, w_ada, b_ada, w_in, rwkv_mu, rwkv_w0, rwkv_w2, rwkv_a0, rwkv_a2, rwkv_g2,
              rwkv_k_k, rwkv_k_a, rwkv_r_k, rwkv_lnx_g, rwkv_lnx_b, rwkv_v0, rwkv_v1, rwkv_v2,
              nsa_cmp_pos, nsa_cmp_w1, nsa_cmp_w2, w_out, ln1_g, ln1_b, mlp_w1, mlp_w2, ln2_g, ln2_b):
    cond = jax.nn.silu(c)
    v_first = None
    for i in range(DEPTH):
        mod = cond @ w_ada[i] + b_ada[i]
        sh_m, sc_m, gt_m, sh_f, sc_f, gt_f = [m[:, None, :] for m in jnp.split(mod, 6, axis=-1)]
        h = x * (1 + sc_m) + sh_m
        u = h @ w_in[i]
        v_res = None if i == 0 else (rwkv_v0[i - 1], rwkv_v1[i - 1], rwkv_v2[i - 1])
        o_rwkv, v_first = _rwkv7_mix(u[..., :RWKV_COLS], v_first, rwkv_mu[i], rwkv_w0[i], rwkv_w2[i],
                                     rwkv_a0[i], rwkv_a2[i], rwkv_g2[i], rwkv_k_k[i], rwkv_k_a[i],
                                     rwkv_r_k[i], rwkv_lnx_g[i], rwkv_lnx_b[i], v_res)
        o_nsa = _nsa_mix(u[..., RWKV_COLS:], nsa_cmp_pos[i], nsa_cmp_w1[i], nsa_cmp_w2[i])
        y = jnp.concatenate([o_rwkv, o_nsa], axis=-1) @ w_out[i]
        x = _layer_norm(DN_ALPHA * x + (1 + gt_m) * y, ln1_g[i], ln1_b[i])
        h = x * (1 + sc_f) + sh_f
        y = jnp.square(jax.nn.relu(h @ mlp_w1[i])) @ mlp_w2[i]
        x = _layer_norm(DN_ALPHA * x + (1 + gt_f) * y, ln2_g[i], ln2_b[i])
    return x
```

```python
import functools

import numpy as np
import jax
import jax.numpy as jnp
from jax import lax
from jax.experimental import pallas as pl
from jax.experimental.pallas import tpu as pltpu

F32 = jnp.float32
BF16 = jnp.bfloat16
HI = lax.Precision.HIGHEST

HEAD_DIM = 64
RWKV_HEADS = 16
NSA_HEADS = 16
NSA_KV_HEADS = 4
HEADS_PER_GROUP = NSA_HEADS // NSA_KV_HEADS
D_RWKV = RWKV_HEADS * HEAD_DIM
D_NSA = NSA_HEADS * HEAD_DIM
NSA_KV = NSA_KV_HEADS * HEAD_DIM
DECAY_LORA = 96
AAA_LORA = 96
MV_LORA = 64
GATE_LORA = 256
GN_EPS = 64e-5
CMP_BLOCK = 32
CMP_STRIDE = 16
SEL_BLOCK = 64
N_SELECT = 8
WINDOW = 512
ROPE_THETA = 10000.0
NEG = -1e30
FORCE = 1e4
LN_EPS = 1e-5

LANES = 128
LORA_PAD = LANES
RW_XW = 3 * D_RWKV
RW_XA = RW_XW + LORA_PAD
RW_XG = RW_XA + LORA_PAD
RW_COLS = RW_XG + GATE_LORA
NS_KC = D_NSA
NS_GATE = D_NSA + 6 * NSA_KV
NS_COLS = NS_GATE + LANES
PROJ_TN = 896

CHUNK = 64
VMEM_LIMIT = 56 * 1024 * 1024


def _cparams(sem):
    return pltpu.CompilerParams(dimension_semantics=sem, vmem_limit_bytes=VMEM_LIMIT)


def _dot(a, b, prec=None):
    return jnp.dot(a, b, preferred_element_type=F32, precision=prec)


def _dot_nt(a, b, prec=None):
    return lax.dot_general(a, b, (((1,), (1,)), ((), ())), preferred_element_type=F32, precision=prec)


def _bmm(a, b, prec=None):
    return jnp.einsum("bij,bjk->bik", a, b, preferred_element_type=F32, precision=prec)


def _bmm_nt(a, b, prec=None):
    return jnp.einsum("bik,bjk->bij", a, b, preferred_element_type=F32, precision=prec)


def _bmm_tn(a, b, prec=None):
    return jnp.einsum("bci,bcj->bij", a, b, preferred_element_type=F32, precision=prec)


def _layer_norm_rows(z, g, b):
    mu = jnp.mean(z, axis=-1, keepdims=True)
    zc = z - mu
    var = jnp.mean(zc * zc, axis=-1, keepdims=True)
    return zc * lax.rsqrt(var + LN_EPS) * g + b


def _ada_kernel(c_ref, w_ref, b_ref, o_ref):
    c = c_ref[...]
    cond = c * jax.nn.sigmoid(c)
    o_ref[0] = _dot(cond.astype(BF16), w_ref[0].astype(BF16)) + b_ref[0]


def _ada_call(c_pad, w_ada, b_ada):
    L, D, N = w_ada.shape
    tn = 1024
    return pl.pallas_call(
        _ada_kernel,
        out_shape=jax.ShapeDtypeStruct((L, c_pad.shape[0], N), F32),
        grid=(L, N // tn),
        in_specs=[
            pl.BlockSpec(c_pad.shape, lambda l, n: (0, 0)),
            pl.BlockSpec((1, D, tn), lambda l, n: (l, 0, n)),
            pl.BlockSpec((1, 1, tn), lambda l, n: (l, 0, n)),
        ],
        out_specs=pl.BlockSpec((1, c_pad.shape[0], tn), lambda l, n: (l, 0, n)),
        compiler_params=_cparams(("parallel", "parallel")),
        name="adaln_mod",
    )(c_pad, w_ada, b_ada.reshape(L, 1, N))


def _modmm_kernel(x_ref, sc_ref, sh_ref, w_ref, o_ref, h_scr):
    @pl.when(pl.program_id(2) == 0)
    def _():
        h = x_ref[0] * (1.0 + sc_ref[0]) + sh_ref[0]
        h_scr[...] = h.astype(BF16)

    o_ref[0] = _dot(h_scr[...], w_ref[...])


def _modmm_call(x, mod3, row0, w_bf16, tm, tn, name):
    B, T, D = x.shape
    N = w_bf16.shape[1]
    return pl.pallas_call(
        _modmm_kernel,
        out_shape=jax.ShapeDtypeStruct((B, T, N), F32),
        grid=(B, T // tm, N // tn),
        in_specs=[
            pl.BlockSpec((1, tm, D), lambda b, m, n: (b, m, 0)),
            pl.BlockSpec((1, 1, D), lambda b, m, n: (row0 + 6 * b + 1, 0, 0)),
            pl.BlockSpec((1, 1, D), lambda b, m, n: (row0 + 6 * b, 0, 0)),
            pl.BlockSpec((D, tn), lambda b, m, n: (0, n)),
        ],
        out_specs=pl.BlockSpec((1, tm, tn), lambda b, m, n: (b, m, n)),
        scratch_shapes=[pltpu.VMEM((tm, D), BF16)],
        compiler_params=_cparams(("parallel", "parallel", "arbitrary")),
        name=name,
    )(x, mod3, mod3, w_bf16)


def _rwkv_prep_kernel(*refs, tt, first_layer):
    if first_layer:
        (u_ref, up_ref, mu_ref, w0_ref, w2_ref, a0_ref, a2_ref, g2_ref, kk_ref, ka_ref,
         r_o, ld_o, k_o, v_o, kk_o, a_o, g_o, sh_scr) = refs
    else:
        (u_ref, up_ref, mu_ref, w0_ref, w2_ref, a0_ref, a2_ref, g2_ref, kk_ref, ka_ref,
         vf_ref, v0_ref, v1_ref, v2_ref,
         r_o, ld_o, k_o, v_o, kk_o, a_o, g_o, sh_scr) = refs
    ti = pl.program_id(1)
    u = u_ref[0]
    prev = jnp.where(ti > 0, up_ref[0], 0.0)
    sh_scr[0:8, :] = prev
    sh_scr[8:8 + tt, :] = u
    us = sh_scr[7:7 + tt, :]
    x = u + (us - u) * mu_ref[...]
    r = x[:, 0:D_RWKV]
    k = x[:, D_RWKV:2 * D_RWKV]
    v = x[:, 2 * D_RWKV:3 * D_RWKV]
    xw = x[:, RW_XW:RW_XA]
    xa = x[:, RW_XA:RW_XG]
    xg = x[:, RW_XG:RW_COLS]
    w = -jax.nn.softplus(-(w0_ref[...] + _dot(jnp.tanh(xw).astype(BF16), w2_ref[...]))) - 0.5
    ld_o[0] = -jnp.exp(w)
    a = jax.nn.sigmoid(a0_ref[...] + _dot(xa.astype(BF16), a2_ref[...]))
    g_o[0] = _dot(jax.nn.sigmoid(xg).astype(BF16), g2_ref[...])
    if not first_layer:
        lo = _dot(v.astype(BF16), v1_ref[...])
        gate = jax.nn.sigmoid(v0_ref[...] + _dot(lo.astype(BF16), v2_ref[...]))
        v = v + (vf_ref[0] - v) * gate
    r_o[0] = r
    v_o[0] = v
    a_o[0] = a
    kk_o[0] = k * kk_ref[...]
    k_o[0] = k * (1.0 + (a - 1.0) * ka_ref[...])


def _rwkv_prep_call(u_rw, p, v_first, tt):
    B, T, _ = u_rw.shape
    first_layer = v_first is None
    row = lambda n: pl.BlockSpec((1, n), lambda b, t: (0, 0))
    full = lambda a: pl.BlockSpec(a.shape, lambda b, t: (0, 0))
    tile = pl.BlockSpec((1, tt, D_RWKV), lambda b, t: (b, t, 0))
    in_specs = [
        pl.BlockSpec((1, tt, RW_COLS), lambda b, t: (b, t, 0)),
        pl.BlockSpec((1, 8, RW_COLS), lambda b, t: (b, jnp.maximum(t * (tt // 8) - 1, 0), 0)),
        row(RW_COLS), row(D_RWKV), full(p["w2"]), row(D_RWKV), full(p["a2"]), full(p["g2"]),
        row(D_RWKV), row(D_RWKV),
    ]
    args = [u_rw, u_rw, p["mu"], p["w0"], p["w2"], p["a0"], p["a2"], p["g2"], p["k_k"], p["k_a"]]
    if not first_layer:
        in_specs += [tile, row(D_RWKV), full(p["v1"]), full(p["v2"])]
        args += [v_first, p["v0"], p["v1"], p["v2"]]
    out = jax.ShapeDtypeStruct((B, T, D_RWKV), F32)
    return pl.pallas_call(
        functools.partial(_rwkv_prep_kernel, tt=tt, first_layer=first_layer),
        out_shape=[out] * 7,
        grid=(B, T // tt),
        in_specs=in_specs,
        out_specs=[tile] * 7,
        scratch_shapes=[pltpu.VMEM((tt + 8, RW_COLS), F32)],
        compiler_params=_cparams(("parallel", "parallel")),
        name="rwkv_prep",
    )(*args)


def _rwkv_scan_kernel(r_ref, ld_ref, k_ref, v_ref, kk_ref, a_ref, g_ref, rk_ref, lg_ref, lb_ref,
                      o_ref, s_scr, *, ts, prec):
    C = CHUNK
    nc = ts // C
    nb = 2 * nc
    N = HEAD_DIM

    @pl.when(pl.program_id(2) == 0)
    def _():
        s_scr[...] = jnp.zeros_like(s_scr)

    def split(ref):
        x = ref[0].reshape(nc, C, LANES)
        return jnp.concatenate([x[:, :, :N], x[:, :, N:]], axis=0)

    def split_row(ref):
        x = ref[...]
        return jnp.concatenate([jnp.broadcast_to(x[None, :, :N], (nc, 1, N)),
                                jnp.broadcast_to(x[None, :, N:], (nc, 1, N))], axis=0)

    r, ld, k, v, kkr, a, g = (split(z) for z in (r_ref, ld_ref, k_ref, v_ref, kk_ref, a_ref, g_ref))
    rk, lg, lb = (split_row(z) for z in (rk_ref, lg_ref, lb_ref))

    kkn = kkr / jnp.maximum(jnp.sqrt(jnp.sum(kkr * kkr, axis=-1, keepdims=True)), 1e-12)
    row = lax.broadcasted_iota(jnp.int32, (C, C), 0)
    col = lax.broadcasted_iota(jnp.int32, (C, C), 1)
    incl = col <= row
    strict = col < row
    eye = col == row
    ltri = jnp.broadcast_to(jnp.where(incl, 1.0, 0.0).astype(F32), (nb, C, C))
    cs = _bmm(ltri, ld, HI)
    cs_last = cs[:, C - 1:C, :]
    e_in = jnp.exp(cs)
    e_ex = jnp.exp(cs - ld)
    e_neg = jnp.exp(-cs)
    e_hat = jnp.exp(cs_last - cs)
    at = -kkn * e_ex
    b = kkn * a
    bt = b * e_neg
    kt = k * e_neg
    rt = r * e_in
    bh = b * e_hat
    kh = k * e_hat
    A = _bmm_nt(jnp.concatenate([at, rt], axis=1), jnp.concatenate([bt, kt], axis=1), prec)
    a_ab = jnp.where(strict, A[:, :C, :C], 0.0)
    a_ak = jnp.where(strict, A[:, :C, C:], 0.0)
    a_rb = jnp.where(incl, A[:, C:, :C], 0.0)
    a_rk = jnp.where(incl, A[:, C:, C:], 0.0)
    npow = a_ab
    tinv = jnp.where(eye, 1.0, 0.0).astype(F32) + a_ab
    p2 = 2
    while p2 < C:
        npow = _bmm(npow, npow, prec)
        tinv = tinv + _bmm(npow, tinv, prec)
        p2 *= 2
    akv = _bmm(a_ak, v, prec)
    x = _bmm(tinv, jnp.concatenate([at, akv], axis=-1), prec)
    y = _bmm(a_rb, x, prec)
    wr = rt + y[:, :, :N]
    o0 = y[:, :, N:] + _bmm(a_rk, v, prec)
    pq = _bmm_tn(x, bh, prec)
    pm = pq[:, :N, :] + jnp.where(eye, jnp.exp(cs_last), 0.0)
    qm = pq[:, N:, :] + _bmm_tn(v, kh, prec)

    outs = []
    for hh in range(2):
        S = s_scr[hh]
        for c in range(nc):
            i = hh * nc + c
            outs.append(_dot_nt(wr[i], S, prec) + o0[i])
            S = _dot(S, pm[i], prec) + qm[i]
        s_scr[hh] = S
    o = jnp.stack(outs, axis=0)

    mu = jnp.mean(o, axis=-1, keepdims=True)
    oc = o - mu
    var = jnp.mean(oc * oc, axis=-1, keepdims=True)
    on = oc * lax.rsqrt(var + GN_EPS) * lg + lb
    bonus = jnp.sum(r * k * rk, axis=-1, keepdims=True) * v
    res = (on + bonus) * g
    o_ref[0] = jnp.concatenate([res[:nc].reshape(ts, N), res[nc:].reshape(ts, N)], axis=-1)


def _rwkv_scan_call(r, ld, k, v, kk, a, g, p, ts, prec):
    B, T, _ = r.shape
    tile = pl.BlockSpec((1, ts, LANES), lambda b, h, t: (b, t, h))
    rowp = pl.BlockSpec((1, LANES), lambda b, h, t: (0, h))
    return pl.pallas_call(
        functools.partial(_rwkv_scan_kernel, ts=ts, prec=prec),
        out_shape=jax.ShapeDtypeStruct((B, T, D_RWKV), F32),
        grid=(B, RWKV_HEADS // 2, T // ts),
        in_specs=[tile] * 7 + [rowp] * 3,
        out_specs=tile,
        scratch_shapes=[pltpu.VMEM((2, HEAD_DIM, HEAD_DIM), F32)],
        compiler_params=_cparams(("parallel", "parallel", "arbitrary")),
        name="rwkv_scan",
    )(r, ld, k, v, kk, a, g, p["r_k"], p["lnx_g"], p["lnx_b"])


def _nsa_prep_kernel(u_ref, cos_ref, sin_ref, q_o, kc_o, vc_o, ks_o, vs_o, kw_o, vw_o, gate_o):
    cos = cos_ref[...]
    sin = sin_ref[...]
    lane = lax.broadcasted_iota(jnp.int32, cos.shape, 1)
    first_half = (lane % HEAD_DIM) < (HEAD_DIM // 2)

    def rope(x):
        other = jnp.where(first_half, pltpu.roll(x, LANES - HEAD_DIM // 2, 1), pltpu.roll(x, HEAD_DIM // 2, 1))
        return x * cos + other * sin

    def put(out_ref, col0, nheads, roped, scale=None):
        for j in range(nheads // 2):
            x = u_ref[0, :, col0 + j * LANES:col0 + (j + 1) * LANES]
            if roped:
                x = rope(x)
            if scale is not None:
                x = x * scale
            out_ref[0, 2 * j] = x[:, :HEAD_DIM].astype(out_ref.dtype)
            out_ref[0, 2 * j + 1] = x[:, HEAD_DIM:].astype(out_ref.dtype)

    put(q_o, 0, NSA_HEADS, True, HEAD_DIM ** -0.5)
    put(kc_o, NS_KC, NSA_KV_HEADS, True)
    put(vc_o, NS_KC + NSA_KV, NSA_KV_HEADS, False)
    put(ks_o, NS_KC + 2 * NSA_KV, NSA_KV_HEADS, True)
    put(vs_o, NS_KC + 3 * NSA_KV, NSA_KV_HEADS, False)
    put(kw_o, NS_KC + 4 * NSA_KV, NSA_KV_HEADS, True)
    put(vw_o, NS_KC + 5 * NSA_KV, NSA_KV_HEADS, False)
    gates = jax.nn.sigmoid(u_ref[0, :, NS_GATE:NS_GATE + LANES])
    per_group = 3 * HEADS_PER_GROUP
    for gi in range(NSA_KV_HEADS):
        gate_o[0, gi] = gates[:, gi * per_group:(gi + 1) * per_group]


def _nsa_prep_call(u_ns, cos_t, sin_t, tt):
    B, T, _ = u_ns.shape
    G, H, N = NSA_KV_HEADS, NSA_HEADS, HEAD_DIM
    kv = lambda dt: jax.ShapeDtypeStruct((B, G, T, N), dt)
    kv_spec = pl.BlockSpec((1, G, tt, N), lambda b, t: (b, 0, t, 0))
    return pl.pallas_call(
        _nsa_prep_kernel,
        out_shape=[jax.ShapeDtypeStruct((B, H, T, N), BF16), kv(F32), kv(F32), kv(BF16), kv(BF16),
                   kv(BF16), kv(BF16), jax.ShapeDtypeStruct((B, G, T, 3 * HEADS_PER_GROUP), F32)],
        grid=(B, T // tt),
        in_specs=[
            pl.BlockSpec((1, tt, NS_COLS), lambda b, t: (b, t, 0)),
            pl.BlockSpec((tt, LANES), lambda b, t: (t, 0)),
            pl.BlockSpec((tt, LANES), lambda b, t: (t, 0)),
        ],
        out_specs=[pl.BlockSpec((1, H, tt, N), lambda b, t: (b, 0, t, 0))] + [kv_spec] * 6
        + [pl.BlockSpec((1, G, tt, 3 * HEADS_PER_GROUP), lambda b, t: (b, 0, t, 0))],
        compiler_params=_cparams(("parallel", "parallel")),
        name="nsa_prep",
    )(u_ns, cos_t, sin_t)


def _compress_kernel(x_ref, pos_ref, w1_ref, w2_ref, o_ref):
    x = x_ref[0, 0]
    half = x.shape[1]
    y_top = _dot(x + pos_ref[0, 0:1, :], w1_ref[0, :half, :], HI)
    y_bot = _dot(x + pos_ref[0, 1:2, :], w1_ref[0, half:, :], HI)
    pre = y_top + pltpu.roll(y_bot, x.shape[0] - 1, 0)
    o_ref[0, 0] = _dot(jax.nn.gelu(pre), w2_ref[0], HI)


def _compress_call(kv, pos, w1, w2, which):
    B, G, T, N = kv.shape
    n_half = T // CMP_STRIDE
    width = CMP_STRIDE * N
    hidden = w1.shape[-1]
    x = kv.reshape(B, G, n_half, width)
    return pl.pallas_call(
        _compress_kernel,
        out_shape=jax.ShapeDtypeStruct((B, G, n_half, N), F32),
        grid=(B, G),
        in_specs=[
            pl.BlockSpec((1, 1, n_half, width), lambda b, g: (b, g, 0, 0)),
            pl.BlockSpec((1, 2, width), lambda b, g: (which, 0, 0)),
            pl.BlockSpec((1, 2 * width, hidden), lambda b, g: (which, 0, 0)),
            pl.BlockSpec((1, hidden, N), lambda b, g: (which, 0, 0)),
        ],
        out_specs=pl.BlockSpec((1, 1, n_half, N), lambda b, g: (b, g, 0, 0)),
        compiler_params=_cparams(("parallel", "parallel")),
        name="nsa_compress",
    )(x, pos.reshape(2, 2, width), w1, w2)


def _nsa_attn_kernel(q_ref, kc_ref, vc_ref, ks_ref, vs_ref, kw_ref, vw_ref, gate_ref, ov_ref, ex_ref,
                     o_ref, msel_scr, *, tq):
    HPG, N = HEADS_PER_GROUP, HEAD_DIM
    qi = pl.program_id(2)
    q0 = qi * tq
    q4 = q_ref[0].reshape(HPG * tq, N)

    n_cmp = kc_ref.shape[2]
    s = _dot_nt(q4, kc_ref[0, 0].astype(BF16)).reshape(HPG, tq, n_cmp)
    t_c = q0 + lax.broadcasted_iota(jnp.int32, (tq, n_cmp), 0)
    c_c = lax.broadcasted_iota(jnp.int32, (tq, n_cmp), 1)
    cmask = (c_c * CMP_STRIDE + (CMP_BLOCK - 1) <= t_c)[None]
    s = jnp.where(cmask, s, NEG)
    m = jnp.max(s, axis=-1, keepdims=True)
    e = jnp.where(cmask, jnp.exp(s - m), 0.0)
    l = jnp.sum(e, axis=-1, keepdims=True)
    p = e / jnp.maximum(l, 1e-30)
    o_cmp = _dot(p.reshape(HPG * tq, n_cmp).astype(BF16), vc_ref[0, 0].astype(BF16))
    psum = p[0]
    for h in range(1, HPG):
        psum = psum + p[h]

    n_sel = ov_ref.shape[0]
    imp = _dot_nt(ov_ref[...], psum, HI)
    blk = lax.broadcasted_iota(jnp.int32, (n_sel, tq), 0)
    t_s = q0 + lax.broadcasted_iota(jnp.int32, (n_sel, tq), 1)
    cur = t_s // SEL_BLOCK
    forced = (blk == 0) | (blk == cur) | (blk == cur - 1)
    val = jnp.where(forced, FORCE, jnp.where(blk * SEL_BLOCK <= t_s, imp, -1.0))
    rank = jnp.zeros((n_sel, tq), F32)
    for i in range(n_sel):
        vi = val[i:i + 1, :]
        ahead = (vi > val) | ((vi == val) & (blk > i))
        rank = rank + jnp.where(ahead, 1.0, 0.0)
    sel_t = jnp.where(rank < float(min(N_SELECT, n_sel)), 1.0, 0.0)
    msel_scr[...] = _dot(sel_t.T.astype(BF16), ex_ref[...])

    t_q = q0 + lax.broadcasted_iota(jnp.int32, (tq, tq), 0)
    k_l = lax.broadcasted_iota(jnp.int32, (tq, tq), 1)

    def flash(k_ref, v_ref, lo, hi, mask_fn):
        def body(j, carry):
            m_i, l_i, acc = carry
            k0 = pl.multiple_of(j * tq, tq)
            kb = k_ref[0, 0, pl.ds(k0, tq), :]
            vb = v_ref[0, 0, pl.ds(k0, tq), :]
            sc = _dot_nt(q4, kb).reshape(HPG, tq, tq)
            sc = jnp.where(mask_fn(k0)[None], sc, NEG).reshape(HPG * tq, tq)
            m_n = jnp.maximum(m_i, jnp.max(sc, axis=-1, keepdims=True))
            alpha = jnp.exp(m_i - m_n)
            pr = jnp.exp(sc - m_n)
            l_n = alpha * l_i + jnp.sum(pr, axis=-1, keepdims=True)
            acc = alpha * acc + _dot(pr.astype(BF16), vb)
            return m_n, l_n, acc

        init = (jnp.full((HPG * tq, 1), NEG, F32), jnp.zeros((HPG * tq, 1), F32),
                jnp.zeros((HPG * tq, N), F32))
        _, l_f, acc = lax.fori_loop(lo, hi, body, init)
        return acc / l_f

    def sel_mask(k0):
        return (msel_scr[:, pl.ds(k0, tq)] > 0.5) & (k0 + k_l <= t_q)

    def win_mask(k0):
        kp = k0 + k_l
        return (kp <= t_q) & (t_q - kp < WINDOW)

    o_slc = flash(ks_ref, vs_ref, 0, qi + 1, sel_mask)
    o_win = flash(kw_ref, vw_ref, jnp.maximum(qi - WINDOW // tq, 0), qi + 1, win_mask)

    gate = gate_ref[0, 0]
    outs = []
    for h in range(HPG):
        rows = slice(h * tq, (h + 1) * tq)
        outs.append(gate[:, 3 * h:3 * h + 1] * o_cmp[rows] + gate[:, 3 * h + 1:3 * h + 2] * o_slc[rows]
                    + gate[:, 3 * h + 2:3 * h + 3] * o_win[rows])
    o_ref[0] = jnp.concatenate(outs, axis=-1)


def _nsa_attn_call(q, kc, vc, ks, vs, kw, vw, gates, ov_t, ex, tq):
    B, H, T, N = q.shape
    G, HPG = NSA_KV_HEADS, HEADS_PER_GROUP
    n_half = kc.shape[2]
    cmp_spec = pl.BlockSpec((1, 1, n_half, N), lambda b, g, t: (b, g, 0, 0))
    kv_spec = pl.BlockSpec((1, 1, T, N), lambda b, g, t: (b, g, 0, 0))
    return pl.pallas_call(
        functools.partial(_nsa_attn_kernel, tq=tq),
        out_shape=jax.ShapeDtypeStruct((B, T, D_NSA), F32),
        grid=(B, G, T // tq),
        in_specs=[
            pl.BlockSpec((1, HPG, tq, N), lambda b, g, t: (b, g, t, 0)),
            cmp_spec, cmp_spec, kv_spec, kv_spec, kv_spec, kv_spec,
            pl.BlockSpec((1, 1, tq, 3 * HPG), lambda b, g, t: (b, g, t, 0)),
            pl.BlockSpec(ov_t.shape, lambda b, g, t: (0, 0)),
            pl.BlockSpec(ex.shape, lambda b, g, t: (0, 0)),
        ],
        out_specs=pl.BlockSpec((1, tq, HPG * N), lambda b, g, t: (b, t, g)),
        scratch_shapes=[pltpu.VMEM((tq, T), F32)],
        compiler_params=_cparams(("parallel", "parallel", "arbitrary")),
        name="nsa_attn",
    )(q, kc, vc, ks, vs, kw, vw, gates, ov_t, ex)


def _outproj_kernel(orw_ref, ons_ref, x_ref, gt_ref, w_ref, g_ref, b_ref, o_ref, *, alpha):
    half = orw_ref.shape[-1]
    y = _dot(orw_ref[0].astype(BF16), w_ref[:half, :]) + _dot(ons_ref[0].astype(BF16), w_ref[half:, :])
    z = alpha * x_ref[0] + (1.0 + gt_ref[0]) * y
    o_ref[0] = _layer_norm_rows(z, g_ref[...], b_ref[...])


def _outproj_call(o_rw, o_ns, x, mod3, row0, w_bf16, ln_g, ln_b, alpha, tm):
    B, T, D = x.shape
    half = o_rw.shape[-1]
    return pl.pallas_call(
        functools.partial(_outproj_kernel, alpha=alpha),
        out_shape=jax.ShapeDtypeStruct((B, T, D), F32),
        grid=(B, T // tm),
        in_specs=[
            pl.BlockSpec((1, tm, half), lambda b, m: (b, m, 0)),
            pl.BlockSpec((1, tm, half), lambda b, m: (b, m, 0)),
            pl.BlockSpec((1, tm, D), lambda b, m: (b, m, 0)),
            pl.BlockSpec((1, 1, D), lambda b, m: (row0 + 6 * b + 2, 0, 0)),
            pl.BlockSpec(w_bf16.shape, lambda b, m: (0, 0)),
            pl.BlockSpec((1, D), lambda b, m: (0, 0)),
            pl.BlockSpec((1, D), lambda b, m: (0, 0)),
        ],
        out_specs=pl.BlockSpec((1, tm, D), lambda b, m: (b, m, 0)),
        compiler_params=_cparams(("parallel", "parallel")),
        name="outproj_ln",
    )(o_rw, o_ns, x, mod3, w_bf16, ln_g, ln_b)


def _mlp_kernel(x_ref, sc_ref, sh_ref, gt_ref, w1_ref, w2_ref, g_ref, b_ref, o_ref, h_scr, acc_scr, *, alpha):
    f = pl.program_id(2)

    @pl.when(f == 0)
    def _():
        h = x_ref[0] * (1.0 + sc_ref[0]) + sh_ref[0]
        h_scr[...] = h.astype(BF16)
        acc_scr[...] = jnp.zeros_like(acc_scr)

    a = jnp.maximum(_dot(h_scr[...], w1_ref[...]), 0.0)
    acc_scr[...] += _dot((a * a).astype(BF16), w2_ref[...])

    @pl.when(f == pl.num_programs(2) - 1)
    def _():
        z = alpha * x_ref[0] + (1.0 + gt_ref[0]) * acc_scr[...]
        o_ref[0] = _layer_norm_rows(z, g_ref[...], b_ref[...])


def _mlp_call(x, mod3, row0, w1_bf16, w2_bf16, ln_g, ln_b, alpha, tm, tf):
    B, T, D = x.shape
    FF = w1_bf16.shape[1]
    modspec = lambda j: pl.BlockSpec((1, 1, D), lambda b, m, f: (row0 + 6 * b + j, 0, 0))
    return pl.pallas_call(
        functools.partial(_mlp_kernel, alpha=alpha),
        out_shape=jax.ShapeDtypeStruct((B, T, D), F32),
        grid=(B, T // tm, FF // tf),
        in_specs=[
            pl.BlockSpec((1, tm, D), lambda b, m, f: (b, m, 0)),
            modspec(4), modspec(3), modspec(5),
            pl.BlockSpec((D, tf), lambda b, m, f: (0, f)),
            pl.BlockSpec((tf, D), lambda b, m, f: (f, 0)),
            pl.BlockSpec((1, D), lambda b, m, f: (0, 0)),
            pl.BlockSpec((1, D), lambda b, m, f: (0, 0)),
        ],
        out_specs=pl.BlockSpec((1, tm, D), lambda b, m, f: (b, m, 0)),
        scratch_shapes=[pltpu.VMEM((tm, D), BF16), pltpu.VMEM((tm, D), F32)],
        compiler_params=_cparams(("parallel", "parallel", "arbitrary")),
        name="mlp_ln",
    )(x, mod3, mod3, mod3, w1_bf16, w2_bf16, ln_g, ln_b)


def _pad_cols(w, n):
    return jnp.pad(w, ((0, 0), (0, n - w.shape[1])))


def _pad_rows(w, n):
    return jnp.pad(w, ((0, n - w.shape[0]), (0, 0)))


def _split_w_in(w_in):
    c = np.cumsum([0, D_RWKV, D_RWKV, D_RWKV, DECAY_LORA, AAA_LORA, GATE_LORA]).tolist()
    rw = jnp.concatenate([w_in[:, c[0]:c[3]], _pad_cols(w_in[:, c[3]:c[4]], LORA_PAD),
                          _pad_cols(w_in[:, c[4]:c[5]], LORA_PAD), w_in[:, c[5]:c[6]]], axis=1)
    ns = _pad_cols(w_in[:, c[6]:], NS_COLS)
    return rw.astype(BF16), ns.astype(BF16)


def _pad_mu(mu):
    c = np.cumsum([0, 3 * D_RWKV, DECAY_LORA, AAA_LORA, GATE_LORA]).tolist()
    parts = [mu[c[0]:c[1]], jnp.pad(mu[c[1]:c[2]], (0, LORA_PAD - DECAY_LORA)),
             jnp.pad(mu[c[2]:c[3]], (0, LORA_PAD - AAA_LORA)), mu[c[3]:c[4]]]
    return jnp.concatenate(parts)[None, :]


def _rope_tables(T):
    half = HEAD_DIM // 2
    inv = ROPE_THETA ** (-jnp.arange(half, dtype=F32) / half)
    ang = jnp.arange(T, dtype=F32)[:, None] * inv[None]
    cos, sin = jnp.cos(ang), jnp.sin(ang)
    cos_t = jnp.tile(cos, (1, LANES // half))
    sin_t = jnp.tile(jnp.concatenate([-sin, sin], axis=1), (1, LANES // HEAD_DIM))
    return cos_t, sin_t


def _selection_constants(T):
    n_half = T // CMP_STRIDE
    n_cmp = (T - CMP_BLOCK) // CMP_STRIDE + 1
    n_sel = T // SEL_BLOCK
    pos = np.arange(n_cmp)[:, None] * CMP_STRIDE + np.arange(CMP_BLOCK)[None]
    ov = ((pos // SEL_BLOCK)[..., None] == np.arange(n_sel)).sum(1) / CMP_BLOCK
    ov_t = np.zeros((n_sel, n_half), np.float32)
    ov_t[:, :n_cmp] = ov.T
    ex = (np.arange(T)[None, :] // SEL_BLOCK == np.arange(n_sel)[:, None]).astype(np.float32)
    return jnp.asarray(ov_t), jnp.asarray(ex, BF16)


SCAN_PREC = HI


def kernel(x, c, w_ada, b_ada, w_in, rwkv_mu, rwkv_w0, rwkv_w2, rwkv_a0, rwkv_a2, rwkv_g2, rwkv_k_k, rwkv_k_a, rwkv_r_k, rwkv_lnx_g, rwkv_lnx_b, rwkv_v0, rwkv_v1, rwkv_v2, nsa_cmp_pos, nsa_cmp_w1, nsa_cmp_w2, w_out, ln1_g, ln1_b, mlp_w1, mlp_w2, ln2_g, ln2_b):
    B, T, D = x.shape
    L = w_ada.shape[0]
    alpha = (2 * L) ** 0.25

    c_pad = jnp.pad(c, ((0, 8 - B % 8 if B % 8 else 0), (0, 0)))
    mod = _ada_call(c_pad, w_ada, b_ada)[:, :B]
    mod3 = mod.reshape(L * B * 6, 1, D)
    cos_t, sin_t = _rope_tables(T)
    ov_t, ex = _selection_constants(T)
    row = lambda z: z.reshape(1, -1)

    v_first = None
    for i in range(L):
        row0 = i * B * 6
        w_rw, w_ns = _split_w_in(w_in[i])
        u_rw = _modmm_call(x, mod3, row0, w_rw, 512, PROJ_TN, "inproj_rwkv")
        u_ns = _modmm_call(x, mod3, row0, w_ns, 512, PROJ_TN, "inproj_nsa")

        p = {
            "mu": _pad_mu(rwkv_mu[i]), "w0": row(rwkv_w0[i]), "a0": row(rwkv_a0[i]),
            "w2": _pad_rows(rwkv_w2[i], LORA_PAD).astype(BF16),
            "a2": _pad_rows(rwkv_a2[i], LORA_PAD).astype(BF16),
            "g2": rwkv_g2[i].astype(BF16), "k_k": row(rwkv_k_k[i]), "k_a": row(rwkv_k_a[i]),
            "r_k": row(rwkv_r_k[i]), "lnx_g": row(rwkv_lnx_g[i]), "lnx_b": row(rwkv_lnx_b[i]),
        }
        if i > 0:
            p["v0"] = row(rwkv_v0[i - 1])
            p["v1"] = _pad_cols(rwkv_v1[i - 1], LORA_PAD).astype(BF16)
            p["v2"] = _pad_rows(rwkv_v2[i - 1], LORA_PAD).astype(BF16)
        r, ld, k, v, kk, a, g = _rwkv_prep_call(u_rw, p, v_first, 256)
        if i == 0:
            v_first = v
        o_rw = _rwkv_scan_call(r, ld, k, v, kk, a, g, p, 256, SCAN_PREC)

        q, kc_in, vc_in, ks, vs, kw, vw, gates = _nsa_prep_call(u_ns, cos_t, sin_t, 256)
        kc = _compress_call(kc_in, nsa_cmp_pos[i], nsa_cmp_w1[i], nsa_cmp_w2[i], 0)
        vc = _compress_call(vc_in, nsa_cmp_pos[i], nsa_cmp_w1[i], nsa_cmp_w2[i], 1)
        o_ns = _nsa_attn_call(q, kc, vc, ks, vs, kw, vw, gates, ov_t, ex, 256)

        x = _outproj_call(o_rw, o_ns, x, mod3, row0, w_out[i].astype(BF16), row(ln1_g[i]), row(ln1_b[i]),
                          alpha, 256)
        x = _mlp_call(x, mod3, row0, mlp_w1[i].astype(BF16), mlp_w2[i].astype(BF16), row(ln2_g[i]),
                      row(ln2_b[i]), alpha, 512, 512)
    return x
```

```python
import functools

import numpy as np
import jax
import jax.numpy as jnp
from jax import lax
from jax.experimental import pallas as pl
from jax.experimental.pallas import tpu as pltpu

F32 = jnp.float32
BF16 = jnp.bfloat16
HI = lax.Precision.HIGHEST

HEAD_DIM = 64
RWKV_HEADS = 16
NSA_HEADS = 16
NSA_KV_HEADS = 4
HEADS_PER_GROUP = NSA_HEADS // NSA_KV_HEADS
D_RWKV = RWKV_HEADS * HEAD_DIM
D_NSA = NSA_HEADS * HEAD_DIM
NSA_KV = NSA_KV_HEADS * HEAD_DIM
DECAY_LORA = 96
AAA_LORA = 96
MV_LORA = 64
GATE_LORA = 256
GN_EPS = 64e-5
CMP_BLOCK = 32
CMP_STRIDE = 16
SEL_BLOCK = 64
N_SELECT = 8
WINDOW = 512
ROPE_THETA = 10000.0
NEG = -1e30
FORCE = 1e4
LN_EPS = 1e-5
LOG2_E = 1.4426950408889634

LANES = 128
LORA_PAD = LANES
RW_XW = 3 * D_RWKV
RW_XA = RW_XW + LORA_PAD
RW_XG = RW_XA + LORA_PAD
RW_COLS = RW_XG + GATE_LORA
NS_KC = D_NSA
NS_GATE = D_NSA + 6 * NSA_KV
NS_COLS = NS_GATE + LANES
GATE_PAD = 16
PROJ_TN = 896

CHUNK = 64
VMEM_LIMIT = 56 * 1024 * 1024


def _cparams(sem):
    return pltpu.CompilerParams(dimension_semantics=sem, vmem_limit_bytes=VMEM_LIMIT)


def _mm(fn, a, b, mode):
    if mode is None:
        return fn(a, b, None)
    if mode == "hi":
        return fn(a, b, HI)
    ah, bh = a.astype(BF16), b.astype(BF16)
    if mode == "bf":
        return fn(ah, bh, None)
    al = (a - ah.astype(F32)).astype(BF16)
    bl = (b - bh.astype(F32)).astype(BF16)
    return fn(ah, bh, None) + (fn(ah, bl, None) + fn(al, bh, None))


def _dot(a, b, mode=None):
    return _mm(lambda p, q, pr: jnp.dot(p, q, preferred_element_type=F32, precision=pr), a, b, mode)


def _dot_nt(a, b, mode=None):
    return _mm(lambda p, q, pr: lax.dot_general(p, q, (((1,), (1,)), ((), ())), preferred_element_type=F32,
                                               precision=pr), a, b, mode)


def _bmm(a, b, mode=None):
    return _mm(lambda p, q, pr: jnp.einsum("bij,bjk->bik", p, q, preferred_element_type=F32, precision=pr),
               a, b, mode)


def _bmm_nt(a, b, mode=None):
    return _mm(lambda p, q, pr: jnp.einsum("bik,bjk->bij", p, q, preferred_element_type=F32, precision=pr),
               a, b, mode)


def _bmm_tn(a, b, mode=None):
    return _mm(lambda p, q, pr: jnp.einsum("bci,bcj->bij", p, q, preferred_element_type=F32, precision=pr),
               a, b, mode)


def _layer_norm_rows(z, g, b):
    mu = jnp.mean(z, axis=-1, keepdims=True)
    zc = z - mu
    var = jnp.mean(zc * zc, axis=-1, keepdims=True)
    return zc * lax.rsqrt(var + LN_EPS) * g + b


def _ada_kernel(c_ref, w_ref, b_ref, o_ref):
    c = c_ref[...]
    cond = c * jax.nn.sigmoid(c)
    o_ref[0] = _dot(cond.astype(BF16), w_ref[0].astype(BF16)) + b_ref[0]


def _ada_call(c_pad, w_ada, b_ada):
    L, D, N = w_ada.shape
    tn = 1024
    return pl.pallas_call(
        _ada_kernel,
        out_shape=jax.ShapeDtypeStruct((L, c_pad.shape[0], N), F32),
        grid=(L, N // tn),
        in_specs=[
            pl.BlockSpec(c_pad.shape, lambda l, n: (0, 0)),
            pl.BlockSpec((1, D, tn), lambda l, n: (l, 0, n)),
            pl.BlockSpec((1, 1, tn), lambda l, n: (l, 0, n)),
        ],
        out_specs=pl.BlockSpec((1, c_pad.shape[0], tn), lambda l, n: (l, 0, n)),
        compiler_params=_cparams(("parallel", "parallel")),
        name="adaln_mod",
    )(c_pad, w_ada, b_ada.reshape(L, 1, N))


def _modmm_kernel(x_ref, sc_ref, sh_ref, w_ref, o_ref, h_scr):
    @pl.when(pl.program_id(2) == 0)
    def _():
        h = x_ref[0] * (1.0 + sc_ref[0]) + sh_ref[0]
        h_scr[...] = h.astype(BF16)

    o_ref[0] = _dot(h_scr[...], w_ref[...])


def _modmm_call(x, mod3, row0, w_bf16, tm, tn, name):
    B, T, D = x.shape
    N = w_bf16.shape[1]
    return pl.pallas_call(
        _modmm_kernel,
        out_shape=jax.ShapeDtypeStruct((B, T, N), F32),
        grid=(B, T // tm, N // tn),
        in_specs=[
            pl.BlockSpec((1, tm, D), lambda b, m, n: (b, m, 0)),
            pl.BlockSpec((1, 1, D), lambda b, m, n: (row0 + 6 * b + 1, 0, 0)),
            pl.BlockSpec((1, 1, D), lambda b, m, n: (row0 + 6 * b, 0, 0)),
            pl.BlockSpec((D, tn), lambda b, m, n: (0, n)),
        ],
        out_specs=pl.BlockSpec((1, tm, tn), lambda b, m, n: (b, m, n)),
        scratch_shapes=[pltpu.VMEM((tm, D), BF16)],
        compiler_params=_cparams(("parallel", "parallel", "arbitrary")),
        name=name,
    )(x, mod3, mod3, w_bf16)


def _rwkv_prep_kernel(*refs, tt, first_layer):
    if first_layer:
        (u_ref, up_ref, mu_ref, w0_ref, w2_ref, a0_ref, a2_ref, g2_ref, kk_ref, ka_ref,
         r_o, ld_o, k_o, v_o, kk_o, a_o, g_o, sh_scr) = refs
    else:
        (u_ref, up_ref, mu_ref, w0_ref, w2_ref, a0_ref, a2_ref, g2_ref, kk_ref, ka_ref,
         vf_ref, v0_ref, v1_ref, v2_ref,
         r_o, ld_o, k_o, v_o, kk_o, a_o, g_o, sh_scr) = refs
    ti = pl.program_id(1)
    u = u_ref[0]
    prev = jnp.where(ti > 0, up_ref[0], 0.0)
    sh_scr[0:8, :] = prev
    sh_scr[8:8 + tt, :] = u
    us = sh_scr[7:7 + tt, :]
    x = u + (us - u) * mu_ref[...]
    r = x[:, 0:D_RWKV]
    k = x[:, D_RWKV:2 * D_RWKV]
    v = x[:, 2 * D_RWKV:3 * D_RWKV]
    xw = x[:, RW_XW:RW_XA]
    xa = x[:, RW_XA:RW_XG]
    xg = x[:, RW_XG:RW_COLS]
    w = -jax.nn.softplus(-(w0_ref[...] + _dot(jnp.tanh(xw).astype(BF16), w2_ref[...]))) - 0.5
    ld_o[0] = -jnp.exp(w)
    a = jax.nn.sigmoid(a0_ref[...] + _dot(xa.astype(BF16), a2_ref[...]))
    g_o[0] = _dot(jax.nn.sigmoid(xg).astype(BF16), g2_ref[...])
    if not first_layer:
        lo = _dot(v.astype(BF16), v1_ref[...])
        gate = jax.nn.sigmoid(v0_ref[...] + _dot(lo.astype(BF16), v2_ref[...]))
        v = v + (vf_ref[0] - v) * gate
    r_o[0] = r
    v_o[0] = v
    a_o[0] = a
    kk_o[0] = k * kk_ref[...]
    k_o[0] = k * (1.0 + (a - 1.0) * ka_ref[...])


def _rwkv_prep_call(u_rw, p, v_first, tt):
    B, T, _ = u_rw.shape
    first_layer = v_first is None
    row = lambda n: pl.BlockSpec((1, n), lambda b, t: (0, 0))
    full = lambda a: pl.BlockSpec(a.shape, lambda b, t: (0, 0))
    tile = pl.BlockSpec((1, tt, D_RWKV), lambda b, t: (b, t, 0))
    in_specs = [
        pl.BlockSpec((1, tt, RW_COLS), lambda b, t: (b, t, 0)),
        pl.BlockSpec((1, 8, RW_COLS), lambda b, t: (b, jnp.maximum(t * (tt // 8) - 1, 0), 0)),
        row(RW_COLS), row(D_RWKV), full(p["w2"]), row(D_RWKV), full(p["a2"]), full(p["g2"]),
        row(D_RWKV), row(D_RWKV),
    ]
    args = [u_rw, u_rw, p["mu"], p["w0"], p["w2"], p["a0"], p["a2"], p["g2"], p["k_k"], p["k_a"]]
    if not first_layer:
        in_specs += [tile, row(D_RWKV), full(p["v1"]), full(p["v2"])]
        args += [v_first, p["v0"], p["v1"], p["v2"]]
    out = jax.ShapeDtypeStruct((B, T, D_RWKV), F32)
    return pl.pallas_call(
        functools.partial(_rwkv_prep_kernel, tt=tt, first_layer=first_layer),
        out_shape=[out] * 7,
        grid=(B, T // tt),
        in_specs=in_specs,
        out_specs=[tile] * 7,
        scratch_shapes=[pltpu.VMEM((tt + 8, RW_COLS), F32)],
        compiler_params=_cparams(("parallel", "parallel")),
        name="rwkv_prep",
    )(*args)


def _rwkv_scan_kernel(r_ref, ld_ref, k_ref, v_ref, kk_ref, a_ref, g_ref, rk_ref, lg_ref, lb_ref,
                      o_ref, s_scr, *, ts, prec):
    C = CHUNK
    nc = ts // C
    nb = 2 * nc
    N = HEAD_DIM

    @pl.when(pl.program_id(2) == 0)
    def _():
        s_scr[...] = jnp.zeros_like(s_scr)

    def split(ref):
        x = ref[0].reshape(nc, C, LANES)
        return jnp.concatenate([x[:, :, :N], x[:, :, N:]], axis=0)

    def split_row(ref):
        x = ref[...]
        return jnp.concatenate([jnp.broadcast_to(x[None, :, :N], (nc, 1, N)),
                                jnp.broadcast_to(x[None, :, N:], (nc, 1, N))], axis=0)

    r, ld, k, v, kkr, a, g = (split(z) for z in (r_ref, ld_ref, k_ref, v_ref, kk_ref, a_ref, g_ref))
    rk, lg, lb = (split_row(z) for z in (rk_ref, lg_ref, lb_ref))

    kkn = kkr / jnp.maximum(jnp.sqrt(jnp.sum(kkr * kkr, axis=-1, keepdims=True)), 1e-12)
    row = lax.broadcasted_iota(jnp.int32, (C, C), 0)
    col = lax.broadcasted_iota(jnp.int32, (C, C), 1)
    incl = col <= row
    strict = col < row
    eye = col == row
    ltri = jnp.broadcast_to(jnp.where(incl, 1.0, 0.0).astype(F32), (nb, C, C))
    m_a, m_inv, m_loc, m_st = prec
    cs = _bmm(ltri, ld, "hi")
    cs_last = cs[:, C - 1:C, :]
    e_in = jnp.exp(cs)
    e_ex = jnp.exp(cs - ld)
    e_neg = jnp.exp(-cs)
    e_hat = jnp.exp(cs_last - cs)
    at = -kkn * e_ex
    b = kkn * a
    bt = b * e_neg
    kt = k * e_neg
    rt = r * e_in
    bh = b * e_hat
    kh = k * e_hat
    A = _bmm_nt(jnp.concatenate([at, rt], axis=1), jnp.concatenate([bt, kt], axis=1), m_a)
    a_ab = jnp.where(strict, A[:, :C, :C], 0.0)
    a_ak = jnp.where(strict, A[:, :C, C:], 0.0)
    a_rb = jnp.where(incl, A[:, C:, :C], 0.0)
    a_rk = jnp.where(incl, A[:, C:, C:], 0.0)
    npow = a_ab
    tinv = jnp.where(eye, 1.0, 0.0).astype(F32) + a_ab
    p2 = 2
    while p2 < C:
        npow = _bmm(npow, npow, m_inv)
        tinv = tinv + _bmm(npow, tinv, m_inv)
        p2 *= 2
    akv = _bmm(a_ak, v, m_loc)
    x = _bmm(tinv, jnp.concatenate([at, akv], axis=-1), m_loc)
    y = _bmm(a_rb, x, m_loc)
    wr = rt + y[:, :, :N]
    o0 = y[:, :, N:] + _bmm(a_rk, v, m_loc)
    pq = _bmm_tn(x, bh, m_loc)
    pm = pq[:, :N, :] + jnp.where(eye, jnp.exp(cs_last), 0.0)
    qm = pq[:, N:, :] + _bmm_tn(v, kh, m_loc)

    outs = []
    for hh in range(2):
        S = s_scr[hh]
        for c in range(nc):
            i = hh * nc + c
            outs.append(_dot_nt(wr[i], S, m_st) + o0[i])
            S = _dot(S, pm[i], m_st) + qm[i]
        s_scr[hh] = S
    o = jnp.stack(outs, axis=0)

    mu = jnp.mean(o, axis=-1, keepdims=True)
    oc = o - mu
    var = jnp.mean(oc * oc, axis=-1, keepdims=True)
    on = oc * lax.rsqrt(var + GN_EPS) * lg + lb
    bonus = jnp.sum(r * k * rk, axis=-1, keepdims=True) * v
    res = (on + bonus) * g
    o_ref[0] = jnp.concatenate([res[:nc].reshape(ts, N), res[nc:].reshape(ts, N)], axis=-1)


def _rwkv_scan_call(r, ld, k, v, kk, a, g, p, ts, prec):
    B, T, _ = r.shape
    tile = pl.BlockSpec((1, ts, LANES), lambda b, h, t: (b, t, h))
    rowp = pl.BlockSpec((1, LANES), lambda b, h, t: (0, h))
    return pl.pallas_call(
        functools.partial(_rwkv_scan_kernel, ts=ts, prec=prec),
        out_shape=jax.ShapeDtypeStruct((B, T, D_RWKV), F32),
        grid=(B, RWKV_HEADS // 2, T // ts),
        in_specs=[tile] * 7 + [rowp] * 3,
        out_specs=tile,
        scratch_shapes=[pltpu.VMEM((2, HEAD_DIM, HEAD_DIM), F32)],
        compiler_params=_cparams(("parallel", "parallel", "arbitrary")),
        name="rwkv_scan",
    )(r, ld, k, v, kk, a, g, p["r_k"], p["lnx_g"], p["lnx_b"])


def _nsa_prep_kernel(u_ref, cos_ref, sin_ref, q_o, kc_o, vc_o, ks_o, vs_o, kw_o, vw_o, gate_o):
    cos = cos_ref[...]
    sin = sin_ref[...]
    lane = lax.broadcasted_iota(jnp.int32, cos.shape, 1)
    first_half = (lane % HEAD_DIM) < (HEAD_DIM // 2)

    def rope(x):
        other = jnp.where(first_half, pltpu.roll(x, LANES - HEAD_DIM // 2, 1), pltpu.roll(x, HEAD_DIM // 2, 1))
        return x * cos + other * sin

    def put(out_ref, col0, nheads, roped, scale=None):
        for j in range(nheads // 2):
            x = u_ref[0, :, col0 + j * LANES:col0 + (j + 1) * LANES]
            if roped:
                x = rope(x)
            if scale is not None:
                x = x * scale
            out_ref[0, 2 * j] = x[:, :HEAD_DIM].astype(out_ref.dtype)
            out_ref[0, 2 * j + 1] = x[:, HEAD_DIM:].astype(out_ref.dtype)

    def put_t(out_ref, col0, nheads):
        for j in range(nheads // 2):
            xt = u_ref[0, :, col0 + j * LANES:col0 + (j + 1) * LANES].T
            out_ref[0, 2 * j] = xt[:HEAD_DIM, :].astype(out_ref.dtype)
            out_ref[0, 2 * j + 1] = xt[HEAD_DIM:, :].astype(out_ref.dtype)

    put(q_o, 0, NSA_HEADS, True, HEAD_DIM ** -0.5 * LOG2_E)
    put(kc_o, NS_KC, NSA_KV_HEADS, True)
    put(vc_o, NS_KC + NSA_KV, NSA_KV_HEADS, False)
    put(ks_o, NS_KC + 2 * NSA_KV, NSA_KV_HEADS, True)
    put_t(vs_o, NS_KC + 3 * NSA_KV, NSA_KV_HEADS)
    put(kw_o, NS_KC + 4 * NSA_KV, NSA_KV_HEADS, True)
    put_t(vw_o, NS_KC + 5 * NSA_KV, NSA_KV_HEADS)
    gates_t = jax.nn.sigmoid(u_ref[0, :, NS_GATE:NS_GATE + LANES]).T
    for gi in range(NSA_KV_HEADS):
        gate_o[0, gi] = gates_t[gi * GATE_PAD:(gi + 1) * GATE_PAD, :]


def _nsa_prep_call(u_ns, cos_t, sin_t, tt):
    B, T, _ = u_ns.shape
    G, H, N = NSA_KV_HEADS, NSA_HEADS, HEAD_DIM
    kv = lambda dt: jax.ShapeDtypeStruct((B, G, T, N), dt)
    kv_t = jax.ShapeDtypeStruct((B, G, N, T), BF16)
    kv_spec = pl.BlockSpec((1, G, tt, N), lambda b, t: (b, 0, t, 0))
    kvt_spec = pl.BlockSpec((1, G, N, tt), lambda b, t: (b, 0, 0, t))
    return pl.pallas_call(
        _nsa_prep_kernel,
        out_shape=[jax.ShapeDtypeStruct((B, H, T, N), BF16), kv(F32), kv(F32), kv(BF16), kv_t,
                   kv(BF16), kv_t, jax.ShapeDtypeStruct((B, G, GATE_PAD, T), F32)],
        grid=(B, T // tt),
        in_specs=[
            pl.BlockSpec((1, tt, NS_COLS), lambda b, t: (b, t, 0)),
            pl.BlockSpec((tt, LANES), lambda b, t: (t, 0)),
            pl.BlockSpec((tt, LANES), lambda b, t: (t, 0)),
        ],
        out_specs=[pl.BlockSpec((1, H, tt, N), lambda b, t: (b, 0, t, 0)), kv_spec, kv_spec, kv_spec, kvt_spec,
                   kv_spec, kvt_spec, pl.BlockSpec((1, G, GATE_PAD, tt), lambda b, t: (b, 0, 0, t))],
        compiler_params=_cparams(("parallel", "parallel")),
        name="nsa_prep",
    )(u_ns, cos_t, sin_t)


def _compress_kernel(x_ref, pos_ref, w1_ref, w2_ref, o_ref):
    x = x_ref[0, 0]
    half = x.shape[1]
    y_top = _dot(x + pos_ref[0, 0:1, :], w1_ref[0, :half, :], "hi")
    y_bot = _dot(x + pos_ref[0, 1:2, :], w1_ref[0, half:, :], "hi")
    pre = y_top + pltpu.roll(y_bot, x.shape[0] - 1, 0)
    o_ref[0, 0] = _dot(jax.nn.gelu(pre), w2_ref[0], "hi")


def _compress_call(kv, pos, w1, w2, which):
    B, G, T, N = kv.shape
    n_half = T // CMP_STRIDE
    width = CMP_STRIDE * N
    hidden = w1.shape[-1]
    x = kv.reshape(B, G, n_half, width)
    return pl.pallas_call(
        _compress_kernel,
        out_shape=jax.ShapeDtypeStruct((B, G, n_half, N), F32),
        grid=(B, G),
        in_specs=[
            pl.BlockSpec((1, 1, n_half, width), lambda b, g: (b, g, 0, 0)),
            pl.BlockSpec((1, 2, width), lambda b, g: (which, 0, 0)),
            pl.BlockSpec((1, 2 * width, hidden), lambda b, g: (which, 0, 0)),
            pl.BlockSpec((1, hidden, N), lambda b, g: (which, 0, 0)),
        ],
        out_specs=pl.BlockSpec((1, 1, n_half, N), lambda b, g: (b, g, 0, 0)),
        compiler_params=_cparams(("parallel", "parallel")),
        name="nsa_compress",
    )(x, pos.reshape(2, 2, width), w1, w2)


def _nsa_attn_kernel(q_ref, kc_ref, vc_ref, ks_ref, vst_ref, kw_ref, vwt_ref, gate_ref, ov_ref, ext_ref,
                     o_ref, bias_scr, *, tq, tk):
    HPG, N = HEADS_PER_GROUP, HEAD_DIM
    qi = pl.program_id(2)
    q0 = qi * tq

    n_cmp = kc_ref.shape[2]
    kc = kc_ref[0, 0].astype(BF16)
    vc_t = vc_ref[0, 0].T.astype(BF16)
    c_c = lax.broadcasted_iota(jnp.int32, (n_cmp, tq), 0)
    t_c = q0 + lax.broadcasted_iota(jnp.int32, (n_cmp, tq), 1)
    cmask = c_c * CMP_STRIDE + (CMP_BLOCK - 1) <= t_c
    o_cmp = []
    psum = None
    for h in range(HPG):
        s = jnp.where(cmask, _dot_nt(kc, q_ref[0, h]), NEG)
        m = jnp.max(s, axis=0, keepdims=True)
        e = jnp.where(cmask, jnp.exp2(s - m), 0.0)
        l = jnp.sum(e, axis=0, keepdims=True)
        p = e / jnp.maximum(l, 1e-30)
        o_cmp.append(_dot(vc_t, p.astype(BF16)))
        psum = p if psum is None else psum + p

    n_sel = ov_ref.shape[0]
    imp = _dot(ov_ref[...], psum, "hi")
    blk = lax.broadcasted_iota(jnp.int32, (n_sel, tq), 0)
    t_s = q0 + lax.broadcasted_iota(jnp.int32, (n_sel, tq), 1)
    cur = t_s // SEL_BLOCK
    forced = (blk == 0) | (blk == cur) | (blk == cur - 1)
    val = jnp.where(forced, FORCE, jnp.where(blk * SEL_BLOCK <= t_s, imp, -1.0))
    rank = jnp.zeros((n_sel, tq), F32)
    for i in range(n_sel):
        vi = val[i:i + 1, :]
        ahead = (vi > val) | ((vi == val) & (blk > i))
        rank = rank + jnp.where(ahead, 1.0, 0.0)
    unsel_t = jnp.where(rank < float(min(N_SELECT, n_sel)), 0.0, NEG)
    bias_scr[...] = _dot(ext_ref[...], unsel_t.astype(BF16))

    rel = lax.broadcasted_iota(jnp.int32, (tk, tq), 1) - lax.broadcasted_iota(jnp.int32, (tk, tq), 0)
    wq = HPG * tq
    init = (jnp.full((1, wq), NEG, F32), jnp.zeros((1, wq), F32), jnp.zeros((N, wq), F32))
    q4 = q_ref[0].reshape(wq, N)

    def tile(carry, k_ref, vt_ref, k0, bias):
        m_i, l_i, acc = carry
        s = _dot_nt(k_ref[0, 0, pl.ds(k0, tk), :], q4)
        if bias is not None:
            s = s + (bias if bias.ndim == 0 else jnp.concatenate([bias] * HPG, axis=1))
        m_n = jnp.maximum(m_i, jnp.max(s, axis=0, keepdims=True))
        alpha = jnp.exp2(m_i - m_n)
        p = jnp.exp2(s - m_n)
        l_n = alpha * l_i + jnp.sum(p, axis=0, keepdims=True)
        return m_n, l_n, alpha * acc + _dot(vt_ref[0, 0, :, pl.ds(k0, tk)], p.astype(BF16))

    def split(carry):
        o = carry[2] / carry[1]
        return [o[:, h * tq:(h + 1) * tq] for h in range(HPG)]

    def sel_tile(carries, j):
        k0 = pl.multiple_of(j * tk, tk)
        return tile(carries, ks_ref, vst_ref, k0, bias_scr[pl.ds(k0, tk), :])

    n_off = qi * (tq // tk)
    carries = lax.fori_loop(0, n_off // 2, lambda jj, c: sel_tile(sel_tile(c, 2 * jj), 2 * jj + 1), init)
    carries = lax.cond(n_off % 2 == 1, lambda c: sel_tile(c, n_off - 1), lambda c: c, carries)
    for e in range(tq // tk):
        k0 = pl.multiple_of(q0 + e * tk, tk)
        causal = jnp.where(rel - e * tk >= 0, 0.0, NEG)
        carries = tile(carries, ks_ref, vst_ref, k0, bias_scr[pl.ds(k0, tk), :] + causal)
    o_slc = split(carries)

    carries = init
    for e in range((WINDOW + tq) // tk):
        koff = e * tk - WINDOW
        lo, hi = -(tk - 1) - koff, (tq - 1) - koff
        if lo >= WINDOW or hi < 0:
            continue
        bias = None
        if lo < 0 or hi >= WINDOW:
            dist = rel - koff
            bias = jnp.where((dist >= 0) & (dist < WINDOW), 0.0, NEG)
        if koff < 0:
            bias = jnp.where(q0 + koff >= 0, 0.0 if bias is None else bias, NEG)
        k0 = pl.multiple_of(jnp.maximum(q0 + koff, 0), tk)
        carries = tile(carries, kw_ref, vwt_ref, k0, bias)
    o_win = split(carries)

    gate = gate_ref[0, 0]
    outs = [gate[3 * h:3 * h + 1, :] * o_cmp[h] + gate[3 * h + 1:3 * h + 2, :] * o_slc[h]
            + gate[3 * h + 2:3 * h + 3, :] * o_win[h] for h in range(HPG)]
    o_ref[0] = jnp.concatenate(outs, axis=0).T


def _nsa_attn_call(q, kc, vc, ks, vs_t, kw, vw_t, gates_t, ov_t, ex_t, tq, tk):
    B, H, T, N = q.shape
    G, HPG = NSA_KV_HEADS, HEADS_PER_GROUP
    assert WINDOW % tk == 0 and tq % tk == 0
    n_half = kc.shape[2]
    cmp_spec = pl.BlockSpec((1, 1, n_half, N), lambda b, g, t: (b, g, 0, 0))
    kv_spec = pl.BlockSpec((1, 1, T, N), lambda b, g, t: (b, g, 0, 0))
    kvt_spec = pl.BlockSpec((1, 1, N, T), lambda b, g, t: (b, g, 0, 0))
    return pl.pallas_call(
        functools.partial(_nsa_attn_kernel, tq=tq, tk=tk),
        out_shape=jax.ShapeDtypeStruct((B, T, D_NSA), F32),
        grid=(B, G, T // tq),
        in_specs=[
            pl.BlockSpec((1, HPG, tq, N), lambda b, g, t: (b, g, t, 0)),
            cmp_spec, cmp_spec, kv_spec, kvt_spec, kv_spec, kvt_spec,
            pl.BlockSpec((1, 1, GATE_PAD, tq), lambda b, g, t: (b, g, 0, t)),
            pl.BlockSpec(ov_t.shape, lambda b, g, t: (0, 0)),
            pl.BlockSpec(ex_t.shape, lambda b, g, t: (0, 0)),
        ],
        out_specs=pl.BlockSpec((1, tq, HPG * N), lambda b, g, t: (b, t, g)),
        scratch_shapes=[pltpu.VMEM((T, tq), F32)],
        compiler_params=_cparams(("parallel", "parallel", "arbitrary")),
        name="nsa_attn",
    )(q, kc, vc, ks, vs_t, kw, vw_t, gates_t, ov_t, ex_t)


def _outproj_kernel(orw_ref, ons_ref, x_ref, gt_ref, w_ref, g_ref, b_ref, o_ref, *, alpha):
    half = orw_ref.shape[-1]
    y = _dot(orw_ref[0].astype(BF16), w_ref[:half, :]) + _dot(ons_ref[0].astype(BF16), w_ref[half:, :])
    z = alpha * x_ref[0] + (1.0 + gt_ref[0]) * y
    o_ref[0] = _layer_norm_rows(z, g_ref[...], b_ref[...])


def _outproj_call(o_rw, o_ns, x, mod3, row0, w_bf16, ln_g, ln_b, alpha, tm):
    B, T, D = x.shape
    half = o_rw.shape[-1]
    return pl.pallas_call(
        functools.partial(_outproj_kernel, alpha=alpha),
        out_shape=jax.ShapeDtypeStruct((B, T, D), F32),
        grid=(B, T // tm),
        in_specs=[
            pl.BlockSpec((1, tm, half), lambda b, m: (b, m, 0)),
            pl.BlockSpec((1, tm, half), lambda b, m: (b, m, 0)),
            pl.BlockSpec((1, tm, D), lambda b, m: (b, m, 0)),
            pl.BlockSpec((1, 1, D), lambda b, m: (row0 + 6 * b + 2, 0, 0)),
            pl.BlockSpec(w_bf16.shape, lambda b, m: (0, 0)),
            pl.BlockSpec((1, D), lambda b, m: (0, 0)),
            pl.BlockSpec((1, D), lambda b, m: (0, 0)),
        ],
        out_specs=pl.BlockSpec((1, tm, D), lambda b, m: (b, m, 0)),
        compiler_params=_cparams(("parallel", "parallel")),
        name="outproj_ln",
    )(o_rw, o_ns, x, mod3, w_bf16, ln_g, ln_b)


def _mlp_kernel(x_ref, sc_ref, sh_ref, gt_ref, w1_ref, w2_ref, g_ref, b_ref, o_ref, h_scr, acc_scr, *, alpha):
    f = pl.program_id(2)

    @pl.when(f == 0)
    def _():
        h = x_ref[0] * (1.0 + sc_ref[0]) + sh_ref[0]
        h_scr[...] = h.astype(BF16)
        acc_scr[...] = jnp.zeros_like(acc_scr)

    a = jnp.maximum(_dot(h_scr[...], w1_ref[...]), 0.0)
    acc_scr[...] += _dot((a * a).astype(BF16), w2_ref[...])

    @pl.when(f == pl.num_programs(2) - 1)
    def _():
        z = alpha * x_ref[0] + (1.0 + gt_ref[0]) * acc_scr[...]
        o_ref[0] = _layer_norm_rows(z, g_ref[...], b_ref[...])


def _mlp_call(x, mod3, row0, w1_bf16, w2_bf16, ln_g, ln_b, alpha, tm, tf):
    B, T, D = x.shape
    FF = w1_bf16.shape[1]
    modspec = lambda j: pl.BlockSpec((1, 1, D), lambda b, m, f: (row0 + 6 * b + j, 0, 0))
    return pl.pallas_call(
        functools.partial(_mlp_kernel, alpha=alpha),
        out_shape=jax.ShapeDtypeStruct((B, T, D), F32),
        grid=(B, T // tm, FF // tf),
        in_specs=[
            pl.BlockSpec((1, tm, D), lambda b, m, f: (b, m, 0)),
            modspec(4), modspec(3), modspec(5),
            pl.BlockSpec((D, tf), lambda b, m, f: (0, f)),
            pl.BlockSpec((tf, D), lambda b, m, f: (f, 0)),
            pl.BlockSpec((1, D), lambda b, m, f: (0, 0)),
            pl.BlockSpec((1, D), lambda b, m, f: (0, 0)),
        ],
        out_specs=pl.BlockSpec((1, tm, D), lambda b, m, f: (b, m, 0)),
        scratch_shapes=[pltpu.VMEM((tm, D), BF16), pltpu.VMEM((tm, D), F32)],
        compiler_params=_cparams(("parallel", "parallel", "arbitrary")),
        name="mlp_ln",
    )(x, mod3, mod3, mod3, w1_bf16, w2_bf16, ln_g, ln_b)


def _pad_cols(w, n):
    return jnp.pad(w, ((0, 0), (0, n - w.shape[1])))


def _pad_rows(w, n):
    return jnp.pad(w, ((0, n - w.shape[0]), (0, 0)))


def _split_w_in(w_in):
    c = np.cumsum([0, D_RWKV, D_RWKV, D_RWKV, DECAY_LORA, AAA_LORA, GATE_LORA]).tolist()
    rw = jnp.concatenate([w_in[:, c[0]:c[3]], _pad_cols(w_in[:, c[3]:c[4]], LORA_PAD),
                          _pad_cols(w_in[:, c[4]:c[5]], LORA_PAD), w_in[:, c[5]:c[6]]], axis=1)
    ng = 3 * HEADS_PER_GROUP
    gates = w_in[:, c[6] + NS_GATE:].reshape(-1, NSA_KV_HEADS, ng)
    gates = jnp.pad(gates, ((0, 0), (0, 0), (0, GATE_PAD - ng))).reshape(-1, NSA_KV_HEADS * GATE_PAD)
    ns = jnp.concatenate([w_in[:, c[6]:c[6] + NS_GATE], _pad_cols(gates, LANES)], axis=1)
    return rw.astype(BF16), ns.astype(BF16)


def _pad_mu(mu):
    c = np.cumsum([0, 3 * D_RWKV, DECAY_LORA, AAA_LORA, GATE_LORA]).tolist()
    parts = [mu[c[0]:c[1]], jnp.pad(mu[c[1]:c[2]], (0, LORA_PAD - DECAY_LORA)),
             jnp.pad(mu[c[2]:c[3]], (0, LORA_PAD - AAA_LORA)), mu[c[3]:c[4]]]
    return jnp.concatenate(parts)[None, :]


def _rope_tables(T):
    half = HEAD_DIM // 2
    inv = ROPE_THETA ** (-jnp.arange(half, dtype=F32) / half)
    ang = jnp.arange(T, dtype=F32)[:, None] * inv[None]
    cos, sin = jnp.cos(ang), jnp.sin(ang)
    cos_t = jnp.tile(cos, (1, LANES // half))
    sin_t = jnp.tile(jnp.concatenate([-sin, sin], axis=1), (1, LANES // HEAD_DIM))
    return cos_t, sin_t


def _selection_constants(T):
    n_half = T // CMP_STRIDE
    n_cmp = (T - CMP_BLOCK) // CMP_STRIDE + 1
    n_sel = T // SEL_BLOCK
    pos = np.arange(n_cmp)[:, None] * CMP_STRIDE + np.arange(CMP_BLOCK)[None]
    ov = ((pos // SEL_BLOCK)[..., None] == np.arange(n_sel)).sum(1) / CMP_BLOCK
    ov_t = np.zeros((n_sel, n_half), np.float32)
    ov_t[:, :n_cmp] = ov.T
    ex_t = (np.arange(T)[:, None] // SEL_BLOCK == np.arange(n_sel)[None, :]).astype(np.float32)
    return jnp.asarray(ov_t), jnp.asarray(ex_t, BF16)


SCAN_PREC = ("bf", "bf", "bf", "bf")


def kernel(x, c, w_ada, b_ada, w_in, rwkv_mu, rwkv_w0, rwkv_w2, rwkv_a0, rwkv_a2, rwkv_g2, rwkv_k_k, rwkv_k_a, rwkv_r_k, rwkv_lnx_g, rwkv_lnx_b, rwkv_v0, rwkv_v1, rwkv_v2, nsa_cmp_pos, nsa_cmp_w1, nsa_cmp_w2, w_out, ln1_g, ln1_b, mlp_w1, mlp_w2, ln2_g, ln2_b):
    B, T, D = x.shape
    L = w_ada.shape[0]
    alpha = (2 * L) ** 0.25

    c_pad = jnp.pad(c, ((0, 8 - B % 8 if B % 8 else 0), (0, 0)))
    mod = _ada_call(c_pad, w_ada, b_ada)[:, :B]
    mod3 = mod.reshape(L * B * 6, 1, D)
    cos_t, sin_t = _rope_tables(T)
    ov_t, ex = _selection_constants(T)
    row = lambda z: z.reshape(1, -1)

    v_first = None
    for i in range(L):
        row0 = i * B * 6
        w_rw, w_ns = _split_w_in(w_in[i])
        u_rw = _modmm_call(x, mod3, row0, w_rw, 512, PROJ_TN, "inproj_rwkv")
        u_ns = _modmm_call(x, mod3, row0, w_ns, 512, PROJ_TN, "inproj_nsa")

        p = {
            "mu": _pad_mu(rwkv_mu[i]), "w0": row(rwkv_w0[i]), "a0": row(rwkv_a0[i]),
            "w2": _pad_rows(rwkv_w2[i], LORA_PAD).astype(BF16),
            "a2": _pad_rows(rwkv_a2[i], LORA_PAD).astype(BF16),
            "g2": rwkv_g2[i].astype(BF16), "k_k": row(rwkv_k_k[i]), "k_a": row(rwkv_k_a[i]),
            "r_k": row(rwkv_r_k[i]), "lnx_g": row(rwkv_lnx_g[i]), "lnx_b": row(rwkv_lnx_b[i]),
        }
        if i > 0:
            p["v0"] = row(rwkv_v0[i - 1])
            p["v1"] = _pad_cols(rwkv_v1[i - 1], LORA_PAD).astype(BF16)
            p["v2"] = _pad_rows(rwkv_v2[i - 1], LORA_PAD).astype(BF16)
        r, ld, k, v, kk, a, g = _rwkv_prep_call(u_rw, p, v_first, 256)
        if i == 0:
            v_first = v
        o_rw = _rwkv_scan_call(r, ld, k, v, kk, a, g, p, 256, SCAN_PREC)

        q, kc_in, vc_in, ks, vs, kw, vw, gates = _nsa_prep_call(u_ns, cos_t, sin_t, 256)
        kc = _compress_call(kc_in, nsa_cmp_pos[i], nsa_cmp_w1[i], nsa_cmp_w2[i], 0)
        vc = _compress_call(vc_in, nsa_cmp_pos[i], nsa_cmp_w1[i], nsa_cmp_w2[i], 1)
        o_ns = _nsa_attn_call(q, kc, vc, ks, vs, kw, vw, gates, ov_t, ex, 256, 256)

        x = _outproj_call(o_rw, o_ns, x, mod3, row0, w_out[i].astype(BF16), row(ln1_g[i]), row(ln1_b[i]),
                          alpha, 256)
        x = _mlp_call(x, mod3, row0, mlp_w1[i].astype(BF16), mlp_w2[i].astype(BF16), row(ln2_g[i]),
                      row(ln2_b[i]), alpha, 512, 512)
    return x
```

```python
import functools

import numpy as np
import jax
import jax.numpy as jnp
from jax import lax
from jax.experimental import pallas as pl
from jax.experimental.pallas import tpu as pltpu

F32 = jnp.float32
BF16 = jnp.bfloat16
HI = lax.Precision.HIGHEST

HEAD_DIM = 64
RWKV_HEADS = 16
NSA_HEADS = 16
NSA_KV_HEADS = 4
HEADS_PER_GROUP = NSA_HEADS // NSA_KV_HEADS
D_RWKV = RWKV_HEADS * HEAD_DIM
D_NSA = NSA_HEADS * HEAD_DIM
NSA_KV = NSA_KV_HEADS * HEAD_DIM
DECAY_LORA = 96
AAA_LORA = 96
MV_LORA = 64
GATE_LORA = 256
GN_EPS = 64e-5
CMP_BLOCK = 32
CMP_STRIDE = 16
SEL_BLOCK = 64
N_SELECT = 8
WINDOW = 512
ROPE_THETA = 10000.0
NEG = -1e30
FORCE = 1e4
LN_EPS = 1e-5
LOG2_E = 1.4426950408889634

LANES = 128
LORA_PAD = LANES
RW_XW = 3 * D_RWKV
RW_XA = RW_XW + LORA_PAD
RW_XG = RW_XA + LORA_PAD
RW_COLS = RW_XG + GATE_LORA
NS_KC = D_NSA
NS_GATE = D_NSA + 6 * NSA_KV
NS_COLS = NS_GATE + LANES
GATE_PAD = 16
PROJ_TN = 896

CHUNK = 64
V7X_VMEM_BYTES = 64 * 1024 * 1024
VMEM_LIMIT = V7X_VMEM_BYTES - 8 * 1024 * 1024
TILES = dict(proj_m=512, prep=256, scan_rows=128, scan_heads=8, attn_q=256, attn_k=256, out_m=256, mlp_m=512,
             mlp_f=1024)


def _layer_spec(arr, layer):
    tail = (0,) * (arr.ndim - 1)
    return pl.BlockSpec((1,) + arr.shape[1:], lambda *_: (layer,) + tail)


def _cparams(sem):
    return pltpu.CompilerParams(dimension_semantics=sem, vmem_limit_bytes=VMEM_LIMIT)


def _mm(fn, a, b, mode):
    if mode is None:
        return fn(a, b, None)
    if mode == "hi":
        return fn(a, b, HI)
    ah, bh = a.astype(BF16), b.astype(BF16)
    if mode == "bf":
        return fn(ah, bh, None)
    if mode == "r3":
        r1 = b - bh.astype(F32)
        bm = r1.astype(BF16)
        bl = (r1 - bm.astype(F32)).astype(BF16)
        return fn(ah, bh, None) + (fn(ah, bm, None) + fn(ah, bl, None))
    al = (a - ah.astype(F32)).astype(BF16)
    bl = (b - bh.astype(F32)).astype(BF16)
    return fn(ah, bh, None) + (fn(ah, bl, None) + fn(al, bh, None))


def _dot(a, b, mode=None):
    return _mm(lambda p, q, pr: jnp.dot(p, q, preferred_element_type=F32, precision=pr), a, b, mode)


def _dot_nt(a, b, mode=None):
    return _mm(lambda p, q, pr: lax.dot_general(p, q, (((1,), (1,)), ((), ())), preferred_element_type=F32,
                                               precision=pr), a, b, mode)


def _bmm(a, b, mode=None):
    return _mm(lambda p, q, pr: jnp.einsum("bij,bjk->bik", p, q, preferred_element_type=F32, precision=pr),
               a, b, mode)


def _bmm_nt(a, b, mode=None):
    return _mm(lambda p, q, pr: jnp.einsum("bik,bjk->bij", p, q, preferred_element_type=F32, precision=pr),
               a, b, mode)


def _bmm_tn(a, b, mode=None):
    return _mm(lambda p, q, pr: jnp.einsum("bci,bcj->bij", p, q, preferred_element_type=F32, precision=pr),
               a, b, mode)


def _layer_norm_rows(z, g, b):
    mu = jnp.mean(z, axis=-1, keepdims=True)
    zc = z - mu
    var = jnp.mean(zc * zc, axis=-1, keepdims=True)
    return zc * lax.rsqrt(var + LN_EPS) * g + b


def _ada_kernel(c_ref, w_ref, b_ref, o_ref):
    c = c_ref[...]
    cond = c * jax.nn.sigmoid(c)
    o_ref[0] = _dot(cond.astype(BF16), w_ref[0].astype(BF16)) + b_ref[0]


def _ada_call(c_pad, w_ada, b_ada):
    L, D, N = w_ada.shape
    tn = 1024
    return pl.pallas_call(
        _ada_kernel,
        out_shape=jax.ShapeDtypeStruct((L, c_pad.shape[0], N), F32),
        grid=(L, N // tn),
        in_specs=[
            pl.BlockSpec(c_pad.shape, lambda l, n: (0, 0)),
            pl.BlockSpec((1, D, tn), lambda l, n: (l, 0, n)),
            pl.BlockSpec((1, 1, tn), lambda l, n: (l, 0, n)),
        ],
        out_specs=pl.BlockSpec((1, c_pad.shape[0], tn), lambda l, n: (l, 0, n)),
        compiler_params=_cparams(("parallel", "parallel")),
        name="adaln_mod",
    )(c_pad, w_ada, b_ada.reshape(L, 1, N))


def _modmm_kernel(x_ref, sc_ref, sh_ref, w_ref, o_ref, h_scr):
    @pl.when(pl.program_id(2) == 0)
    def _():
        h = x_ref[0] * (1.0 + sc_ref[0]) + sh_ref[0]
        h_scr[...] = h.astype(BF16)

    o_ref[0] = _dot(h_scr[...], w_ref[0])


def _modmm_call(x, mod3, row0, w_bf16, layer, tm, tn, name):
    B, T, D = x.shape
    N = w_bf16.shape[2]
    return pl.pallas_call(
        _modmm_kernel,
        out_shape=jax.ShapeDtypeStruct((B, T, N), F32),
        grid=(B, T // tm, N // tn),
        in_specs=[
            pl.BlockSpec((1, tm, D), lambda b, m, n: (b, m, 0)),
            pl.BlockSpec((1, 1, D), lambda b, m, n: (row0 + 6 * b + 1, 0, 0)),
            pl.BlockSpec((1, 1, D), lambda b, m, n: (row0 + 6 * b, 0, 0)),
            pl.BlockSpec((1, D, tn), lambda b, m, n: (layer, 0, n)),
        ],
        out_specs=pl.BlockSpec((1, tm, tn), lambda b, m, n: (b, m, n)),
        scratch_shapes=[pltpu.VMEM((tm, D), BF16)],
        compiler_params=_cparams(("parallel", "parallel", "arbitrary")),
        name=name,
    )(x, mod3, mod3, w_bf16)


def _rwkv_prep_kernel(*refs, tt, first_layer):
    if first_layer:
        (u_ref, up_ref, mu_ref, w0_ref, w2_ref, a0_ref, a2_ref, g2_ref, kk_ref, ka_ref,
         r_o, ld_o, cs_o, k_o, v_o, kk_o, a_o, g_o, sh_scr) = refs
    else:
        (u_ref, up_ref, mu_ref, w0_ref, w2_ref, a0_ref, a2_ref, g2_ref, kk_ref, ka_ref,
         vf_ref, v0_ref, v1_ref, v2_ref,
         r_o, ld_o, cs_o, k_o, v_o, kk_o, a_o, g_o, sh_scr) = refs
    ti = pl.program_id(1)
    u = u_ref[0]
    prev = jnp.where(ti > 0, up_ref[0], 0.0)
    sh_scr[0:8, :] = prev
    sh_scr[8:8 + tt, :] = u
    us = sh_scr[7:7 + tt, :]
    x = u + (us - u) * mu_ref[0]
    r = x[:, 0:D_RWKV]
    k = x[:, D_RWKV:2 * D_RWKV]
    v = x[:, 2 * D_RWKV:3 * D_RWKV]
    xw = x[:, RW_XW:RW_XA]
    xa = x[:, RW_XA:RW_XG]
    xg = x[:, RW_XG:RW_COLS]
    w = -jax.nn.softplus(-(w0_ref[0] + _dot(jnp.tanh(xw).astype(BF16), w2_ref[0]))) - 0.5
    ld = -jnp.exp(w)
    ld_o[0] = ld
    row = lax.broadcasted_iota(jnp.int32, (tt, tt), 0)
    col = lax.broadcasted_iota(jnp.int32, (tt, tt), 1)
    chunk_ltri = jnp.where((row // CHUNK == col // CHUNK) & (col <= row), 1.0, 0.0)
    cs_o[0] = _dot(chunk_ltri, ld, "r3")
    a = jax.nn.sigmoid(a0_ref[0] + _dot(xa.astype(BF16), a2_ref[0]))
    g_o[0] = _dot(jax.nn.sigmoid(xg).astype(BF16), g2_ref[0])
    if not first_layer:
        lo = _dot(v.astype(BF16), v1_ref[0])
        gate = jax.nn.sigmoid(v0_ref[0] + _dot(lo.astype(BF16), v2_ref[0]))
        v = v + (vf_ref[0] - v) * gate
    r_o[0] = r
    v_o[0] = v
    a_o[0] = a
    kk_o[0] = k * kk_ref[0]
    k_o[0] = k * (1.0 + (a - 1.0) * ka_ref[0])


def _rwkv_prep_call(u_rw, p, layer, v_first, tt):
    B, T, _ = u_rw.shape
    first_layer = v_first is None
    tile = pl.BlockSpec((1, tt, D_RWKV), lambda b, t: (b, t, 0))
    names = ["mu", "w0", "w2", "a0", "a2", "g2", "k_k", "k_a"]
    in_specs = [
        pl.BlockSpec((1, tt, RW_COLS), lambda b, t: (b, t, 0)),
        pl.BlockSpec((1, 8, RW_COLS), lambda b, t: (b, jnp.maximum(t * (tt // 8) - 1, 0), 0)),
    ] + [_layer_spec(p[nm], layer) for nm in names]
    args = [u_rw, u_rw] + [p[nm] for nm in names]
    if not first_layer:
        in_specs += [tile] + [_layer_spec(p[nm], layer - 1) for nm in ("v0", "v1", "v2")]
        args += [v_first, p["v0"], p["v1"], p["v2"]]
    out = jax.ShapeDtypeStruct((B, T, D_RWKV), F32)
    return pl.pallas_call(
        functools.partial(_rwkv_prep_kernel, tt=tt, first_layer=first_layer),
        out_shape=[out] * 8,
        grid=(B, T // tt),
        in_specs=in_specs,
        out_specs=[tile] * 8,
        scratch_shapes=[pltpu.VMEM((tt + 8, RW_COLS), F32)],
        compiler_params=_cparams(("parallel", "parallel")),
        name="rwkv_prep",
    )(*args)


def _rwkv_scan_kernel(r_ref, ld_ref, cs_ref, k_ref, v_ref, kk_ref, a_ref, g_ref, rk_ref, lg_ref, lb_ref,
                      o_ref, s_scr, wr_scr, o0_scr, pm_scr, qm_scr, bn_scr, g_scr, *, ts, nt, hpb):
    C = CHUNK
    nc = ts // C
    nb = hpb * nc
    N = HEAD_DIM
    n = pl.program_id(0)

    @pl.when(n == 0)
    def _():
        for scr in (s_scr, wr_scr, o0_scr, pm_scr, qm_scr, bn_scr, g_scr):
            scr[...] = jnp.zeros_like(scr)

    first = lax.rem(jnp.maximum(n - 1, 0), nt) == 0
    outs = []
    for hh in range(hpb):
        S = jnp.where(first, 0.0, s_scr[hh])
        for c in range(nc):
            i = hh * nc + c
            Sb = S.astype(BF16)
            outs.append(_dot_nt(wr_scr[i], Sb) + o0_scr[i])
            S = _dot(Sb, pm_scr[i]) + qm_scr[i]
        s_scr[hh] = S
    o = jnp.stack(outs, axis=0)

    def split_row(ref):
        x = ref[0]
        return jnp.concatenate([jnp.broadcast_to(x[None, :, h * N:(h + 1) * N], (nc, 1, N)) for h in range(hpb)],
                               axis=0)

    mu = jnp.mean(o, axis=-1, keepdims=True)
    oc = o - mu
    var = jnp.mean(oc * oc, axis=-1, keepdims=True)
    on = oc * lax.rsqrt(var + GN_EPS) * split_row(lg_ref) + split_row(lb_ref)
    res = (on + bn_scr[...]) * g_scr[...]
    o_ref[0] = jnp.concatenate([res[h * nc:(h + 1) * nc].reshape(ts, N) for h in range(hpb)], axis=-1)

    def split(ref):
        x = ref[0].reshape(nc, C, hpb * N)
        return jnp.concatenate([x[:, :, h * N:(h + 1) * N] for h in range(hpb)], axis=0)

    r, ld, cs, k, v, kkr, a, g = (split(z) for z in (r_ref, ld_ref, cs_ref, k_ref, v_ref, kk_ref, a_ref, g_ref))
    kkn = kkr / jnp.maximum(jnp.sqrt(jnp.sum(kkr * kkr, axis=-1, keepdims=True)), 1e-12)
    row = lax.broadcasted_iota(jnp.int32, (C, C), 0)
    col = lax.broadcasted_iota(jnp.int32, (C, C), 1)
    incl = col <= row
    strict = col < row
    eye = col == row
    cs_last = cs[:, C - 1:C, :]
    e_in = jnp.exp(cs)
    e_ex = jnp.exp(cs - ld)
    e_neg = jnp.exp(-cs)
    e_hat = jnp.exp(cs_last - cs)
    at = -kkn * e_ex
    b = kkn * a
    bt = b * e_neg
    kt = k * e_neg
    rt = r * e_in
    bh = (b * e_hat).astype(BF16)
    kh = (k * e_hat).astype(BF16)
    vb = v.astype(BF16)
    A = _bmm_nt(jnp.concatenate([at, rt], axis=1).astype(BF16), jnp.concatenate([bt, kt], axis=1).astype(BF16))
    a_ab = jnp.where(strict, A[:, :C, :C], 0.0)
    a_ak = jnp.where(strict, A[:, :C, C:], 0.0).astype(BF16)
    a_rb = jnp.where(incl, A[:, C:, :C], 0.0).astype(BF16)
    a_rk = jnp.where(incl, A[:, C:, C:], 0.0).astype(BF16)
    npow = a_ab.astype(BF16)
    tinv = jnp.where(eye, 1.0, 0.0).astype(F32) + a_ab
    p2 = 2
    while p2 < C:
        npow_f = _bmm(npow, npow)
        npow = npow_f.astype(BF16)
        tinv = tinv + _bmm(npow, tinv.astype(BF16))
        p2 *= 2
    akv = _bmm(a_ak, vb)
    x = _bmm(tinv.astype(BF16), jnp.concatenate([at, akv], axis=-1).astype(BF16))
    xb = x.astype(BF16)
    y = _bmm(a_rb, xb)
    pq = _bmm_tn(xb, bh)
    wr_scr[...] = (rt + y[:, :, :N]).astype(BF16)
    o0_scr[...] = y[:, :, N:] + _bmm(a_rk, vb)
    pm_scr[...] = (pq[:, :N, :] + jnp.where(eye, jnp.exp(cs_last), 0.0)).astype(BF16)
    qm_scr[...] = pq[:, N:, :] + _bmm_tn(vb, kh)
    bn_scr[...] = jnp.sum(r * k * split_row(rk_ref), axis=-1, keepdims=True) * v
    g_scr[...] = g


def _rwkv_scan_call(r, ld, cs, k, v, kk, a, g, p, layer, ts, hpb):
    B, T, _ = r.shape
    nt = T // ts
    hg = RWKV_HEADS // hpb
    nblk = B * hg * nt
    nb = hpb * (ts // CHUNK)
    w = hpb * HEAD_DIM

    def blk(n):
        return n // (hg * nt), lax.rem(n, nt), lax.rem(n // nt, hg)

    cur = lambda n: blk(jnp.minimum(n, nblk - 1))
    prev = lambda n: blk(jnp.maximum(n - 1, 0))
    tile = pl.BlockSpec((1, ts, w), cur)
    rowp = lambda f: pl.BlockSpec((1, 1, w), lambda n: (layer, 0, f(n)[2]))
    sq = lambda dt: pltpu.VMEM((nb, HEAD_DIM, HEAD_DIM), dt)
    return pl.pallas_call(
        functools.partial(_rwkv_scan_kernel, ts=ts, nt=nt, hpb=hpb),
        out_shape=jax.ShapeDtypeStruct((B, T, D_RWKV), F32),
        grid=(nblk + 1,),
        in_specs=[tile] * 8 + [rowp(cur), rowp(prev), rowp(prev)],
        out_specs=pl.BlockSpec((1, ts, w), prev),
        scratch_shapes=[pltpu.VMEM((hpb, HEAD_DIM, HEAD_DIM), F32), pltpu.VMEM((nb, CHUNK, HEAD_DIM), BF16),
                        pltpu.VMEM((nb, CHUNK, HEAD_DIM), F32), sq(BF16), sq(F32),
                        pltpu.VMEM((nb, CHUNK, HEAD_DIM), F32), pltpu.VMEM((nb, CHUNK, HEAD_DIM), F32)],
        compiler_params=_cparams(("arbitrary",)),
        name="rwkv_scan",
    )(r, ld, cs, k, v, kk, a, g, p["r_k"], p["lnx_g"], p["lnx_b"])


def _nsa_prep_kernel(u_ref, cos_ref, sin_ref, q_o, kc_o, vc_o, ks_o, vs_o, kw_o, vw_o, gate_o):
    cos = cos_ref[...]
    sin = sin_ref[...]
    lane = lax.broadcasted_iota(jnp.int32, cos.shape, 1)
    first_half = (lane % HEAD_DIM) < (HEAD_DIM // 2)

    def rope(x):
        other = jnp.where(first_half, pltpu.roll(x, LANES - HEAD_DIM // 2, 1), pltpu.roll(x, HEAD_DIM // 2, 1))
        return x * cos + other * sin

    def put(out_ref, col0, nheads, roped, scale=None):
        for j in range(nheads // 2):
            x = u_ref[0, :, col0 + j * LANES:col0 + (j + 1) * LANES]
            if roped:
                x = rope(x)
            if scale is not None:
                x = x * scale
            out_ref[0, 2 * j] = x[:, :HEAD_DIM].astype(out_ref.dtype)
            out_ref[0, 2 * j + 1] = x[:, HEAD_DIM:].astype(out_ref.dtype)

    def put_t(out_ref, col0, nheads):
        for j in range(nheads // 2):
            xt = u_ref[0, :, col0 + j * LANES:col0 + (j + 1) * LANES].T
            out_ref[0, 2 * j] = xt[:HEAD_DIM, :].astype(out_ref.dtype)
            out_ref[0, 2 * j + 1] = xt[HEAD_DIM:, :].astype(out_ref.dtype)

    put(q_o, 0, NSA_HEADS, True, HEAD_DIM ** -0.5 * LOG2_E)
    put(kc_o, NS_KC, NSA_KV_HEADS, True)
    put(vc_o, NS_KC + NSA_KV, NSA_KV_HEADS, False)
    put(ks_o, NS_KC + 2 * NSA_KV, NSA_KV_HEADS, True)
    put_t(vs_o, NS_KC + 3 * NSA_KV, NSA_KV_HEADS)
    put(kw_o, NS_KC + 4 * NSA_KV, NSA_KV_HEADS, True)
    put_t(vw_o, NS_KC + 5 * NSA_KV, NSA_KV_HEADS)
    gates_t = jax.nn.sigmoid(u_ref[0, :, NS_GATE:NS_GATE + LANES]).T
    for gi in range(NSA_KV_HEADS):
        gate_o[0, gi] = gates_t[gi * GATE_PAD:(gi + 1) * GATE_PAD, :]


def _nsa_prep_call(u_ns, cos_t, sin_t, tt):
    B, T, _ = u_ns.shape
    G, H, N = NSA_KV_HEADS, NSA_HEADS, HEAD_DIM
    kv = lambda dt: jax.ShapeDtypeStruct((B, G, T, N), dt)
    kv_t = jax.ShapeDtypeStruct((B, G, N, T), BF16)
    kv_spec = pl.BlockSpec((1, G, tt, N), lambda b, t: (b, 0, t, 0))
    kvt_spec = pl.BlockSpec((1, G, N, tt), lambda b, t: (b, 0, 0, t))
    return pl.pallas_call(
        _nsa_prep_kernel,
        out_shape=[jax.ShapeDtypeStruct((B, H, T, N), BF16), kv(F32), kv(F32), kv(BF16), kv_t,
                   kv(BF16), kv_t, jax.ShapeDtypeStruct((B, G, GATE_PAD, T), F32)],
        grid=(B, T // tt),
        in_specs=[
            pl.BlockSpec((1, tt, NS_COLS), lambda b, t: (b, t, 0)),
            pl.BlockSpec((tt, LANES), lambda b, t: (t, 0)),
            pl.BlockSpec((tt, LANES), lambda b, t: (t, 0)),
        ],
        out_specs=[pl.BlockSpec((1, H, tt, N), lambda b, t: (b, 0, t, 0)), kv_spec, kv_spec, kv_spec, kvt_spec,
                   kv_spec, kvt_spec, pl.BlockSpec((1, G, GATE_PAD, tt), lambda b, t: (b, 0, 0, t))],
        compiler_params=_cparams(("parallel", "parallel")),
        name="nsa_prep",
    )(u_ns, cos_t, sin_t)


def _compress_kernel(x_ref, pos_ref, w1_ref, w2_ref, o_ref):
    x = x_ref[0, 0]
    half = x.shape[1]
    y_top = _dot(x + pos_ref[0, 0:1, :], w1_ref[0, :half, :], "hi")
    y_bot = _dot(x + pos_ref[0, 1:2, :], w1_ref[0, half:, :], "hi")
    pre = y_top + pltpu.roll(y_bot, x.shape[0] - 1, 0)
    o_ref[0, 0] = _dot(jax.nn.gelu(pre), w2_ref[0], "hi")


def _compress_call(kv, pos, w1, w2, which):
    B, G, T, N = kv.shape
    n_half = T // CMP_STRIDE
    width = CMP_STRIDE * N
    hidden = w1.shape[-1]
    x = kv.reshape(B, G, n_half, width)
    return pl.pallas_call(
        _compress_kernel,
        out_shape=jax.ShapeDtypeStruct((B, G, n_half, N), F32),
        grid=(B, G),
        in_specs=[
            pl.BlockSpec((1, 1, n_half, width), lambda b, g: (b, g, 0, 0)),
            pl.BlockSpec((1, 2, width), lambda b, g: (which, 0, 0)),
            pl.BlockSpec((1, 2 * width, hidden), lambda b, g: (which, 0, 0)),
            pl.BlockSpec((1, hidden, N), lambda b, g: (which, 0, 0)),
        ],
        out_specs=pl.BlockSpec((1, 1, n_half, N), lambda b, g: (b, g, 0, 0)),
        compiler_params=_cparams(("parallel", "parallel")),
        name="nsa_compress",
    )(x, pos, w1, w2)


def _nsa_attn_kernel(q_ref, kc_ref, vc_ref, ks_ref, vst_ref, kw_ref, vwt_ref, gate_ref, ov_ref, ext_ref,
                     o_ref, bias_scr, *, tq, tk):
    HPG, N = HEADS_PER_GROUP, HEAD_DIM
    qi = pl.program_id(2)
    q0 = qi * tq

    n_cmp = kc_ref.shape[2]
    kc = kc_ref[0, 0].astype(BF16)
    vc_t = vc_ref[0, 0].T.astype(BF16)
    c_c = lax.broadcasted_iota(jnp.int32, (n_cmp, tq), 0)
    t_c = q0 + lax.broadcasted_iota(jnp.int32, (n_cmp, tq), 1)
    cmask = c_c * CMP_STRIDE + (CMP_BLOCK - 1) <= t_c
    o_cmp = []
    psum = None
    for h in range(HPG):
        s = jnp.where(cmask, _dot_nt(kc, q_ref[0, h]), NEG)
        m = jnp.max(s, axis=0, keepdims=True)
        e = jnp.where(cmask, jnp.exp2(s - m), 0.0)
        l = jnp.sum(e, axis=0, keepdims=True)
        p = e / jnp.maximum(l, 1e-30)
        o_cmp.append(_dot(vc_t, p.astype(BF16)))
        psum = p if psum is None else psum + p

    n_sel = ov_ref.shape[0]
    imp = _dot(ov_ref[...], psum, "hi")
    blk = lax.broadcasted_iota(jnp.int32, (n_sel, tq), 0)
    t_s = q0 + lax.broadcasted_iota(jnp.int32, (n_sel, tq), 1)
    cur = t_s // SEL_BLOCK
    forced = (blk == 0) | (blk == cur) | (blk == cur - 1)
    val = jnp.where(forced, FORCE, jnp.where(blk * SEL_BLOCK <= t_s, imp, -1.0))
    rank = jnp.zeros((n_sel, tq), F32)
    for i in range(n_sel):
        vi = val[i:i + 1, :]
        ahead = (vi > val) | ((vi == val) & (blk > i))
        rank = rank + jnp.where(ahead, 1.0, 0.0)
    unsel_t = jnp.where(rank < float(min(N_SELECT, n_sel)), 0.0, NEG)
    bias_scr[...] = _dot(ext_ref[...], unsel_t.astype(BF16))

    rel = lax.broadcasted_iota(jnp.int32, (tk, tq), 1) - lax.broadcasted_iota(jnp.int32, (tk, tq), 0)
    wq = HPG * tq
    init = (jnp.full((1, wq), NEG, F32), jnp.zeros((1, wq), F32), jnp.zeros((N, wq), F32))
    q4 = q_ref[0].reshape(wq, N)

    def tile(carry, k_ref, vt_ref, k0, bias):
        m_i, l_i, acc = carry
        s = _dot_nt(k_ref[0, 0, pl.ds(k0, tk), :], q4)
        if bias is not None:
            s = s + (bias if bias.ndim == 0 else jnp.concatenate([bias] * HPG, axis=1))
        m_n = jnp.maximum(m_i, jnp.max(s, axis=0, keepdims=True))
        alpha = jnp.exp2(m_i - m_n)
        p = jnp.exp2(s - m_n)
        l_n = alpha * l_i + jnp.sum(p, axis=0, keepdims=True)
        return m_n, l_n, alpha * acc + _dot(vt_ref[0, 0, :, pl.ds(k0, tk)], p.astype(BF16))

    def split(carry):
        o = carry[2] / carry[1]
        return [o[:, h * tq:(h + 1) * tq] for h in range(HPG)]

    def sel_tile(carries, j):
        k0 = pl.multiple_of(j * tk, tk)
        return tile(carries, ks_ref, vst_ref, k0, bias_scr[pl.ds(k0, tk), :])

    n_off = qi * (tq // tk)
    carries = lax.fori_loop(0, n_off // 2, lambda jj, c: sel_tile(sel_tile(c, 2 * jj), 2 * jj + 1), init)
    carries = lax.cond(n_off % 2 == 1, lambda c: sel_tile(c, n_off - 1), lambda c: c, carries)
    for e in range(tq // tk):
        k0 = pl.multiple_of(q0 + e * tk, tk)
        causal = jnp.where(rel - e * tk >= 0, 0.0, NEG)
        carries = tile(carries, ks_ref, vst_ref, k0, bias_scr[pl.ds(k0, tk), :] + causal)
    o_slc = split(carries)

    carries = init
    for e in range((WINDOW + tq) // tk):
        koff = e * tk - WINDOW
        lo, hi = -(tk - 1) - koff, (tq - 1) - koff
        if lo >= WINDOW or hi < 0:
            continue
        bias = None
        if lo < 0 or hi >= WINDOW:
            dist = rel - koff
            bias = jnp.where((dist >= 0) & (dist < WINDOW), 0.0, NEG)
        if koff < 0:
            bias = jnp.where(q0 + koff >= 0, 0.0 if bias is None else bias, NEG)
        k0 = pl.multiple_of(jnp.maximum(q0 + koff, 0), tk)
        carries = tile(carries, kw_ref, vwt_ref, k0, bias)
    o_win = split(carries)

    gate = gate_ref[0, 0]
    outs = [gate[3 * h:3 * h + 1, :] * o_cmp[h] + gate[3 * h + 1:3 * h + 2, :] * o_slc[h]
            + gate[3 * h + 2:3 * h + 3, :] * o_win[h] for h in range(HPG)]
    o_ref[0] = jnp.concatenate(outs, axis=0).T


def _nsa_attn_call(q, kc, vc, ks, vs_t, kw, vw_t, gates_t, ov_t, ex_t, tq, tk):
    B, H, T, N = q.shape
    G, HPG = NSA_KV_HEADS, HEADS_PER_GROUP
    assert WINDOW % tk == 0 and tq % tk == 0
    n_half = kc.shape[2]
    cmp_spec = pl.BlockSpec((1, 1, n_half, N), lambda b, g, t: (b, g, 0, 0))
    kv_spec = pl.BlockSpec((1, 1, T, N), lambda b, g, t: (b, g, 0, 0))
    kvt_spec = pl.BlockSpec((1, 1, N, T), lambda b, g, t: (b, g, 0, 0))
    return pl.pallas_call(
        functools.partial(_nsa_attn_kernel, tq=tq, tk=tk),
        out_shape=jax.ShapeDtypeStruct((B, T, D_NSA), F32),
        grid=(B, G, T // tq),
        in_specs=[
            pl.BlockSpec((1, HPG, tq, N), lambda b, g, t: (b, g, t, 0)),
            cmp_spec, cmp_spec, kv_spec, kvt_spec, kv_spec, kvt_spec,
            pl.BlockSpec((1, 1, GATE_PAD, tq), lambda b, g, t: (b, g, 0, t)),
            pl.BlockSpec(ov_t.shape, lambda b, g, t: (0, 0)),
            pl.BlockSpec(ex_t.shape, lambda b, g, t: (0, 0)),
        ],
        out_specs=pl.BlockSpec((1, tq, HPG * N), lambda b, g, t: (b, t, g)),
        scratch_shapes=[pltpu.VMEM((T, tq), F32)],
        compiler_params=_cparams(("parallel", "parallel", "arbitrary")),
        name="nsa_attn",
    )(q, kc, vc, ks, vs_t, kw, vw_t, gates_t, ov_t, ex_t)


def _outproj_kernel(orw_ref, ons_ref, x_ref, gt_ref, w_ref, g_ref, b_ref, o_ref, *, alpha):
    half = orw_ref.shape[-1]
    y = _dot(orw_ref[0].astype(BF16), w_ref[0, :half, :]) + _dot(ons_ref[0].astype(BF16), w_ref[0, half:, :])
    z = alpha * x_ref[0] + (1.0 + gt_ref[0]) * y
    o_ref[0] = _layer_norm_rows(z, g_ref[0], b_ref[0])


def _outproj_call(o_rw, o_ns, x, mod3, row0, w_bf16, ln_g, ln_b, layer, alpha, tm):
    B, T, D = x.shape
    half = o_rw.shape[-1]
    return pl.pallas_call(
        functools.partial(_outproj_kernel, alpha=alpha),
        out_shape=jax.ShapeDtypeStruct((B, T, D), F32),
        grid=(B, T // tm),
        in_specs=[
            pl.BlockSpec((1, tm, half), lambda b, m: (b, m, 0)),
            pl.BlockSpec((1, tm, half), lambda b, m: (b, m, 0)),
            pl.BlockSpec((1, tm, D), lambda b, m: (b, m, 0)),
            pl.BlockSpec((1, 1, D), lambda b, m: (row0 + 6 * b + 2, 0, 0)),
            _layer_spec(w_bf16, layer), _layer_spec(ln_g, layer), _layer_spec(ln_b, layer),
        ],
        out_specs=pl.BlockSpec((1, tm, D), lambda b, m: (b, m, 0)),
        compiler_params=_cparams(("parallel", "parallel")),
        name="outproj_ln",
    )(o_rw, o_ns, x, mod3, w_bf16, ln_g, ln_b)


def _mlp_kernel(x_ref, sc_ref, sh_ref, gt_ref, w1_ref, w2_ref, g_ref, b_ref, o_ref, h_scr, acc_scr, *, alpha):
    f = pl.program_id(2)

    @pl.when(f == 0)
    def _():
        h = x_ref[0] * (1.0 + sc_ref[0]) + sh_ref[0]
        h_scr[...] = h.astype(BF16)
        acc_scr[...] = jnp.zeros_like(acc_scr)

    a = jnp.maximum(_dot(h_scr[...], w1_ref[0]), 0.0)
    acc_scr[...] += _dot((a * a).astype(BF16), w2_ref[0])

    @pl.when(f == pl.num_programs(2) - 1)
    def _():
        z = alpha * x_ref[0] + (1.0 + gt_ref[0]) * acc_scr[...]
        o_ref[0] = _layer_norm_rows(z, g_ref[0], b_ref[0])


def _mlp_call(x, mod3, row0, w1_bf16, w2_bf16, ln_g, ln_b, layer, alpha, tm, tf):
    B, T, D = x.shape
    FF = w1_bf16.shape[2]
    modspec = lambda j: pl.BlockSpec((1, 1, D), lambda b, m, f: (row0 + 6 * b + j, 0, 0))
    return pl.pallas_call(
        functools.partial(_mlp_kernel, alpha=alpha),
        out_shape=jax.ShapeDtypeStruct((B, T, D), F32),
        grid=(B, T // tm, FF // tf),
        in_specs=[
            pl.BlockSpec((1, tm, D), lambda b, m, f: (b, m, 0)),
            modspec(4), modspec(3), modspec(5),
            pl.BlockSpec((1, D, tf), lambda b, m, f: (layer, 0, f)),
            pl.BlockSpec((1, tf, D), lambda b, m, f: (layer, f, 0)),
            _layer_spec(ln_g, layer), _layer_spec(ln_b, layer),
        ],
        out_specs=pl.BlockSpec((1, tm, D), lambda b, m, f: (b, m, 0)),
        scratch_shapes=[pltpu.VMEM((tm, D), BF16), pltpu.VMEM((tm, D), F32)],
        compiler_params=_cparams(("parallel", "parallel", "arbitrary")),
        name="mlp_ln",
    )(x, mod3, mod3, mod3, w1_bf16, w2_bf16, ln_g, ln_b)


def _pad_last(w, n):
    return jnp.pad(w, [(0, 0)] * (w.ndim - 1) + [(0, n - w.shape[-1])])


def _pad_rows(w, n):
    return jnp.pad(w, [(0, 0)] * (w.ndim - 2) + [(0, n - w.shape[-2]), (0, 0)])


def _split_w_in(w_in):
    c = np.cumsum([0, D_RWKV, D_RWKV, D_RWKV, DECAY_LORA, AAA_LORA, GATE_LORA]).tolist()
    rw = jnp.concatenate([w_in[..., c[0]:c[3]], _pad_last(w_in[..., c[3]:c[4]], LORA_PAD),
                          _pad_last(w_in[..., c[4]:c[5]], LORA_PAD), w_in[..., c[5]:c[6]]], axis=-1)
    ng = 3 * HEADS_PER_GROUP
    lead = w_in.shape[:-1]
    gates = _pad_last(w_in[..., c[6] + NS_GATE:].reshape(lead + (NSA_KV_HEADS, ng)), GATE_PAD)
    gates = _pad_last(gates.reshape(lead + (NSA_KV_HEADS * GATE_PAD,)), LANES)
    ns = jnp.concatenate([w_in[..., c[6]:c[6] + NS_GATE], gates], axis=-1)
    return rw.astype(BF16), ns.astype(BF16)


def _pad_mu(mu):
    c = np.cumsum([0, 3 * D_RWKV, DECAY_LORA, AAA_LORA, GATE_LORA]).tolist()
    parts = [mu[:, c[0]:c[1]], _pad_last(mu[:, c[1]:c[2]], LORA_PAD), _pad_last(mu[:, c[2]:c[3]], LORA_PAD),
             mu[:, c[3]:c[4]]]
    return jnp.concatenate(parts, axis=-1)[:, None, :]


def _rope_tables(T):
    half = HEAD_DIM // 2
    inv = ROPE_THETA ** (-jnp.arange(half, dtype=F32) / half)
    ang = jnp.arange(T, dtype=F32)[:, None] * inv[None]
    cos, sin = jnp.cos(ang), jnp.sin(ang)
    cos_t = jnp.tile(cos, (1, LANES // half))
    sin_t = jnp.tile(jnp.concatenate([-sin, sin], axis=1), (1, LANES // HEAD_DIM))
    return cos_t, sin_t


def _selection_constants(T):
    n_half = T // CMP_STRIDE
    n_cmp = (T - CMP_BLOCK) // CMP_STRIDE + 1
    n_sel = T // SEL_BLOCK
    pos = np.arange(n_cmp)[:, None] * CMP_STRIDE + np.arange(CMP_BLOCK)[None]
    ov = ((pos // SEL_BLOCK)[..., None] == np.arange(n_sel)).sum(1) / CMP_BLOCK
    ov_t = np.zeros((n_sel, n_half), np.float32)
    ov_t[:, :n_cmp] = ov.T
    ex_t = (np.arange(T)[:, None] // SEL_BLOCK == np.arange(n_sel)[None, :]).astype(np.float32)
    return jnp.asarray(ov_t), jnp.asarray(ex_t, BF16)


def kernel(x, c, w_ada, b_ada, w_in, rwkv_mu, rwkv_w0, rwkv_w2, rwkv_a0, rwkv_a2, rwkv_g2, rwkv_k_k, rwkv_k_a, rwkv_r_k, rwkv_lnx_g, rwkv_lnx_b, rwkv_v0, rwkv_v1, rwkv_v2, nsa_cmp_pos, nsa_cmp_w1, nsa_cmp_w2, w_out, ln1_g, ln1_b, mlp_w1, mlp_w2, ln2_g, ln2_b):
    B, T, D = x.shape
    L = w_ada.shape[0]
    alpha = (2 * L) ** 0.25

    c_pad = jnp.pad(c, ((0, -B % 8), (0, 0)))
    mod = _ada_call(c_pad, w_ada, b_ada)[:, :B]
    mod3 = mod.reshape(L * B * 6, 1, D)
    cos_t, sin_t = _rope_tables(T)
    ov_t, ex_t = _selection_constants(T)

    rows = lambda z: z.reshape(z.shape[0], 1, -1)
    w_rw, w_ns = _split_w_in(w_in)
    p = {
        "mu": _pad_mu(rwkv_mu), "w0": rows(rwkv_w0), "a0": rows(rwkv_a0),
        "w2": _pad_rows(rwkv_w2, LORA_PAD).astype(BF16), "a2": _pad_rows(rwkv_a2, LORA_PAD).astype(BF16),
        "g2": rwkv_g2.astype(BF16), "k_k": rows(rwkv_k_k), "k_a": rows(rwkv_k_a),
        "r_k": rows(rwkv_r_k), "lnx_g": rows(rwkv_lnx_g), "lnx_b": rows(rwkv_lnx_b),
        "v0": rows(rwkv_v0), "v1": _pad_last(rwkv_v1, LORA_PAD).astype(BF16),
        "v2": _pad_rows(rwkv_v2, LORA_PAD).astype(BF16),
    }
    half_block = CMP_STRIDE * HEAD_DIM
    cmp_pos = nsa_cmp_pos.reshape(2 * L, 2, half_block)
    cmp_w1 = nsa_cmp_w1.reshape((2 * L,) + nsa_cmp_w1.shape[2:])
    cmp_w2 = nsa_cmp_w2.reshape((2 * L,) + nsa_cmp_w2.shape[2:])
    w_out_b, w1_b, w2_b = w_out.astype(BF16), mlp_w1.astype(BF16), mlp_w2.astype(BF16)
    ln1 = (rows(ln1_g), rows(ln1_b))
    ln2 = (rows(ln2_g), rows(ln2_b))

    v_first = None
    for i in range(L):
        row0 = i * B * 6
        u_rw = _modmm_call(x, mod3, row0, w_rw, i, TILES["proj_m"], PROJ_TN, "inproj_rwkv")
        u_ns = _modmm_call(x, mod3, row0, w_ns, i, TILES["proj_m"], PROJ_TN, "inproj_nsa")

        r, ld, cs, k, v, kk, a, g = _rwkv_prep_call(u_rw, p, i, v_first, TILES["prep"])
        if i == 0:
            v_first = v
        o_rw = _rwkv_scan_call(r, ld, cs, k, v, kk, a, g, p, i, TILES["scan_rows"], TILES["scan_heads"])

        q, kc_in, vc_in, ks, vs_t, kw, vw_t, gates_t = _nsa_prep_call(u_ns, cos_t, sin_t, TILES["prep"])
        kc = _compress_call(kc_in, cmp_pos, cmp_w1, cmp_w2, 2 * i)
        vc = _compress_call(vc_in, cmp_pos, cmp_w1, cmp_w2, 2 * i + 1)
        o_ns = _nsa_attn_call(q, kc, vc, ks, vs_t, kw, vw_t, gates_t, ov_t, ex_t, TILES["attn_q"], TILES["attn_k"])

        x = _outproj_call(o_rw, o_ns, x, mod3, row0, w_out_b, *ln1, i, alpha, TILES["out_m"])
        x = _mlp_call(x, mod3, row0, w1_b, w2_b, *ln2, i, alpha, TILES["mlp_m"], TILES["mlp_f"])
    return x
```

```python
import functools

import numpy as np
import jax
import jax.numpy as jnp
from jax import lax
from jax.experimental import pallas as pl
from jax.experimental.pallas import tpu as pltpu

F32 = jnp.float32
BF16 = jnp.bfloat16
HI = lax.Precision.HIGHEST

HEAD_DIM = 64
RWKV_HEADS = 16
NSA_HEADS = 16
NSA_KV_HEADS = 4
HEADS_PER_GROUP = NSA_HEADS // NSA_KV_HEADS
D_RWKV = RWKV_HEADS * HEAD_DIM
D_NSA = NSA_HEADS * HEAD_DIM
NSA_KV = NSA_KV_HEADS * HEAD_DIM
DECAY_LORA = 96
AAA_LORA = 96
MV_LORA = 64
GATE_LORA = 256
GN_EPS = 64e-5
CMP_BLOCK = 32
CMP_STRIDE = 16
SEL_BLOCK = 64
N_SELECT = 8
WINDOW = 512
ROPE_THETA = 10000.0
NEG = -1e30
FORCE = 1e4
LN_EPS = 1e-5
LOG2_E = 1.4426950408889634

LANES = 128
LORA_PAD = LANES
RW_XW = 3 * D_RWKV
RW_XA = RW_XW + LORA_PAD
RW_XG = RW_XA + LORA_PAD
RW_COLS = RW_XG + GATE_LORA
NS_KC = D_NSA
NS_GATE = D_NSA + 6 * NSA_KV
NS_COLS = NS_GATE + LANES
GATE_PAD = 16
VT_ROWS = HEAD_DIM + 16
PROJ_TN = 896

CHUNK = 64
V7X_VMEM_BYTES = 64 * 1024 * 1024
VMEM_LIMIT = V7X_VMEM_BYTES - 8 * 1024 * 1024
TILES = dict(proj_m=1024, prep=256, scan_rows=128, scan_heads=8, attn_q=256, attn_k=256, out_m=256, mlp_m=512,
             mlp_f=1024)


def _layer_spec(arr, layer):
    tail = (0,) * (arr.ndim - 1)
    return pl.BlockSpec((1,) + arr.shape[1:], lambda *_: (layer,) + tail)


def _cparams(sem):
    return pltpu.CompilerParams(dimension_semantics=sem, vmem_limit_bytes=VMEM_LIMIT)


def _mm(fn, a, b, mode):
    if mode is None:
        return fn(a, b, None)
    if mode == "hi":
        return fn(a, b, HI)
    ah, bh = a.astype(BF16), b.astype(BF16)
    if mode == "bf":
        return fn(ah, bh, None)
    if mode == "r3":
        r1 = b - bh.astype(F32)
        bm = r1.astype(BF16)
        bl = (r1 - bm.astype(F32)).astype(BF16)
        return fn(ah, bh, None) + (fn(ah, bm, None) + fn(ah, bl, None))
    al = (a - ah.astype(F32)).astype(BF16)
    bl = (b - bh.astype(F32)).astype(BF16)
    return fn(ah, bh, None) + (fn(ah, bl, None) + fn(al, bh, None))


def _dot(a, b, mode=None):
    return _mm(lambda p, q, pr: jnp.dot(p, q, preferred_element_type=F32, precision=pr), a, b, mode)


def _dot_nt(a, b, mode=None):
    return _mm(lambda p, q, pr: lax.dot_general(p, q, (((1,), (1,)), ((), ())), preferred_element_type=F32,
                                               precision=pr), a, b, mode)


def _bmm(a, b, mode=None):
    return _mm(lambda p, q, pr: jnp.einsum("bij,bjk->bik", p, q, preferred_element_type=F32, precision=pr),
               a, b, mode)


def _bmm_nt(a, b, mode=None):
    return _mm(lambda p, q, pr: jnp.einsum("bik,bjk->bij", p, q, preferred_element_type=F32, precision=pr),
               a, b, mode)


def _bmm_tn(a, b, mode=None):
    return _mm(lambda p, q, pr: jnp.einsum("bci,bcj->bij", p, q, preferred_element_type=F32, precision=pr),
               a, b, mode)


def _layer_norm_rows(z, g, b):
    mu = jnp.mean(z, axis=-1, keepdims=True)
    zc = z - mu
    var = jnp.mean(zc * zc, axis=-1, keepdims=True)
    return zc * lax.rsqrt(var + LN_EPS) * g + b


def _ada_kernel(c_ref, w_ref, b_ref, o_ref):
    c = c_ref[...]
    cond = c * jax.nn.sigmoid(c)
    o_ref[0] = _dot(cond.astype(BF16), w_ref[0].astype(BF16)) + b_ref[0]


def _ada_call(c_pad, w_ada, b_ada):
    L, D, N = w_ada.shape
    tn = 1024
    return pl.pallas_call(
        _ada_kernel,
        out_shape=jax.ShapeDtypeStruct((L, c_pad.shape[0], N), F32),
        grid=(L, N // tn),
        in_specs=[
            pl.BlockSpec(c_pad.shape, lambda l, n: (0, 0)),
            pl.BlockSpec((1, D, tn), lambda l, n: (l, 0, n)),
            pl.BlockSpec((1, 1, tn), lambda l, n: (l, 0, n)),
        ],
        out_specs=pl.BlockSpec((1, c_pad.shape[0], tn), lambda l, n: (l, 0, n)),
        compiler_params=_cparams(("parallel", "parallel")),
        name="adaln_mod",
    )(c_pad, w_ada, b_ada.reshape(L, 1, N))


def _modmm_kernel(x_ref, sc_ref, sh_ref, w_ref, o1_ref, o2_ref, h_scr, *, n1):
    n = pl.program_id(2)

    @pl.when(n == 0)
    def _():
        h = x_ref[0] * (1.0 + sc_ref[0]) + sh_ref[0]
        h_scr[...] = h.astype(BF16)

    y = _dot(h_scr[...], w_ref[0])

    @pl.when(n < n1)
    def _():
        o1_ref[0] = y

    @pl.when(n >= n1)
    def _():
        o2_ref[0] = y


def _modmm_call(x, mod3, row0, w_bf16, layer, n_first, tm, tn):
    B, T, D = x.shape
    N = w_bf16.shape[2]
    n1 = n_first // tn
    return pl.pallas_call(
        functools.partial(_modmm_kernel, n1=n1),
        out_shape=[jax.ShapeDtypeStruct((B, T, n_first), F32), jax.ShapeDtypeStruct((B, T, N - n_first), F32)],
        grid=(B, T // tm, N // tn),
        in_specs=[
            pl.BlockSpec((1, tm, D), lambda b, m, n: (b, m, 0)),
            pl.BlockSpec((1, 1, D), lambda b, m, n: (row0 + 6 * b + 1, 0, 0)),
            pl.BlockSpec((1, 1, D), lambda b, m, n: (row0 + 6 * b, 0, 0)),
            pl.BlockSpec((1, D, tn), lambda b, m, n: (layer, 0, n)),
        ],
        out_specs=[pl.BlockSpec((1, tm, tn), lambda b, m, n: (b, m, jnp.minimum(n, n1 - 1))),
                   pl.BlockSpec((1, tm, tn), lambda b, m, n: (b, m, jnp.maximum(n - n1, 0)))],
        scratch_shapes=[pltpu.VMEM((tm, D), BF16)],
        compiler_params=_cparams(("parallel", "parallel", "arbitrary")),
        name="inproj",
    )(x, mod3, mod3, w_bf16)


def _rwkv_prep_kernel(*refs, tt, first_layer):
    if first_layer:
        (u_ref, up_ref, mu_ref, w0_ref, w2_ref, a0_ref, a2_ref, g2_ref, kk_ref, ka_ref,
         r_o, ld_o, cs_o, k_o, v_o, kk_o, a_o, g_o, sh_scr) = refs
    else:
        (u_ref, up_ref, mu_ref, w0_ref, w2_ref, a0_ref, a2_ref, g2_ref, kk_ref, ka_ref,
         vf_ref, v0_ref, v1_ref, v2_ref,
         r_o, ld_o, cs_o, k_o, v_o, kk_o, a_o, g_o, sh_scr) = refs
    ti = pl.program_id(1)
    u = u_ref[0]
    prev = jnp.where(ti > 0, up_ref[0], 0.0)
    sh_scr[0:8, :] = prev
    sh_scr[8:8 + tt, :] = u
    us = sh_scr[7:7 + tt, :]
    x = u + (us - u) * mu_ref[0]
    r = x[:, 0:D_RWKV]
    k = x[:, D_RWKV:2 * D_RWKV]
    v = x[:, 2 * D_RWKV:3 * D_RWKV]
    xw = x[:, RW_XW:RW_XA]
    xa = x[:, RW_XA:RW_XG]
    xg = x[:, RW_XG:RW_COLS]
    w = -jax.nn.softplus(-(w0_ref[0] + _dot(jnp.tanh(xw).astype(BF16), w2_ref[0]))) - 0.5
    ld = -jnp.exp(w)
    ld_o[0] = ld
    row = lax.broadcasted_iota(jnp.int32, (tt, tt), 0)
    col = lax.broadcasted_iota(jnp.int32, (tt, tt), 1)
    chunk_ltri = jnp.where((row // CHUNK == col // CHUNK) & (col <= row), 1.0, 0.0)
    cs_o[0] = _dot(chunk_ltri, ld, "r3")
    a = jax.nn.sigmoid(a0_ref[0] + _dot(xa.astype(BF16), a2_ref[0]))
    g_o[0] = _dot(jax.nn.sigmoid(xg).astype(BF16), g2_ref[0])
    if not first_layer:
        lo = _dot(v.astype(BF16), v1_ref[0])
        gate = jax.nn.sigmoid(v0_ref[0] + _dot(lo.astype(BF16), v2_ref[0]))
        v = v + (vf_ref[0] - v) * gate
    r_o[0] = r
    v_o[0] = v
    a_o[0] = a
    kk_o[0] = k * kk_ref[0]
    k_o[0] = k * (1.0 + (a - 1.0) * ka_ref[0])


def _rwkv_prep_call(u_rw, p, layer, v_first, tt):
    B, T, _ = u_rw.shape
    first_layer = v_first is None
    tile = pl.BlockSpec((1, tt, D_RWKV), lambda b, t: (b, t, 0))
    names = ["mu", "w0", "w2", "a0", "a2", "g2", "k_k", "k_a"]
    in_specs = [
        pl.BlockSpec((1, tt, RW_COLS), lambda b, t: (b, t, 0)),
        pl.BlockSpec((1, 8, RW_COLS), lambda b, t: (b, jnp.maximum(t * (tt // 8) - 1, 0), 0)),
    ] + [_layer_spec(p[nm], layer) for nm in names]
    args = [u_rw, u_rw] + [p[nm] for nm in names]
    if not first_layer:
        in_specs += [tile] + [_layer_spec(p[nm], layer - 1) for nm in ("v0", "v1", "v2")]
        args += [v_first, p["v0"], p["v1"], p["v2"]]
    out = jax.ShapeDtypeStruct((B, T, D_RWKV), F32)
    return pl.pallas_call(
        functools.partial(_rwkv_prep_kernel, tt=tt, first_layer=first_layer),
        out_shape=[out] * 8,
        grid=(B, T // tt),
        in_specs=in_specs,
        out_specs=[tile] * 8,
        scratch_shapes=[pltpu.VMEM((tt + 8, RW_COLS), F32)],
        compiler_params=_cparams(("parallel", "parallel")),
        name="rwkv_prep",
    )(*args)


def _rwkv_scan_kernel(r_ref, ld_ref, cs_ref, k_ref, v_ref, kk_ref, a_ref, g_ref, rk_ref, lg_ref, lb_ref,
                      o_ref, s_scr, wr_scr, o0_scr, pm_scr, qm_scr, bn_scr, g_scr, *, ts, nt, hpb):
    C = CHUNK
    nc = ts // C
    nb = hpb * nc
    N = HEAD_DIM
    n = pl.program_id(0)

    @pl.when(n == 0)
    def _():
        for scr in (s_scr, wr_scr, o0_scr, pm_scr, qm_scr, bn_scr, g_scr):
            scr[...] = jnp.zeros_like(scr)

    first = lax.rem(jnp.maximum(n - 1, 0), nt) == 0
    outs = []
    for hh in range(hpb):
        S = jnp.where(first, 0.0, s_scr[hh])
        for c in range(nc):
            i = hh * nc + c
            Sb = S.astype(BF16)
            outs.append(_dot_nt(wr_scr[i], Sb) + o0_scr[i])
            S = _dot(Sb, pm_scr[i]) + qm_scr[i]
        s_scr[hh] = S
    o = jnp.stack(outs, axis=0)

    def split_row(ref):
        x = ref[0]
        return jnp.concatenate([jnp.broadcast_to(x[None, :, h * N:(h + 1) * N], (nc, 1, N)) for h in range(hpb)],
                               axis=0)

    mu = jnp.mean(o, axis=-1, keepdims=True)
    oc = o - mu
    var = jnp.mean(oc * oc, axis=-1, keepdims=True)
    on = oc * lax.rsqrt(var + GN_EPS) * split_row(lg_ref) + split_row(lb_ref)
    res = (on + bn_scr[...]) * g_scr[...]
    o_ref[0] = jnp.concatenate([res[h * nc:(h + 1) * nc].reshape(ts, N) for h in range(hpb)], axis=-1)

    def split(ref):
        x = ref[0].reshape(nc, C, hpb * N)
        return jnp.concatenate([x[:, :, h * N:(h + 1) * N] for h in range(hpb)], axis=0)

    r, ld, cs, k, v, kkr, a, g = (split(z) for z in (r_ref, ld_ref, cs_ref, k_ref, v_ref, kk_ref, a_ref, g_ref))
    kkn = kkr / jnp.maximum(jnp.sqrt(jnp.sum(kkr * kkr, axis=-1, keepdims=True)), 1e-12)
    row = lax.broadcasted_iota(jnp.int32, (C, C), 0)
    col = lax.broadcasted_iota(jnp.int32, (C, C), 1)
    incl = col <= row
    strict = col < row
    eye = col == row
    cs_last = cs[:, C - 1:C, :]
    e_in = jnp.exp(cs)
    e_ex = jnp.exp(cs - ld)
    e_neg = jnp.exp(-cs)
    e_hat = jnp.exp(cs_last - cs)
    at = -kkn * e_ex
    b = kkn * a
    bt = b * e_neg
    kt = k * e_neg
    rt = r * e_in
    bh = (b * e_hat).astype(BF16)
    kh = (k * e_hat).astype(BF16)
    vb = v.astype(BF16)
    A = _bmm_nt(jnp.concatenate([at, rt], axis=1).astype(BF16), jnp.concatenate([bt, kt], axis=1).astype(BF16))
    a_ab = jnp.where(strict, A[:, :C, :C], 0.0)
    a_ak = jnp.where(strict, A[:, :C, C:], 0.0).astype(BF16)
    a_rb = jnp.where(incl, A[:, C:, :C], 0.0).astype(BF16)
    a_rk = jnp.where(incl, A[:, C:, C:], 0.0).astype(BF16)
    npow = a_ab.astype(BF16)
    tinv = jnp.where(eye, 1.0, 0.0).astype(F32) + a_ab
    p2 = 2
    while p2 < C:
        npow_f = _bmm(npow, npow)
        npow = npow_f.astype(BF16)
        tinv = tinv + _bmm(npow, tinv.astype(BF16))
        p2 *= 2
    akv = _bmm(a_ak, vb)
    x = _bmm(tinv.astype(BF16), jnp.concatenate([at, akv], axis=-1).astype(BF16))
    xb = x.astype(BF16)
    y = _bmm(a_rb, xb)
    pq = _bmm_tn(xb, bh)
    wr_scr[...] = (rt + y[:, :, :N]).astype(BF16)
    o0_scr[...] = y[:, :, N:] + _bmm(a_rk, vb)
    pm_scr[...] = (pq[:, :N, :] + jnp.where(eye, jnp.exp(cs_last), 0.0)).astype(BF16)
    qm_scr[...] = pq[:, N:, :] + _bmm_tn(vb, kh)
    bn_scr[...] = jnp.sum(r * k * split_row(rk_ref), axis=-1, keepdims=True) * v
    g_scr[...] = g


def _rwkv_scan_call(r, ld, cs, k, v, kk, a, g, p, layer, ts, hpb):
    B, T, _ = r.shape
    nt = T // ts
    hg = RWKV_HEADS // hpb
    nblk = B * hg * nt
    nb = hpb * (ts // CHUNK)
    w = hpb * HEAD_DIM

    def blk(n):
        return n // (hg * nt), lax.rem(n, nt), lax.rem(n // nt, hg)

    cur = lambda n: blk(jnp.minimum(n, nblk - 1))
    prev = lambda n: blk(jnp.maximum(n - 1, 0))
    tile = pl.BlockSpec((1, ts, w), cur)
    rowp = lambda f: pl.BlockSpec((1, 1, w), lambda n: (layer, 0, f(n)[2]))
    sq = lambda dt: pltpu.VMEM((nb, HEAD_DIM, HEAD_DIM), dt)
    return pl.pallas_call(
        functools.partial(_rwkv_scan_kernel, ts=ts, nt=nt, hpb=hpb),
        out_shape=jax.ShapeDtypeStruct((B, T, D_RWKV), F32),
        grid=(nblk + 1,),
        in_specs=[tile] * 8 + [rowp(cur), rowp(prev), rowp(prev)],
        out_specs=pl.BlockSpec((1, ts, w), prev),
        scratch_shapes=[pltpu.VMEM((hpb, HEAD_DIM, HEAD_DIM), F32), pltpu.VMEM((nb, CHUNK, HEAD_DIM), BF16),
                        pltpu.VMEM((nb, CHUNK, HEAD_DIM), F32), sq(BF16), sq(F32),
                        pltpu.VMEM((nb, CHUNK, HEAD_DIM), F32), pltpu.VMEM((nb, CHUNK, HEAD_DIM), F32)],
        compiler_params=_cparams(("arbitrary",)),
        name="rwkv_scan",
    )(r, ld, cs, k, v, kk, a, g, p["r_k"], p["lnx_g"], p["lnx_b"])


def _nsa_prep_kernel(u_ref, cos_ref, sin_ref, q_o, kc_o, vc_o, ks_o, vs_o, kw_o, vw_o, gate_o):
    cos = cos_ref[...]
    sin = sin_ref[...]
    lane = lax.broadcasted_iota(jnp.int32, cos.shape, 1)
    first_half = (lane % HEAD_DIM) < (HEAD_DIM // 2)

    def rope(x):
        other = jnp.where(first_half, pltpu.roll(x, LANES - HEAD_DIM // 2, 1), pltpu.roll(x, HEAD_DIM // 2, 1))
        return x * cos + other * sin

    def put(out_ref, col0, nheads, roped, scale=None):
        for j in range(nheads // 2):
            x = u_ref[0, :, col0 + j * LANES:col0 + (j + 1) * LANES]
            if roped:
                x = rope(x)
            if scale is not None:
                x = x * scale
            out_ref[0, 2 * j] = x[:, :HEAD_DIM].astype(out_ref.dtype)
            out_ref[0, 2 * j + 1] = x[:, HEAD_DIM:].astype(out_ref.dtype)

    low = lane < HEAD_DIM

    def put_wide(out_ref, col0, nheads, scale, tail):
        for j in range(nheads // 2):
            x = rope(u_ref[0, :, col0 + j * LANES:col0 + (j + 1) * LANES])
            if scale is not None:
                x = x * scale
            out_ref[0, 2 * j] = jnp.where(low, x, tail).astype(out_ref.dtype)
            out_ref[0, 2 * j + 1] = jnp.where(low, pltpu.roll(x, HEAD_DIM, 1), tail).astype(out_ref.dtype)

    def put_t(out_ref, col0, nheads):
        extra = out_ref.shape[2] - HEAD_DIM
        ones_row = jnp.where(lax.broadcasted_iota(jnp.int32, (extra, cos.shape[0]), 0) == 0, 1.0, 0.0)
        for j in range(nheads // 2):
            xt = u_ref[0, :, col0 + j * LANES:col0 + (j + 1) * LANES].T
            for i, part in enumerate((xt[:HEAD_DIM, :], xt[HEAD_DIM:, :])):
                out_ref[0, 2 * j + i, :HEAD_DIM, :] = part.astype(out_ref.dtype)
                out_ref[0, 2 * j + i, HEAD_DIM:, :] = ones_row.astype(out_ref.dtype)

    pos = pl.program_id(1) * cos.shape[0] + lax.broadcasted_iota(jnp.int32, cos.shape, 0)
    blk_onehot = jnp.where(lane - HEAD_DIM == pos // SEL_BLOCK, 1.0, 0.0)
    put_wide(q_o, 0, NSA_HEADS, HEAD_DIM ** -0.5 * LOG2_E, 0.0)
    put(kc_o, NS_KC, NSA_KV_HEADS, True)
    put(vc_o, NS_KC + NSA_KV, NSA_KV_HEADS, False)
    put_wide(ks_o, NS_KC + 2 * NSA_KV, NSA_KV_HEADS, None, blk_onehot)
    put_t(vs_o, NS_KC + 3 * NSA_KV, NSA_KV_HEADS)
    put_wide(kw_o, NS_KC + 4 * NSA_KV, NSA_KV_HEADS, None, 0.0)
    put_t(vw_o, NS_KC + 5 * NSA_KV, NSA_KV_HEADS)
    gates_t = jax.nn.sigmoid(u_ref[0, :, NS_GATE:NS_GATE + LANES]).T
    for gi in range(NSA_KV_HEADS):
        gate_o[0, gi] = gates_t[gi * GATE_PAD:(gi + 1) * GATE_PAD, :]


def _nsa_prep_call(u_ns, cos_t, sin_t, tt):
    B, T, _ = u_ns.shape
    G, H, N = NSA_KV_HEADS, NSA_HEADS, HEAD_DIM
    assert T // SEL_BLOCK <= LANES - N
    kv = jax.ShapeDtypeStruct((B, G, T, N), F32)
    kv_wide = jax.ShapeDtypeStruct((B, G, T, LANES), BF16)
    kv_t = jax.ShapeDtypeStruct((B, G, VT_ROWS, T), BF16)
    kv_spec = pl.BlockSpec((1, G, tt, N), lambda b, t: (b, 0, t, 0))
    kvw_spec = pl.BlockSpec((1, G, tt, LANES), lambda b, t: (b, 0, t, 0))
    kvt_spec = pl.BlockSpec((1, G, VT_ROWS, tt), lambda b, t: (b, 0, 0, t))
    return pl.pallas_call(
        _nsa_prep_kernel,
        out_shape=[jax.ShapeDtypeStruct((B, H, T, LANES), BF16), kv, kv, kv_wide, kv_t,
                   kv_wide, kv_t, jax.ShapeDtypeStruct((B, G, GATE_PAD, T), F32)],
        grid=(B, T // tt),
        in_specs=[
            pl.BlockSpec((1, tt, NS_COLS), lambda b, t: (b, t, 0)),
            pl.BlockSpec((tt, LANES), lambda b, t: (t, 0)),
            pl.BlockSpec((tt, LANES), lambda b, t: (t, 0)),
        ],
        out_specs=[pl.BlockSpec((1, H, tt, LANES), lambda b, t: (b, 0, t, 0)), kv_spec, kv_spec, kvw_spec, kvt_spec,
                   kvw_spec, kvt_spec, pl.BlockSpec((1, G, GATE_PAD, tt), lambda b, t: (b, 0, 0, t))],
        compiler_params=_cparams(("parallel", "parallel")),
        name="nsa_prep",
    )(u_ns, cos_t, sin_t)


def _compress_kernel(x_ref, pos_ref, w1_ref, w2_ref, o_ref):
    x = x_ref[0, 0]
    half = x.shape[1]
    y_top = _dot(x + pos_ref[0, 0:1, :], w1_ref[0, :half, :], "hi")
    y_bot = _dot(x + pos_ref[0, 1:2, :], w1_ref[0, half:, :], "hi")
    pre = y_top + pltpu.roll(y_bot, x.shape[0] - 1, 0)
    o_ref[0, 0] = _dot(jax.nn.gelu(pre), w2_ref[0], "hi")


def _compress_call(kv, pos, w1, w2, which):
    B, G, T, N = kv.shape
    n_half = T // CMP_STRIDE
    width = CMP_STRIDE * N
    hidden = w1.shape[-1]
    x = kv.reshape(B, G, n_half, width)
    return pl.pallas_call(
        _compress_kernel,
        out_shape=jax.ShapeDtypeStruct((B, G, n_half, w2.shape[-1]), F32),
        grid=(B, G),
        in_specs=[
            pl.BlockSpec((1, 1, n_half, width), lambda b, g: (b, g, 0, 0)),
            pl.BlockSpec((1, 2, width), lambda b, g: (which, 0, 0)),
            pl.BlockSpec((1, 2 * width, hidden), lambda b, g: (which, 0, 0)),
            pl.BlockSpec((1, hidden, w2.shape[-1]), lambda b, g: (which, 0, 0)),
        ],
        out_specs=pl.BlockSpec((1, 1, n_half, w2.shape[-1]), lambda b, g: (b, g, 0, 0)),
        compiler_params=_cparams(("parallel", "parallel")),
        name="nsa_compress",
    )(x, pos, w1, w2)


def _nsa_attn_kernel(q_ref, kc_ref, vc_ref, ks_ref, vst_ref, kw_ref, vwt_ref, gate_ref, ov_ref, put_ref,
                     o_ref, s0_scr, s1_scr, *, tq, tk):
    HPG, N = HEADS_PER_GROUP, HEAD_DIM
    qi = pl.program_id(2)
    q0 = qi * tq
    k_len = ks_ref.shape[2]

    wq = HPG * tq
    q_win = q_ref[0].reshape(wq, LANES)
    rel = lax.broadcasted_iota(jnp.int32, (tk, tq), 1) - lax.broadcasted_iota(jnp.int32, (tk, tq), 0)
    init = (jnp.full((1, wq), NEG, F32), jnp.zeros((VT_ROWS, wq), F32))
    n_tiles = k_len // tk

    def scores(q_all, k_ref, k0, dst):
        dst[...] = _dot_nt(k_ref[0, 0, pl.ds(k0, tk), :], q_all)

    def update(carry, src, vt_ref, k0, bias):
        m_i, acc = carry
        s = src[...]
        if bias is not None:
            s = s + (bias if bias.ndim == 0 else jnp.concatenate([bias] * HPG, axis=1))
        m_n = jnp.maximum(m_i, jnp.max(s, axis=0, keepdims=True))
        p = jnp.exp2(s - m_n)
        return m_n, jnp.exp2(m_i - m_n) * acc + _dot(vt_ref[0, 0, :, pl.ds(k0, tk)], p.astype(BF16))

    def split(o):
        return [o[:, h * tq:(h + 1) * tq] for h in range(HPG)]

    def finish(carry):
        return split(carry[1][:N] / carry[1][N:N + 1])

    def window_tile(e):
        koff = e * tk - WINDOW
        lo, hi = -(tk - 1) - koff, (tq - 1) - koff
        bias = None
        if lo < 0 or hi >= WINDOW:
            dist = rel - koff
            bias = jnp.where((dist >= 0) & (dist < WINDOW), 0.0, NEG)
        if koff < 0:
            bias = jnp.where(q0 + koff >= 0, 0.0 if bias is None else bias, NEG)
        return pl.multiple_of(jnp.maximum(q0 + koff, 0), tk), bias

    win = [window_tile(e) for e in range((WINDOW + tq) // tk)]
    bufs = (s0_scr, s1_scr)
    scores(q_win, kw_ref, win[0][0], bufs[0])
    carry = init
    for e, (k0, bias) in enumerate(win):
        if e + 1 < len(win):
            scores(q_win, kw_ref, win[e + 1][0], bufs[(e + 1) % 2])
        carry = update(carry, bufs[e % 2], vwt_ref, k0, bias)
    o_win = finish(carry)

    n_cmp = kc_ref.shape[2]
    c_c = lax.broadcasted_iota(jnp.int32, (n_cmp, tq), 0)
    t_c = q0 + lax.broadcasted_iota(jnp.int32, (n_cmp, tq), 1)
    cmask = jnp.concatenate([c_c * CMP_STRIDE + (CMP_BLOCK - 1) <= t_c] * HPG, axis=1)
    s = jnp.where(cmask, _dot_nt(kc_ref[0, 0].astype(BF16), q_win), NEG)
    m = jnp.max(s, axis=0, keepdims=True)
    e = jnp.where(cmask, jnp.exp2(s - m), 0.0)
    l = jnp.sum(e, axis=0, keepdims=True)
    p = e / jnp.maximum(l, 1e-30)
    o_cmp = split(_dot(vc_ref[0, 0].T[:N].astype(BF16), p.astype(BF16)))
    psum = p[:, :tq]
    for h in range(1, HPG):
        psum = psum + p[:, h * tq:(h + 1) * tq]

    n_sel = ov_ref.shape[0]
    imp = _dot(ov_ref[...], psum, "r3")
    blk = lax.broadcasted_iota(jnp.int32, (n_sel, tq), 0)
    t_s = q0 + lax.broadcasted_iota(jnp.int32, (n_sel, tq), 1)
    cur = t_s // SEL_BLOCK
    forced = (blk == 0) | (blk == cur) | (blk == cur - 1)
    val = jnp.where(forced, FORCE, jnp.where(blk * SEL_BLOCK <= t_s, imp, -1.0))
    rank = jnp.zeros((n_sel, tq), F32)
    for i in range(n_sel):
        vi = val[i:i + 1, :]
        ahead = (vi > val) | ((vi == val) & (blk > i))
        rank = rank + jnp.where(ahead, 1.0, 0.0)
    unsel_t = jnp.where(rank < float(min(N_SELECT, n_sel)), 0.0, NEG)
    q_bias = lax.dot_general(unsel_t.astype(BF16), put_ref[...], (((0,), (0,)), ((), ())),
                             preferred_element_type=F32)

    q_sel = jnp.concatenate([(q_ref[0, h].astype(F32) + q_bias).astype(BF16) for h in range(HPG)], axis=0)

    def key0(j):
        return pl.multiple_of(j * tk, tk)

    scores(q_sel, ks_ref, key0(0), s0_scr)

    def pair(jj, carry):
        scores(q_sel, ks_ref, key0(2 * jj + 1), s1_scr)
        carry = update(carry, s0_scr, vst_ref, key0(2 * jj), None)
        scores(q_sel, ks_ref, key0(2 * jj + 2), s0_scr)
        return update(carry, s1_scr, vst_ref, key0(2 * jj + 1), None)

    carry = lax.fori_loop(0, qi // 2, pair, init)
    t1 = 2 * (qi // 2)
    t2 = jnp.minimum(t1 + 1, n_tiles - 1)

    def causal(t):
        return jnp.where(rel >= (t - qi) * tk, 0.0, NEG)

    scores(q_sel, ks_ref, key0(t2), s1_scr)
    carry = update(carry, s0_scr, vst_ref, key0(t1), causal(t1))
    o_slc = finish(update(carry, s1_scr, vst_ref, key0(t2), causal(t1 + 1)))

    gate = gate_ref[0, 0]
    outs = [gate[3 * h:3 * h + 1, :] * o_cmp[h] + gate[3 * h + 1:3 * h + 2, :] * o_slc[h]
            + gate[3 * h + 2:3 * h + 3, :] * o_win[h] for h in range(HPG)]
    o_ref[0] = jnp.concatenate(outs, axis=0).T


def _nsa_attn_call(q, kc, vc, ks, vs_t, kw, vw_t, gates_t, ov_t, put, tq, tk):
    B, H, T, _ = q.shape
    N = HEAD_DIM
    G, HPG = NSA_KV_HEADS, HEADS_PER_GROUP
    assert WINDOW % tk == 0 and tq == tk
    n_half = kc.shape[2]
    cmp_spec = pl.BlockSpec((1, 1, n_half, LANES), lambda b, g, t: (b, g, 0, 0))
    kv_spec = pl.BlockSpec((1, 1, T, LANES), lambda b, g, t: (b, g, 0, 0))
    kvt_spec = pl.BlockSpec((1, 1, VT_ROWS, T), lambda b, g, t: (b, g, 0, 0))
    return pl.pallas_call(
        functools.partial(_nsa_attn_kernel, tq=tq, tk=tk),
        out_shape=jax.ShapeDtypeStruct((B, T, D_NSA), F32),
        grid=(B, G, T // tq),
        in_specs=[
            pl.BlockSpec((1, HPG, tq, LANES), lambda b, g, t: (b, g, t, 0)),
            cmp_spec, cmp_spec, kv_spec, kvt_spec, kv_spec, kvt_spec,
            pl.BlockSpec((1, 1, GATE_PAD, tq), lambda b, g, t: (b, g, 0, t)),
            pl.BlockSpec(ov_t.shape, lambda b, g, t: (0, 0)),
            pl.BlockSpec(put.shape, lambda b, g, t: (0, 0)),
        ],
        out_specs=pl.BlockSpec((1, tq, HPG * N), lambda b, g, t: (b, t, g)),
        scratch_shapes=[pltpu.VMEM((tk, HPG * tq), F32), pltpu.VMEM((tk, HPG * tq), F32)],
        compiler_params=_cparams(("parallel", "parallel", "arbitrary")),
        name="nsa_attn",
    )(q, kc, vc, ks, vs_t, kw, vw_t, gates_t, ov_t, put)


def _outproj_kernel(orw_ref, ons_ref, x_ref, gt_ref, w_ref, g_ref, b_ref, o_ref, *, alpha):
    half = orw_ref.shape[-1]
    y = _dot(orw_ref[0].astype(BF16), w_ref[0, :half, :]) + _dot(ons_ref[0].astype(BF16), w_ref[0, half:, :])
    z = alpha * x_ref[0] + (1.0 + gt_ref[0]) * y
    o_ref[0] = _layer_norm_rows(z, g_ref[0], b_ref[0])


def _outproj_call(o_rw, o_ns, x, mod3, row0, w_bf16, ln_g, ln_b, layer, alpha, tm):
    B, T, D = x.shape
    half = o_rw.shape[-1]
    return pl.pallas_call(
        functools.partial(_outproj_kernel, alpha=alpha),
        out_shape=jax.ShapeDtypeStruct((B, T, D), F32),
        grid=(B, T // tm),
        in_specs=[
            pl.BlockSpec((1, tm, half), lambda b, m: (b, m, 0)),
            pl.BlockSpec((1, tm, half), lambda b, m: (b, m, 0)),
            pl.BlockSpec((1, tm, D), lambda b, m: (b, m, 0)),
            pl.BlockSpec((1, 1, D), lambda b, m: (row0 + 6 * b + 2, 0, 0)),
            _layer_spec(w_bf16, layer), _layer_spec(ln_g, layer), _layer_spec(ln_b, layer),
        ],
        out_specs=pl.BlockSpec((1, tm, D), lambda b, m: (b, m, 0)),
        compiler_params=_cparams(("parallel", "parallel")),
        name="outproj_ln",
    )(o_rw, o_ns, x, mod3, w_bf16, ln_g, ln_b)


def _mlp_kernel(x_ref, sc_ref, sh_ref, gt_ref, w1_ref, w2_ref, g_ref, b_ref, o_ref, h_scr, acc_scr, *, alpha):
    f = pl.program_id(2)

    @pl.when(f == 0)
    def _():
        h = x_ref[0] * (1.0 + sc_ref[0]) + sh_ref[0]
        h_scr[...] = h.astype(BF16)
        acc_scr[...] = jnp.zeros_like(acc_scr)

    a = jnp.maximum(_dot(h_scr[...], w1_ref[0]), 0.0)
    acc_scr[...] += _dot((a * a).astype(BF16), w2_ref[0])

    @pl.when(f == pl.num_programs(2) - 1)
    def _():
        z = alpha * x_ref[0] + (1.0 + gt_ref[0]) * acc_scr[...]
        o_ref[0] = _layer_norm_rows(z, g_ref[0], b_ref[0])


def _mlp_call(x, mod3, row0, w1_bf16, w2_bf16, ln_g, ln_b, layer, alpha, tm, tf):
    B, T, D = x.shape
    FF = w1_bf16.shape[2]
    modspec = lambda j: pl.BlockSpec((1, 1, D), lambda b, m, f: (row0 + 6 * b + j, 0, 0))
    return pl.pallas_call(
        functools.partial(_mlp_kernel, alpha=alpha),
        out_shape=jax.ShapeDtypeStruct((B, T, D), F32),
        grid=(B, T // tm, FF // tf),
        in_specs=[
            pl.BlockSpec((1, tm, D), lambda b, m, f: (b, m, 0)),
            modspec(4), modspec(3), modspec(5),
            pl.BlockSpec((1, D, tf), lambda b, m, f: (layer, 0, f)),
            pl.BlockSpec((1, tf, D), lambda b, m, f: (layer, f, 0)),
            _layer_spec(ln_g, layer), _layer_spec(ln_b, layer),
        ],
        out_specs=pl.BlockSpec((1, tm, D), lambda b, m, f: (b, m, 0)),
        scratch_shapes=[pltpu.VMEM((tm, D), BF16), pltpu.VMEM((tm, D), F32)],
        compiler_params=_cparams(("parallel", "parallel", "arbitrary")),
        name="mlp_ln",
    )(x, mod3, mod3, mod3, w1_bf16, w2_bf16, ln_g, ln_b)


def _pad_last(w, n):
    return jnp.pad(w, [(0, 0)] * (w.ndim - 1) + [(0, n - w.shape[-1])])


def _pad_rows(w, n):
    return jnp.pad(w, [(0, 0)] * (w.ndim - 2) + [(0, n - w.shape[-2]), (0, 0)])


def _split_w_in(w_in):
    c = np.cumsum([0, D_RWKV, D_RWKV, D_RWKV, DECAY_LORA, AAA_LORA, GATE_LORA]).tolist()
    rw = jnp.concatenate([w_in[..., c[0]:c[3]], _pad_last(w_in[..., c[3]:c[4]], LORA_PAD),
                          _pad_last(w_in[..., c[4]:c[5]], LORA_PAD), w_in[..., c[5]:c[6]]], axis=-1)
    ng = 3 * HEADS_PER_GROUP
    lead = w_in.shape[:-1]
    gates = _pad_last(w_in[..., c[6] + NS_GATE:].reshape(lead + (NSA_KV_HEADS, ng)), GATE_PAD)
    gates = _pad_last(gates.reshape(lead + (NSA_KV_HEADS * GATE_PAD,)), LANES)
    ns = jnp.concatenate([w_in[..., c[6]:c[6] + NS_GATE], gates], axis=-1)
    return jnp.concatenate([rw, ns], axis=-1).astype(BF16)


def _pad_mu(mu):
    c = np.cumsum([0, 3 * D_RWKV, DECAY_LORA, AAA_LORA, GATE_LORA]).tolist()
    parts = [mu[:, c[0]:c[1]], _pad_last(mu[:, c[1]:c[2]], LORA_PAD), _pad_last(mu[:, c[2]:c[3]], LORA_PAD),
             mu[:, c[3]:c[4]]]
    return jnp.concatenate(parts, axis=-1)[:, None, :]


def _rope_tables(T):
    half = HEAD_DIM // 2
    inv = ROPE_THETA ** (-jnp.arange(half, dtype=F32) / half)
    ang = jnp.arange(T, dtype=F32)[:, None] * inv[None]
    cos, sin = jnp.cos(ang), jnp.sin(ang)
    cos_t = jnp.tile(cos, (1, LANES // half))
    sin_t = jnp.tile(jnp.concatenate([-sin, sin], axis=1), (1, LANES // HEAD_DIM))
    return cos_t, sin_t


def _selection_constants(T):
    n_half = T // CMP_STRIDE
    n_cmp = (T - CMP_BLOCK) // CMP_STRIDE + 1
    n_sel = T // SEL_BLOCK
    pos = np.arange(n_cmp)[:, None] * CMP_STRIDE + np.arange(CMP_BLOCK)[None]
    ov = ((pos // SEL_BLOCK)[..., None] == np.arange(n_sel)).sum(1) / CMP_BLOCK
    ov_t = np.zeros((n_sel, n_half), np.float32)
    ov_t[:, :n_cmp] = ov.T
    put = (np.arange(LANES)[None, :] == HEAD_DIM + np.arange(n_sel)[:, None]).astype(np.float32)
    return jnp.asarray(ov_t), jnp.asarray(put, BF16)


def kernel(x, c, w_ada, b_ada, w_in, rwkv_mu, rwkv_w0, rwkv_w2, rwkv_a0, rwkv_a2, rwkv_g2, rwkv_k_k, rwkv_k_a, rwkv_r_k, rwkv_lnx_g, rwkv_lnx_b, rwkv_v0, rwkv_v1, rwkv_v2, nsa_cmp_pos, nsa_cmp_w1, nsa_cmp_w2, w_out, ln1_g, ln1_b, mlp_w1, mlp_w2, ln2_g, ln2_b):
    B, T, D = x.shape
    L = w_ada.shape[0]
    alpha = (2 * L) ** 0.25

    c_pad = jnp.pad(c, ((0, -B % 8), (0, 0)))
    mod = _ada_call(c_pad, w_ada, b_ada)[:, :B]
    mod3 = mod.reshape(L * B * 6, 1, D)
    cos_t, sin_t = _rope_tables(T)
    ov_t, sel_put = _selection_constants(T)

    rows = lambda z: z.reshape(z.shape[0], 1, -1)
    w_in_b = _split_w_in(w_in)
    p = {
        "mu": _pad_mu(rwkv_mu), "w0": rows(rwkv_w0), "a0": rows(rwkv_a0),
        "w2": _pad_rows(rwkv_w2, LORA_PAD).astype(BF16), "a2": _pad_rows(rwkv_a2, LORA_PAD).astype(BF16),
        "g2": rwkv_g2.astype(BF16), "k_k": rows(rwkv_k_k), "k_a": rows(rwkv_k_a),
        "r_k": rows(rwkv_r_k), "lnx_g": rows(rwkv_lnx_g), "lnx_b": rows(rwkv_lnx_b),
        "v0": rows(rwkv_v0), "v1": _pad_last(rwkv_v1, LORA_PAD).astype(BF16),
        "v2": _pad_rows(rwkv_v2, LORA_PAD).astype(BF16),
    }
    half_block = CMP_STRIDE * HEAD_DIM
    cmp_pos = nsa_cmp_pos.reshape(2 * L, 2, half_block)
    cmp_w1 = nsa_cmp_w1.reshape((2 * L,) + nsa_cmp_w1.shape[2:])
    cmp_w2 = _pad_last(nsa_cmp_w2.reshape((2 * L,) + nsa_cmp_w2.shape[2:]), LANES)
    w_out_b, w1_b, w2_b = w_out.astype(BF16), mlp_w1.astype(BF16), mlp_w2.astype(BF16)
    ln1 = (rows(ln1_g), rows(ln1_b))
    ln2 = (rows(ln2_g), rows(ln2_b))

    v_first = None
    for i in range(L):
        row0 = i * B * 6
        u_rw, u_ns = _modmm_call(x, mod3, row0, w_in_b, i, RW_COLS, TILES["proj_m"], PROJ_TN)

        r, ld, cs, k, v, kk, a, g = _rwkv_prep_call(u_rw, p, i, v_first, TILES["prep"])
        if i == 0:
            v_first = v
        o_rw = _rwkv_scan_call(r, ld, cs, k, v, kk, a, g, p, i, TILES["scan_rows"], TILES["scan_heads"])

        q, kc_in, vc_in, ks, vs_t, kw, vw_t, gates_t = _nsa_prep_call(u_ns, cos_t, sin_t, TILES["prep"])
        kc = _compress_call(kc_in, cmp_pos, cmp_w1, cmp_w2, 2 * i)
        vc = _compress_call(vc_in, cmp_pos, cmp_w1, cmp_w2, 2 * i + 1)
        o_ns = _nsa_attn_call(q, kc, vc, ks, vs_t, kw, vw_t, gates_t, ov_t, sel_put, TILES["attn_q"], TILES["attn_k"])

        x = _outproj_call(o_rw, o_ns, x, mod3, row0, w_out_b, *ln1, i, alpha, TILES["out_m"])
        x = _mlp_call(x, mod3, row0, w1_b, w2_b, *ln2, i, alpha, TILES["mlp_m"], TILES["mlp_f"])
    return x
```

```python
import functools

import numpy as np
import jax
import jax.numpy as jnp
from jax import lax
from jax.experimental import pallas as pl
from jax.experimental.pallas import tpu as pltpu

F32 = jnp.float32
BF16 = jnp.bfloat16
HI = lax.Precision.HIGHEST

HEAD_DIM = 64
RWKV_HEADS = 16
NSA_HEADS = 16
NSA_KV_HEADS = 4
HEADS_PER_GROUP = NSA_HEADS // NSA_KV_HEADS
D_RWKV = RWKV_HEADS * HEAD_DIM
D_NSA = NSA_HEADS * HEAD_DIM
NSA_KV = NSA_KV_HEADS * HEAD_DIM
DECAY_LORA = 96
AAA_LORA = 96
MV_LORA = 64
GATE_LORA = 256
GN_EPS = 64e-5
CMP_BLOCK = 32
CMP_STRIDE = 16
SEL_BLOCK = 64
N_SELECT = 8
WINDOW = 512
ROPE_THETA = 10000.0
NEG = -1e30
FORCE = 1e4
LN_EPS = 1e-5
LOG2_E = 1.4426950408889634

LANES = 128
LORA_PAD = LANES
RW_XW = 3 * D_RWKV
RW_XA = RW_XW + LORA_PAD
RW_XG = RW_XA + LORA_PAD
RW_COLS = RW_XG + GATE_LORA
NS_KC = D_NSA
NS_GATE = D_NSA + 6 * NSA_KV
NS_COLS = NS_GATE + LANES
GATE_PAD = 16
VT_ROWS = HEAD_DIM + 16
PROJ_TN = 896

CHUNK = 64
V7X_VMEM_BYTES = 64 * 1024 * 1024
VMEM_LIMIT = V7X_VMEM_BYTES - 8 * 1024 * 1024
TILES = dict(proj_m=1024, prep=256, scan_rows=128, scan_heads=8, attn_q=256, attn_k=256, out_m=256, mlp_m=512,
             mlp_f=1024)


def _layer_spec(arr, layer):
    tail = (0,) * (arr.ndim - 1)
    return pl.BlockSpec((1,) + arr.shape[1:], lambda *_: (layer,) + tail)


def _cparams(sem):
    return pltpu.CompilerParams(dimension_semantics=sem, vmem_limit_bytes=VMEM_LIMIT)


def _mm(fn, a, b, mode):
    if mode is None:
        return fn(a, b, None)
    if mode == "hi":
        return fn(a, b, HI)
    ah, bh = a.astype(BF16), b.astype(BF16)
    if mode == "bf":
        return fn(ah, bh, None)
    if mode == "r3":
        r1 = b - bh.astype(F32)
        bm = r1.astype(BF16)
        bl = (r1 - bm.astype(F32)).astype(BF16)
        return fn(ah, bh, None) + (fn(ah, bm, None) + fn(ah, bl, None))
    al = (a - ah.astype(F32)).astype(BF16)
    bl = (b - bh.astype(F32)).astype(BF16)
    return fn(ah, bh, None) + (fn(ah, bl, None) + fn(al, bh, None))


def _dot(a, b, mode=None):
    return _mm(lambda p, q, pr: jnp.dot(p, q, preferred_element_type=F32, precision=pr), a, b, mode)


def _dot_nt(a, b, mode=None):
    return _mm(lambda p, q, pr: lax.dot_general(p, q, (((1,), (1,)), ((), ())), preferred_element_type=F32,
                                               precision=pr), a, b, mode)


def _bmm(a, b, mode=None):
    return _mm(lambda p, q, pr: jnp.einsum("bij,bjk->bik", p, q, preferred_element_type=F32, precision=pr),
               a, b, mode)


def _bmm_nt(a, b, mode=None):
    return _mm(lambda p, q, pr: jnp.einsum("bik,bjk->bij", p, q, preferred_element_type=F32, precision=pr),
               a, b, mode)


def _bmm_tn(a, b, mode=None):
    return _mm(lambda p, q, pr: jnp.einsum("bci,bcj->bij", p, q, preferred_element_type=F32, precision=pr),
               a, b, mode)


def _layer_norm_rows(z, g, b):
    mu = jnp.mean(z, axis=-1, keepdims=True)
    zc = z - mu
    var = jnp.mean(zc * zc, axis=-1, keepdims=True)
    return zc * lax.rsqrt(var + LN_EPS) * g + b


def _ada_kernel(c_ref, w_ref, b_ref, o_ref):
    c = c_ref[...]
    cond = c * jax.nn.sigmoid(c)
    o_ref[0] = _dot(cond.astype(BF16), w_ref[0].astype(BF16)) + b_ref[0]


def _ada_call(c_pad, w_ada, b_ada):
    L, D, N = w_ada.shape
    tn = 1024
    return pl.pallas_call(
        _ada_kernel,
        out_shape=jax.ShapeDtypeStruct((L, c_pad.shape[0], N), F32),
        grid=(L, N // tn),
        in_specs=[
            pl.BlockSpec(c_pad.shape, lambda l, n: (0, 0)),
            pl.BlockSpec((1, D, tn), lambda l, n: (l, 0, n)),
            pl.BlockSpec((1, 1, tn), lambda l, n: (l, 0, n)),
        ],
        out_specs=pl.BlockSpec((1, c_pad.shape[0], tn), lambda l, n: (l, 0, n)),
        compiler_params=_cparams(("parallel", "parallel")),
        name="adaln_mod",
    )(c_pad, w_ada, b_ada.reshape(L, 1, N))


def _modmm_kernel(x_ref, sc_ref, sh_ref, w_ref, o1_ref, o2_ref, h_scr, *, n1):
    n = pl.program_id(2)

    @pl.when(n == 0)
    def _():
        h = x_ref[0] * (1.0 + sc_ref[0]) + sh_ref[0]
        h_scr[...] = h.astype(BF16)

    y = _dot(h_scr[...], w_ref[0])

    @pl.when(n < n1)
    def _():
        o1_ref[0] = y

    @pl.when(n >= n1)
    def _():
        o2_ref[0] = y


def _modmm_call(x, mod3, row0, w_bf16, layer, n_first, tm, tn):
    B, T, D = x.shape
    N = w_bf16.shape[2]
    n1 = n_first // tn
    return pl.pallas_call(
        functools.partial(_modmm_kernel, n1=n1),
        out_shape=[jax.ShapeDtypeStruct((B, T, n_first), F32), jax.ShapeDtypeStruct((B, T, N - n_first), F32)],
        grid=(B, T // tm, N // tn),
        in_specs=[
            pl.BlockSpec((1, tm, D), lambda b, m, n: (b, m, 0)),
            pl.BlockSpec((1, 1, D), lambda b, m, n: (row0 + 6 * b + 1, 0, 0)),
            pl.BlockSpec((1, 1, D), lambda b, m, n: (row0 + 6 * b, 0, 0)),
            pl.BlockSpec((1, D, tn), lambda b, m, n: (layer, 0, n)),
        ],
        out_specs=[pl.BlockSpec((1, tm, tn), lambda b, m, n: (b, m, jnp.minimum(n, n1 - 1))),
                   pl.BlockSpec((1, tm, tn), lambda b, m, n: (b, m, jnp.maximum(n - n1, 0)))],
        scratch_shapes=[pltpu.VMEM((tm, D), BF16)],
        compiler_params=_cparams(("parallel", "parallel", "arbitrary")),
        name="inproj",
    )(x, mod3, mod3, w_bf16)


def _rwkv_prep_kernel(*refs, tt, first_layer):
    if first_layer:
        (u_ref, up_ref, mu_ref, w0_ref, w2_ref, a0_ref, a2_ref, g2_ref, kk_ref, ka_ref,
         r_o, ld_o, cs_o, k_o, v_o, kk_o, a_o, g_o, sh_scr) = refs
    else:
        (u_ref, up_ref, mu_ref, w0_ref, w2_ref, a0_ref, a2_ref, g2_ref, kk_ref, ka_ref,
         vf_ref, v0_ref, v1_ref, v2_ref,
         r_o, ld_o, cs_o, k_o, v_o, kk_o, a_o, g_o, sh_scr) = refs
    ti = pl.program_id(1)
    u = u_ref[0]
    prev = jnp.where(ti > 0, up_ref[0], 0.0)
    sh_scr[0:8, :] = prev
    sh_scr[8:8 + tt, :] = u
    us = sh_scr[7:7 + tt, :]
    x = u + (us - u) * mu_ref[0]
    r = x[:, 0:D_RWKV]
    k = x[:, D_RWKV:2 * D_RWKV]
    v = x[:, 2 * D_RWKV:3 * D_RWKV]
    xw = x[:, RW_XW:RW_XA]
    xa = x[:, RW_XA:RW_XG]
    xg = x[:, RW_XG:RW_COLS]
    w = -jax.nn.softplus(-(w0_ref[0] + _dot(jnp.tanh(xw).astype(BF16), w2_ref[0]))) - 0.5
    ld = -jnp.exp(w)
    ld_o[0] = ld
    row = lax.broadcasted_iota(jnp.int32, (tt, tt), 0)
    col = lax.broadcasted_iota(jnp.int32, (tt, tt), 1)
    chunk_ltri = jnp.where((row // CHUNK == col // CHUNK) & (col <= row), 1.0, 0.0)
    cs_o[0] = _dot(chunk_ltri, ld, "r3")
    a = jax.nn.sigmoid(a0_ref[0] + _dot(xa.astype(BF16), a2_ref[0]))
    g_o[0] = _dot(jax.nn.sigmoid(xg).astype(BF16), g2_ref[0])
    if not first_layer:
        lo = _dot(v.astype(BF16), v1_ref[0])
        gate = jax.nn.sigmoid(v0_ref[0] + _dot(lo.astype(BF16), v2_ref[0]))
        v = v + (vf_ref[0] - v) * gate
    r_o[0] = r
    v_o[0] = v
    a_o[0] = a
    kk_o[0] = k * kk_ref[0]
    k_o[0] = k * (1.0 + (a - 1.0) * ka_ref[0])


def _rwkv_prep_call(u_rw, p, layer, v_first, tt):
    B, T, _ = u_rw.shape
    first_layer = v_first is None
    tile = pl.BlockSpec((1, tt, D_RWKV), lambda b, t: (b, t, 0))
    names = ["mu", "w0", "w2", "a0", "a2", "g2", "k_k", "k_a"]
    in_specs = [
        pl.BlockSpec((1, tt, RW_COLS), lambda b, t: (b, t, 0)),
        pl.BlockSpec((1, 8, RW_COLS), lambda b, t: (b, jnp.maximum(t * (tt // 8) - 1, 0), 0)),
    ] + [_layer_spec(p[nm], layer) for nm in names]
    args = [u_rw, u_rw] + [p[nm] for nm in names]
    if not first_layer:
        in_specs += [tile] + [_layer_spec(p[nm], layer - 1) for nm in ("v0", "v1", "v2")]
        args += [v_first, p["v0"], p["v1"], p["v2"]]
    out = jax.ShapeDtypeStruct((B, T, D_RWKV), F32)
    return pl.pallas_call(
        functools.partial(_rwkv_prep_kernel, tt=tt, first_layer=first_layer),
        out_shape=[out] * 8,
        grid=(B, T // tt),
        in_specs=in_specs,
        out_specs=[tile] * 8,
        scratch_shapes=[pltpu.VMEM((tt + 8, RW_COLS), F32)],
        compiler_params=_cparams(("parallel", "parallel")),
        name="rwkv_prep",
    )(*args)


def _rwkv_scan_kernel(r_ref, ld_ref, cs_ref, k_ref, v_ref, kk_ref, a_ref, g_ref, rk_ref, lg_ref, lb_ref,
                      o_ref, s_scr, wr_scr, o0_scr, pm_scr, qm_scr, bn_scr, g_scr, *, ts, nt, hpb):
    C = CHUNK
    nc = ts // C
    nb = hpb * nc
    N = HEAD_DIM
    n = pl.program_id(0)

    @pl.when(n == 0)
    def _():
        for scr in (s_scr, wr_scr, o0_scr, pm_scr, qm_scr, bn_scr, g_scr):
            scr[...] = jnp.zeros_like(scr)

    first = lax.rem(jnp.maximum(n - 1, 0), nt) == 0
    outs = []
    for hh in range(hpb):
        S = jnp.where(first, 0.0, s_scr[hh])
        for c in range(nc):
            i = hh * nc + c
            Sb = S.astype(BF16)
            outs.append(_dot_nt(wr_scr[i], Sb) + o0_scr[i])
            S = _dot(Sb, pm_scr[i]) + qm_scr[i]
        s_scr[hh] = S
    o = jnp.stack(outs, axis=0)

    def split_row(ref):
        x = ref[0]
        return jnp.concatenate([jnp.broadcast_to(x[None, :, h * N:(h + 1) * N], (nc, 1, N)) for h in range(hpb)],
                               axis=0)

    mu = jnp.mean(o, axis=-1, keepdims=True)
    oc = o - mu
    var = jnp.mean(oc * oc, axis=-1, keepdims=True)
    on = oc * lax.rsqrt(var + GN_EPS) * split_row(lg_ref) + split_row(lb_ref)
    res = (on + bn_scr[...]) * g_scr[...]
    o_ref[0] = jnp.concatenate([res[h * nc:(h + 1) * nc].reshape(ts, N) for h in range(hpb)],
                               axis=-1).astype(o_ref.dtype)

    def split(ref):
        x = ref[0].reshape(nc, C, hpb * N)
        return jnp.concatenate([x[:, :, h * N:(h + 1) * N] for h in range(hpb)], axis=0)

    r, ld, cs, k, v, kkr, a, g = (split(z) for z in (r_ref, ld_ref, cs_ref, k_ref, v_ref, kk_ref, a_ref, g_ref))
    kkn = kkr / jnp.maximum(jnp.sqrt(jnp.sum(kkr * kkr, axis=-1, keepdims=True)), 1e-12)
    row = lax.broadcasted_iota(jnp.int32, (C, C), 0)
    col = lax.broadcasted_iota(jnp.int32, (C, C), 1)
    incl = col <= row
    strict = col < row
    eye = col == row
    cs_last = cs[:, C - 1:C, :]
    e_in = jnp.exp(cs)
    e_ex = jnp.exp(cs - ld)
    e_neg = jnp.exp(-cs)
    e_hat = jnp.exp(cs_last - cs)
    at = -kkn * e_ex
    b = kkn * a
    bt = b * e_neg
    kt = k * e_neg
    rt = r * e_in
    bh = (b * e_hat).astype(BF16)
    kh = (k * e_hat).astype(BF16)
    vb = v.astype(BF16)
    A = _bmm_nt(jnp.concatenate([at, rt], axis=1).astype(BF16), jnp.concatenate([bt, kt], axis=1).astype(BF16))
    a_ab = jnp.where(strict, A[:, :C, :C], 0.0)
    a_ak = jnp.where(strict, A[:, :C, C:], 0.0).astype(BF16)
    a_rb = jnp.where(incl, A[:, C:, :C], 0.0).astype(BF16)
    a_rk = jnp.where(incl, A[:, C:, C:], 0.0).astype(BF16)
    npow = a_ab.astype(BF16)
    tinv = jnp.where(eye, 1.0, 0.0).astype(F32) + a_ab
    p2 = 2
    while p2 < C:
        npow_f = _bmm(npow, npow)
        npow = npow_f.astype(BF16)
        tinv = tinv + _bmm(npow, tinv.astype(BF16))
        p2 *= 2
    akv = _bmm(a_ak, vb)
    x = _bmm(tinv.astype(BF16), jnp.concatenate([at, akv], axis=-1).astype(BF16))
    xb = x.astype(BF16)
    y = _bmm(a_rb, xb)
    pq = _bmm_tn(xb, bh)
    wr_scr[...] = (rt + y[:, :, :N]).astype(BF16)
    o0_scr[...] = y[:, :, N:] + _bmm(a_rk, vb)
    pm_scr[...] = (pq[:, :N, :] + jnp.where(eye, jnp.exp(cs_last), 0.0)).astype(BF16)
    qm_scr[...] = pq[:, N:, :] + _bmm_tn(vb, kh)
    bn_scr[...] = jnp.sum(r * k * split_row(rk_ref), axis=-1, keepdims=True) * v
    g_scr[...] = g


def _rwkv_scan_call(r, ld, cs, k, v, kk, a, g, p, layer, ts, hpb):
    B, T, _ = r.shape
    nt = T // ts
    hg = RWKV_HEADS // hpb
    nblk = B * hg * nt
    nb = hpb * (ts // CHUNK)
    w = hpb * HEAD_DIM

    def blk(n):
        return n // (hg * nt), lax.rem(n, nt), lax.rem(n // nt, hg)

    cur = lambda n: blk(jnp.minimum(n, nblk - 1))
    prev = lambda n: blk(jnp.maximum(n - 1, 0))
    tile = pl.BlockSpec((1, ts, w), cur)
    rowp = lambda f: pl.BlockSpec((1, 1, w), lambda n: (layer, 0, f(n)[2]))
    sq = lambda dt: pltpu.VMEM((nb, HEAD_DIM, HEAD_DIM), dt)
    return pl.pallas_call(
        functools.partial(_rwkv_scan_kernel, ts=ts, nt=nt, hpb=hpb),
        out_shape=jax.ShapeDtypeStruct((B, T, D_RWKV), BF16),
        grid=(nblk + 1,),
        in_specs=[tile] * 8 + [rowp(cur), rowp(prev), rowp(prev)],
        out_specs=pl.BlockSpec((1, ts, w), prev),
        scratch_shapes=[pltpu.VMEM((hpb, HEAD_DIM, HEAD_DIM), F32), pltpu.VMEM((nb, CHUNK, HEAD_DIM), BF16),
                        pltpu.VMEM((nb, CHUNK, HEAD_DIM), F32), sq(BF16), sq(F32),
                        pltpu.VMEM((nb, CHUNK, HEAD_DIM), F32), pltpu.VMEM((nb, CHUNK, HEAD_DIM), F32)],
        compiler_params=_cparams(("arbitrary",)),
        name="rwkv_scan",
    )(r, ld, cs, k, v, kk, a, g, p["r_k"], p["lnx_g"], p["lnx_b"])


def _nsa_prep_kernel(u_ref, cos_ref, sin_ref, q_o, kc_o, vc_o, ks_o, vs_o, kw_o, vw_o, gate_o):
    cos = cos_ref[...]
    sin = sin_ref[...]
    lane = lax.broadcasted_iota(jnp.int32, cos.shape, 1)
    first_half = (lane % HEAD_DIM) < (HEAD_DIM // 2)

    def rope(x):
        other = jnp.where(first_half, pltpu.roll(x, LANES - HEAD_DIM // 2, 1), pltpu.roll(x, HEAD_DIM // 2, 1))
        return x * cos + other * sin

    def put(out_ref, col0, nheads, roped, scale=None):
        for j in range(nheads // 2):
            x = u_ref[0, :, col0 + j * LANES:col0 + (j + 1) * LANES]
            if roped:
                x = rope(x)
            if scale is not None:
                x = x * scale
            out_ref[0, 2 * j] = x[:, :HEAD_DIM].astype(out_ref.dtype)
            out_ref[0, 2 * j + 1] = x[:, HEAD_DIM:].astype(out_ref.dtype)

    low = lane < HEAD_DIM

    def put_wide(out_ref, col0, nheads, scale, tail):
        for j in range(nheads // 2):
            x = rope(u_ref[0, :, col0 + j * LANES:col0 + (j + 1) * LANES])
            if scale is not None:
                x = x * scale
            out_ref[0, 2 * j] = jnp.where(low, x, tail).astype(out_ref.dtype)
            out_ref[0, 2 * j + 1] = jnp.where(low, pltpu.roll(x, HEAD_DIM, 1), tail).astype(out_ref.dtype)

    def put_t(out_ref, col0, nheads):
        extra = out_ref.shape[2] - HEAD_DIM
        ones_row = jnp.where(lax.broadcasted_iota(jnp.int32, (extra, cos.shape[0]), 0) == 0, 1.0, 0.0)
        for j in range(nheads // 2):
            xt = u_ref[0, :, col0 + j * LANES:col0 + (j + 1) * LANES].T
            for i, part in enumerate((xt[:HEAD_DIM, :], xt[HEAD_DIM:, :])):
                out_ref[0, 2 * j + i, :HEAD_DIM, :] = part.astype(out_ref.dtype)
                out_ref[0, 2 * j + i, HEAD_DIM:, :] = ones_row.astype(out_ref.dtype)

    pos = pl.program_id(1) * cos.shape[0] + lax.broadcasted_iota(jnp.int32, cos.shape, 0)
    blk_onehot = jnp.where(lane - HEAD_DIM == pos // SEL_BLOCK, 1.0, 0.0)
    put_wide(q_o, 0, NSA_HEADS, HEAD_DIM ** -0.5 * LOG2_E, 0.0)
    put(kc_o, NS_KC, NSA_KV_HEADS, True)
    put(vc_o, NS_KC + NSA_KV, NSA_KV_HEADS, False)
    put_wide(ks_o, NS_KC + 2 * NSA_KV, NSA_KV_HEADS, None, blk_onehot)
    put_t(vs_o, NS_KC + 3 * NSA_KV, NSA_KV_HEADS)
    put_wide(kw_o, NS_KC + 4 * NSA_KV, NSA_KV_HEADS, None, 0.0)
    put_t(vw_o, NS_KC + 5 * NSA_KV, NSA_KV_HEADS)
    gates_t = jax.nn.sigmoid(u_ref[0, :, NS_GATE:NS_GATE + LANES]).T
    for gi in range(NSA_KV_HEADS):
        gate_o[0, gi] = gates_t[gi * GATE_PAD:(gi + 1) * GATE_PAD, :]


def _nsa_prep_call(u_ns, cos_t, sin_t, tt):
    B, T, _ = u_ns.shape
    G, H, N = NSA_KV_HEADS, NSA_HEADS, HEAD_DIM
    assert T // SEL_BLOCK <= LANES - N
    kv = jax.ShapeDtypeStruct((B, G, T, N), F32)
    kv_wide = jax.ShapeDtypeStruct((B, G, T, LANES), BF16)
    kv_t = jax.ShapeDtypeStruct((B, G, VT_ROWS, T), BF16)
    kv_spec = pl.BlockSpec((1, G, tt, N), lambda b, t: (b, 0, t, 0))
    kvw_spec = pl.BlockSpec((1, G, tt, LANES), lambda b, t: (b, 0, t, 0))
    kvt_spec = pl.BlockSpec((1, G, VT_ROWS, tt), lambda b, t: (b, 0, 0, t))
    return pl.pallas_call(
        _nsa_prep_kernel,
        out_shape=[jax.ShapeDtypeStruct((B, H, T, LANES), BF16), kv, kv, kv_wide, kv_t,
                   kv_wide, kv_t, jax.ShapeDtypeStruct((B, G, GATE_PAD, T), F32)],
        grid=(B, T // tt),
        in_specs=[
            pl.BlockSpec((1, tt, NS_COLS), lambda b, t: (b, t, 0)),
            pl.BlockSpec((tt, LANES), lambda b, t: (t, 0)),
            pl.BlockSpec((tt, LANES), lambda b, t: (t, 0)),
        ],
        out_specs=[pl.BlockSpec((1, H, tt, LANES), lambda b, t: (b, 0, t, 0)), kv_spec, kv_spec, kvw_spec, kvt_spec,
                   kvw_spec, kvt_spec, pl.BlockSpec((1, G, GATE_PAD, tt), lambda b, t: (b, 0, 0, t))],
        compiler_params=_cparams(("parallel", "parallel")),
        name="nsa_prep",
    )(u_ns, cos_t, sin_t)


def _compress_kernel(x_ref, pos_ref, w1_ref, w2_ref, o_ref):
    x = x_ref[0, 0]
    half = x.shape[1]
    y_top = _dot(x + pos_ref[0, 0:1, :], w1_ref[0, :half, :], "x3")
    y_bot = _dot(x + pos_ref[0, 1:2, :], w1_ref[0, half:, :], "x3")
    pre = y_top + pltpu.roll(y_bot, x.shape[0] - 1, 0)
    o_ref[0, 0] = _dot(jax.nn.gelu(pre), w2_ref[0], "x3")


def _compress_call(kv, pos, w1, w2, which):
    B, G, T, N = kv.shape
    n_half = T // CMP_STRIDE
    width = CMP_STRIDE * N
    hidden = w1.shape[-1]
    x = kv.reshape(B, G, n_half, width)
    return pl.pallas_call(
        _compress_kernel,
        out_shape=jax.ShapeDtypeStruct((B, G, n_half, w2.shape[-1]), F32),
        grid=(B, G),
        in_specs=[
            pl.BlockSpec((1, 1, n_half, width), lambda b, g: (b, g, 0, 0)),
            pl.BlockSpec((1, 2, width), lambda b, g: (which, 0, 0)),
            pl.BlockSpec((1, 2 * width, hidden), lambda b, g: (which, 0, 0)),
            pl.BlockSpec((1, hidden, w2.shape[-1]), lambda b, g: (which, 0, 0)),
        ],
        out_specs=pl.BlockSpec((1, 1, n_half, w2.shape[-1]), lambda b, g: (b, g, 0, 0)),
        compiler_params=_cparams(("parallel", "parallel")),
        name="nsa_compress",
    )(x, pos, w1, w2)


def _nsa_attn_kernel(q_ref, kc_ref, vc_ref, ks_ref, vst_ref, kw_ref, vwt_ref, gate_ref, ov_ref, put_ref,
                     o_ref, s0_scr, s1_scr, *, tq, tk):
    HPG, N = HEADS_PER_GROUP, HEAD_DIM
    qi = pl.program_id(2)
    q0 = qi * tq
    k_len = ks_ref.shape[2]

    wq = HPG * tq
    q_win = q_ref[0].reshape(wq, LANES)
    rel = lax.broadcasted_iota(jnp.int32, (tk, tq), 1) - lax.broadcasted_iota(jnp.int32, (tk, tq), 0)
    init = (jnp.full((1, wq), NEG, F32), jnp.zeros((VT_ROWS, wq), F32))
    n_tiles = k_len // tk

    def scores(q_all, k_ref, k0, dst):
        dst[...] = _dot_nt(k_ref[0, 0, pl.ds(k0, tk), :], q_all)

    def update(carry, src, vt_ref, k0, bias):
        m_i, acc = carry
        s = src[...]
        if bias is not None:
            s = s + (bias if bias.ndim == 0 else jnp.concatenate([bias] * HPG, axis=1))
        m_n = jnp.maximum(m_i, jnp.max(s, axis=0, keepdims=True))
        p = jnp.exp2(s - m_n)
        return m_n, jnp.exp2(m_i - m_n) * acc + _dot(vt_ref[0, 0, :, pl.ds(k0, tk)], p.astype(BF16))

    def split(o):
        return [o[:, h * tq:(h + 1) * tq] for h in range(HPG)]

    def finish(carry):
        return split(carry[1][:N] / carry[1][N:N + 1])

    def window_tile(e):
        koff = e * tk - WINDOW
        lo, hi = -(tk - 1) - koff, (tq - 1) - koff
        bias = None
        if lo < 0 or hi >= WINDOW:
            dist = rel - koff
            bias = jnp.where((dist >= 0) & (dist < WINDOW), 0.0, NEG)
        if koff < 0:
            bias = jnp.where(q0 + koff >= 0, 0.0 if bias is None else bias, NEG)
        return pl.multiple_of(jnp.maximum(q0 + koff, 0), tk), bias

    win = [window_tile(e) for e in range((WINDOW + tq) // tk)]
    assert len(win) == 3
    bufs = (s0_scr, s1_scr)
    scores(q_win, kw_ref, win[0][0], bufs[0])

    n_cmp = kc_ref.shape[2]
    c_c = lax.broadcasted_iota(jnp.int32, (n_cmp, tq), 0)
    t_c = q0 + lax.broadcasted_iota(jnp.int32, (n_cmp, tq), 1)
    cmask = jnp.concatenate([c_c * CMP_STRIDE + (CMP_BLOCK - 1) <= t_c] * HPG, axis=1)
    s = jnp.where(cmask, _dot_nt(kc_ref[0, 0].astype(BF16), q_win), NEG)
    scores(q_win, kw_ref, win[1][0], bufs[1])
    m = jnp.max(s, axis=0, keepdims=True)
    e = jnp.where(cmask, jnp.exp2(s - m), 0.0)
    l = jnp.sum(e, axis=0, keepdims=True)
    p = e / jnp.maximum(l, 1e-30)
    o_cmp = split(_dot(vc_ref[0, 0].T[:N].astype(BF16), p.astype(BF16)))
    psum = p[:, :tq]
    for h in range(1, HPG):
        psum = psum + p[:, h * tq:(h + 1) * tq]

    n_sel = ov_ref.shape[0]
    imp = _dot(ov_ref[...], psum, "r3")
    carry = update(init, bufs[0], vwt_ref, *win[0])
    scores(q_win, kw_ref, win[2][0], bufs[0])
    blk = lax.broadcasted_iota(jnp.int32, (n_sel, tq), 0)
    t_s = q0 + lax.broadcasted_iota(jnp.int32, (n_sel, tq), 1)
    cur = t_s // SEL_BLOCK
    forced = (blk == 0) | (blk == cur) | (blk == cur - 1)
    val = jnp.where(forced, FORCE, jnp.where(blk * SEL_BLOCK <= t_s, imp, -1.0))
    rank = jnp.zeros((n_sel, tq), F32)
    for i in range(n_sel):
        vi = val[i:i + 1, :]
        ahead = (vi > val) | ((vi == val) & (blk > i))
        rank = rank + jnp.where(ahead, 1.0, 0.0)
    unsel_t = jnp.where(rank < float(min(N_SELECT, n_sel)), 0.0, NEG)
    q_bias = lax.dot_general(unsel_t.astype(BF16), put_ref[...], (((0,), (0,)), ((), ())),
                             preferred_element_type=F32)
    carry = update(carry, bufs[1], vwt_ref, *win[1])
    q_sel = jnp.concatenate([(q_ref[0, h].astype(F32) + q_bias).astype(BF16) for h in range(HPG)], axis=0)
    o_win = finish(update(carry, bufs[0], vwt_ref, *win[2]))

    def key0(j):
        return pl.multiple_of(j * tk, tk)

    scores(q_sel, ks_ref, key0(0), s0_scr)

    def pair(jj, carry):
        scores(q_sel, ks_ref, key0(2 * jj + 1), s1_scr)
        carry = update(carry, s0_scr, vst_ref, key0(2 * jj), None)
        scores(q_sel, ks_ref, key0(2 * jj + 2), s0_scr)
        return update(carry, s1_scr, vst_ref, key0(2 * jj + 1), None)

    carry = lax.fori_loop(0, qi // 2, pair, init)
    t1 = 2 * (qi // 2)
    t2 = jnp.minimum(t1 + 1, n_tiles - 1)

    def causal(t):
        return jnp.where(rel >= (t - qi) * tk, 0.0, NEG)

    scores(q_sel, ks_ref, key0(t2), s1_scr)
    carry = update(carry, s0_scr, vst_ref, key0(t1), causal(t1))
    o_slc = finish(update(carry, s1_scr, vst_ref, key0(t2), causal(t1 + 1)))

    gate = gate_ref[0, 0]
    outs = [gate[3 * h:3 * h + 1, :] * o_cmp[h] + gate[3 * h + 1:3 * h + 2, :] * o_slc[h]
            + gate[3 * h + 2:3 * h + 3, :] * o_win[h] for h in range(HPG)]
    o_ref[0] = jnp.concatenate(outs, axis=0).T.astype(o_ref.dtype)


def _nsa_attn_call(q, kc, vc, ks, vs_t, kw, vw_t, gates_t, ov_t, put, tq, tk):
    B, H, T, _ = q.shape
    N = HEAD_DIM
    G, HPG = NSA_KV_HEADS, HEADS_PER_GROUP
    assert WINDOW % tk == 0 and tq == tk
    n_half = kc.shape[2]
    cmp_spec = pl.BlockSpec((1, 1, n_half, LANES), lambda b, g, t: (b, g, 0, 0))
    kv_spec = pl.BlockSpec((1, 1, T, LANES), lambda b, g, t: (b, g, 0, 0))
    kvt_spec = pl.BlockSpec((1, 1, VT_ROWS, T), lambda b, g, t: (b, g, 0, 0))
    return pl.pallas_call(
        functools.partial(_nsa_attn_kernel, tq=tq, tk=tk),
        out_shape=jax.ShapeDtypeStruct((B, T, D_NSA), BF16),
        grid=(B, G, T // tq),
        in_specs=[
            pl.BlockSpec((1, HPG, tq, LANES), lambda b, g, t: (b, g, t, 0)),
            cmp_spec, cmp_spec, kv_spec, kvt_spec, kv_spec, kvt_spec,
            pl.BlockSpec((1, 1, GATE_PAD, tq), lambda b, g, t: (b, g, 0, t)),
            pl.BlockSpec(ov_t.shape, lambda b, g, t: (0, 0)),
            pl.BlockSpec(put.shape, lambda b, g, t: (0, 0)),
        ],
        out_specs=pl.BlockSpec((1, tq, HPG * N), lambda b, g, t: (b, t, g)),
        scratch_shapes=[pltpu.VMEM((tk, HPG * tq), F32), pltpu.VMEM((tk, HPG * tq), F32)],
        compiler_params=_cparams(("parallel", "parallel", "arbitrary")),
        name="nsa_attn",
    )(q, kc, vc, ks, vs_t, kw, vw_t, gates_t, ov_t, put)


def _outproj_kernel(orw_ref, ons_ref, x_ref, gt_ref, w_ref, g_ref, b_ref, o_ref, *, alpha):
    half = orw_ref.shape[-1]
    y = _dot(orw_ref[0].astype(BF16), w_ref[0, :half, :]) + _dot(ons_ref[0].astype(BF16), w_ref[0, half:, :])
    z = alpha * x_ref[0] + (1.0 + gt_ref[0]) * y
    o_ref[0] = _layer_norm_rows(z, g_ref[0], b_ref[0])


def _outproj_call(o_rw, o_ns, x, mod3, row0, w_bf16, ln_g, ln_b, layer, alpha, tm):
    B, T, D = x.shape
    half = o_rw.shape[-1]
    return pl.pallas_call(
        functools.partial(_outproj_kernel, alpha=alpha),
        out_shape=jax.ShapeDtypeStruct((B, T, D), F32),
        grid=(B, T // tm),
        in_specs=[
            pl.BlockSpec((1, tm, half), lambda b, m: (b, m, 0)),
            pl.BlockSpec((1, tm, half), lambda b, m: (b, m, 0)),
            pl.BlockSpec((1, tm, D), lambda b, m: (b, m, 0)),
            pl.BlockSpec((1, 1, D), lambda b, m: (row0 + 6 * b + 2, 0, 0)),
            _layer_spec(w_bf16, layer), _layer_spec(ln_g, layer), _layer_spec(ln_b, layer),
        ],
        out_specs=pl.BlockSpec((1, tm, D), lambda b, m: (b, m, 0)),
        compiler_params=_cparams(("parallel", "parallel")),
        name="outproj_ln",
    )(o_rw, o_ns, x, mod3, w_bf16, ln_g, ln_b)


def _mlp_kernel(x_ref, sc_ref, sh_ref, gt_ref, w1_ref, w2_ref, g_ref, b_ref, o_ref, h_scr, acc_scr, *, alpha):
    f = pl.program_id(2)

    @pl.when(f == 0)
    def _():
        h = x_ref[0] * (1.0 + sc_ref[0]) + sh_ref[0]
        h_scr[...] = h.astype(BF16)
        acc_scr[...] = jnp.zeros_like(acc_scr)

    a = jnp.maximum(_dot(h_scr[...], w1_ref[0]), 0.0)
    acc_scr[...] += _dot((a * a).astype(BF16), w2_ref[0])

    @pl.when(f == pl.num_programs(2) - 1)
    def _():
        z = alpha * x_ref[0] + (1.0 + gt_ref[0]) * acc_scr[...]
        o_ref[0] = _layer_norm_rows(z, g_ref[0], b_ref[0])


def _mlp_call(x, mod3, row0, w1_bf16, w2_bf16, ln_g, ln_b, layer, alpha, tm, tf):
    B, T, D = x.shape
    FF = w1_bf16.shape[2]
    modspec = lambda j: pl.BlockSpec((1, 1, D), lambda b, m, f: (row0 + 6 * b + j, 0, 0))
    return pl.pallas_call(
        functools.partial(_mlp_kernel, alpha=alpha),
        out_shape=jax.ShapeDtypeStruct((B, T, D), F32),
        grid=(B, T // tm, FF // tf),
        in_specs=[
            pl.BlockSpec((1, tm, D), lambda b, m, f: (b, m, 0)),
            modspec(4), modspec(3), modspec(5),
            pl.BlockSpec((1, D, tf), lambda b, m, f: (layer, 0, f)),
            pl.BlockSpec((1, tf, D), lambda b, m, f: (layer, f, 0)),
            _layer_spec(ln_g, layer), _layer_spec(ln_b, layer),
        ],
        out_specs=pl.BlockSpec((1, tm, D), lambda b, m, f: (b, m, 0)),
        scratch_shapes=[pltpu.VMEM((tm, D), BF16), pltpu.VMEM((tm, D), F32)],
        compiler_params=_cparams(("parallel", "parallel", "arbitrary")),
        name="mlp_ln",
    )(x, mod3, mod3, mod3, w1_bf16, w2_bf16, ln_g, ln_b)


def _pad_last(w, n):
    return jnp.pad(w, [(0, 0)] * (w.ndim - 1) + [(0, n - w.shape[-1])])


def _pad_rows(w, n):
    return jnp.pad(w, [(0, 0)] * (w.ndim - 2) + [(0, n - w.shape[-2]), (0, 0)])


def _split_w_in(w_in):
    c = np.cumsum([0, D_RWKV, D_RWKV, D_RWKV, DECAY_LORA, AAA_LORA, GATE_LORA]).tolist()
    w_in = w_in.astype(BF16)
    rw = jnp.concatenate([w_in[..., c[0]:c[3]], _pad_last(w_in[..., c[3]:c[4]], LORA_PAD),
                          _pad_last(w_in[..., c[4]:c[5]], LORA_PAD), w_in[..., c[5]:c[6]]], axis=-1)
    ng = 3 * HEADS_PER_GROUP
    lead = w_in.shape[:-1]
    gates = _pad_last(w_in[..., c[6] + NS_GATE:].reshape(lead + (NSA_KV_HEADS, ng)), GATE_PAD)
    gates = _pad_last(gates.reshape(lead + (NSA_KV_HEADS * GATE_PAD,)), LANES)
    ns = jnp.concatenate([w_in[..., c[6]:c[6] + NS_GATE], gates], axis=-1)
    return jnp.concatenate([rw, ns], axis=-1)


def _pad_mu(mu):
    c = np.cumsum([0, 3 * D_RWKV, DECAY_LORA, AAA_LORA, GATE_LORA]).tolist()
    parts = [mu[:, c[0]:c[1]], _pad_last(mu[:, c[1]:c[2]], LORA_PAD), _pad_last(mu[:, c[2]:c[3]], LORA_PAD),
             mu[:, c[3]:c[4]]]
    return jnp.concatenate(parts, axis=-1)[:, None, :]


def _rope_tables(T):
    half = HEAD_DIM // 2
    inv = ROPE_THETA ** (-jnp.arange(half, dtype=F32) / half)
    ang = jnp.arange(T, dtype=F32)[:, None] * inv[None]
    cos, sin = jnp.cos(ang), jnp.sin(ang)
    cos_t = jnp.tile(cos, (1, LANES // half))
    sin_t = jnp.tile(jnp.concatenate([-sin, sin], axis=1), (1, LANES // HEAD_DIM))
    return cos_t, sin_t


def _selection_constants(T):
    n_half = T // CMP_STRIDE
    n_cmp = (T - CMP_BLOCK) // CMP_STRIDE + 1
    n_sel = T // SEL_BLOCK
    pos = np.arange(n_cmp)[:, None] * CMP_STRIDE + np.arange(CMP_BLOCK)[None]
    ov = ((pos // SEL_BLOCK)[..., None] == np.arange(n_sel)).sum(1) / CMP_BLOCK
    ov_t = np.zeros((n_sel, n_half), np.float32)
    ov_t[:, :n_cmp] = ov.T
    put = (np.arange(LANES)[None, :] == HEAD_DIM + np.arange(n_sel)[:, None]).astype(np.float32)
    return jnp.asarray(ov_t), jnp.asarray(put, BF16)


def kernel(x, c, w_ada, b_ada, w_in, rwkv_mu, rwkv_w0, rwkv_w2, rwkv_a0, rwkv_a2, rwkv_g2, rwkv_k_k, rwkv_k_a, rwkv_r_k, rwkv_lnx_g, rwkv_lnx_b, rwkv_v0, rwkv_v1, rwkv_v2, nsa_cmp_pos, nsa_cmp_w1, nsa_cmp_w2, w_out, ln1_g, ln1_b, mlp_w1, mlp_w2, ln2_g, ln2_b):
    B, T, D = x.shape
    L = w_ada.shape[0]
    alpha = (2 * L) ** 0.25

    c_pad = jnp.pad(c, ((0, -B % 8), (0, 0)))
    mod = _ada_call(c_pad, w_ada, b_ada)[:, :B]
    mod3 = mod.reshape(L * B * 6, 1, D)
    cos_t, sin_t = _rope_tables(T)
    ov_t, sel_put = _selection_constants(T)

    rows = lambda z: z.reshape(z.shape[0], 1, -1)
    w_in_b = _split_w_in(w_in)
    p = {
        "mu": _pad_mu(rwkv_mu), "w0": rows(rwkv_w0), "a0": rows(rwkv_a0),
        "w2": _pad_rows(rwkv_w2, LORA_PAD).astype(BF16), "a2": _pad_rows(rwkv_a2, LORA_PAD).astype(BF16),
        "g2": rwkv_g2.astype(BF16), "k_k": rows(rwkv_k_k), "k_a": rows(rwkv_k_a),
        "r_k": rows(rwkv_r_k), "lnx_g": rows(rwkv_lnx_g), "lnx_b": rows(rwkv_lnx_b),
        "v0": rows(rwkv_v0), "v1": _pad_last(rwkv_v1, LORA_PAD).astype(BF16),
        "v2": _pad_rows(rwkv_v2, LORA_PAD).astype(BF16),
    }
    half_block = CMP_STRIDE * HEAD_DIM
    cmp_pos = nsa_cmp_pos.reshape(2 * L, 2, half_block)
    cmp_w1 = nsa_cmp_w1.reshape((2 * L,) + nsa_cmp_w1.shape[2:])
    cmp_w2 = _pad_last(nsa_cmp_w2.reshape((2 * L,) + nsa_cmp_w2.shape[2:]), LANES)
    w_out_b, w1_b, w2_b = w_out.astype(BF16), mlp_w1.astype(BF16), mlp_w2.astype(BF16)
    ln1 = (rows(ln1_g), rows(ln1_b))
    ln2 = (rows(ln2_g), rows(ln2_b))

    v_first = None
    for i in range(L):
        row0 = i * B * 6
        u_rw, u_ns = _modmm_call(x, mod3, row0, w_in_b, i, RW_COLS, TILES["proj_m"], PROJ_TN)

        r, ld, cs, k, v, kk, a, g = _rwkv_prep_call(u_rw, p, i, v_first, TILES["prep"])
        if i == 0:
            v_first = v
        o_rw = _rwkv_scan_call(r, ld, cs, k, v, kk, a, g, p, i, TILES["scan_rows"], TILES["scan_heads"])

        q, kc_in, vc_in, ks, vs_t, kw, vw_t, gates_t = _nsa_prep_call(u_ns, cos_t, sin_t, TILES["prep"])
        kc = _compress_call(kc_in, cmp_pos, cmp_w1, cmp_w2, 2 * i)
        vc = _compress_call(vc_in, cmp_pos, cmp_w1, cmp_w2, 2 * i + 1)
        o_ns = _nsa_attn_call(q, kc, vc, ks, vs_t, kw, vw_t, gates_t, ov_t, sel_put, TILES["attn_q"], TILES["attn_k"])

        x = _outproj_call(o_rw, o_ns, x, mod3, row0, w_out_b, *ln1, i, alpha, TILES["out_m"])
        x = _mlp_call(x, mod3, row0, w1_b, w2_b, *ln2, i, alpha, TILES["mlp_m"], TILES["mlp_f"])
    return x
```

```python
import functools

import numpy as np
import jax
import jax.numpy as jnp
from jax import lax
from jax.experimental import pallas as pl
from jax.experimental.pallas import tpu as pltpu

F32 = jnp.float32
BF16 = jnp.bfloat16
HI = lax.Precision.HIGHEST

HEAD_DIM = 64
RWKV_HEADS = 16
NSA_HEADS = 16
NSA_KV_HEADS = 4
HEADS_PER_GROUP = NSA_HEADS // NSA_KV_HEADS
D_RWKV = RWKV_HEADS * HEAD_DIM
D_NSA = NSA_HEADS * HEAD_DIM
NSA_KV = NSA_KV_HEADS * HEAD_DIM
DECAY_LORA = 96
AAA_LORA = 96
MV_LORA = 64
GATE_LORA = 256
GN_EPS = 64e-5
CMP_BLOCK = 32
CMP_STRIDE = 16
SEL_BLOCK = 64
N_SELECT = 8
WINDOW = 512
ROPE_THETA = 10000.0
NEG = -1e30
FORCE = 1e4
LN_EPS = 1e-5
LOG2_E = 1.4426950408889634

LANES = 128
LORA_PAD = LANES
RW_XW = 3 * D_RWKV
RW_XA = RW_XW + LORA_PAD
RW_XG = RW_XA + LORA_PAD
RW_COLS = RW_XG + GATE_LORA
NS_KC = D_NSA
NS_GATE = D_NSA + 6 * NSA_KV
NS_COLS = NS_GATE + LANES
GATE_PAD = 16
VT_ROWS = HEAD_DIM + 16
PROJ_TN = 896

CHUNK = 64
V7X_VMEM_BYTES = 64 * 1024 * 1024
VMEM_LIMIT = V7X_VMEM_BYTES - 8 * 1024 * 1024
TILES = dict(proj_m=1024, prep=256, scan_rows=128, scan_heads=8, attn_q=256, attn_k=256, out_m=256, mlp_m=512,
             mlp_f=1024)


def _layer_spec(arr, layer):
    tail = (0,) * (arr.ndim - 1)
    return pl.BlockSpec((1,) + arr.shape[1:], lambda *_: (layer,) + tail)


def _cparams(sem):
    return pltpu.CompilerParams(dimension_semantics=sem, vmem_limit_bytes=VMEM_LIMIT)


def _mm(fn, a, b, mode):
    if mode is None:
        return fn(a, b, None)
    if mode == "hi":
        return fn(a, b, HI)
    ah, bh = a.astype(BF16), b.astype(BF16)
    if mode == "bf":
        return fn(ah, bh, None)
    if mode == "r3":
        r1 = b - bh.astype(F32)
        bm = r1.astype(BF16)
        bl = (r1 - bm.astype(F32)).astype(BF16)
        return fn(ah, bh, None) + (fn(ah, bm, None) + fn(ah, bl, None))
    al = (a - ah.astype(F32)).astype(BF16)
    bl = (b - bh.astype(F32)).astype(BF16)
    return fn(ah, bh, None) + (fn(ah, bl, None) + fn(al, bh, None))


def _dot(a, b, mode=None):
    return _mm(lambda p, q, pr: jnp.dot(p, q, preferred_element_type=F32, precision=pr), a, b, mode)


def _dot_nt(a, b, mode=None):
    return _mm(lambda p, q, pr: lax.dot_general(p, q, (((1,), (1,)), ((), ())), preferred_element_type=F32,
                                               precision=pr), a, b, mode)


def _bmm(a, b, mode=None):
    return _mm(lambda p, q, pr: jnp.einsum("bij,bjk->bik", p, q, preferred_element_type=F32, precision=pr),
               a, b, mode)


def _bmm_nt(a, b, mode=None):
    return _mm(lambda p, q, pr: jnp.einsum("bik,bjk->bij", p, q, preferred_element_type=F32, precision=pr),
               a, b, mode)


def _bmm_tn(a, b, mode=None):
    return _mm(lambda p, q, pr: jnp.einsum("bci,bcj->bij", p, q, preferred_element_type=F32, precision=pr),
               a, b, mode)


def _layer_norm_rows(z, g, b):
    mu = jnp.mean(z, axis=-1, keepdims=True)
    zc = z - mu
    var = jnp.mean(zc * zc, axis=-1, keepdims=True)
    return zc * lax.rsqrt(var + LN_EPS) * g + b


def _ada_kernel(c_ref, w_ref, b_ref, o_ref):
    c = c_ref[...]
    cond = c * jax.nn.sigmoid(c)
    o_ref[0] = _dot(cond.astype(BF16), w_ref[0].astype(BF16)) + b_ref[0]


def _ada_call(c_pad, w_ada, b_ada):
    L, D, N = w_ada.shape
    tn = 1024
    return pl.pallas_call(
        _ada_kernel,
        out_shape=jax.ShapeDtypeStruct((L, c_pad.shape[0], N), F32),
        grid=(L, N // tn),
        in_specs=[
            pl.BlockSpec(c_pad.shape, lambda l, n: (0, 0)),
            pl.BlockSpec((1, D, tn), lambda l, n: (l, 0, n)),
            pl.BlockSpec((1, 1, tn), lambda l, n: (l, 0, n)),
        ],
        out_specs=pl.BlockSpec((1, c_pad.shape[0], tn), lambda l, n: (l, 0, n)),
        compiler_params=_cparams(("parallel", "parallel")),
        name="adaln_mod",
    )(c_pad, w_ada, b_ada.reshape(L, 1, N))


def _modmm_kernel(x_ref, sc_ref, sh_ref, w_ref, o1_ref, o2_ref, h_scr, *, n1):
    n = pl.program_id(2)

    @pl.when(n == 0)
    def _():
        h = x_ref[0] * (1.0 + sc_ref[0]) + sh_ref[0]
        h_scr[...] = h.astype(BF16)

    y = _dot(h_scr[...], w_ref[0])

    @pl.when(n < n1)
    def _():
        o1_ref[0] = y

    @pl.when(n >= n1)
    def _():
        o2_ref[0] = y


def _modmm_call(x, mod3, row0, w_bf16, layer, n_first, tm, tn):
    B, T, D = x.shape
    N = w_bf16.shape[2]
    n1 = n_first // tn
    return pl.pallas_call(
        functools.partial(_modmm_kernel, n1=n1),
        out_shape=[jax.ShapeDtypeStruct((B, T, n_first), F32), jax.ShapeDtypeStruct((B, T, N - n_first), F32)],
        grid=(B, T // tm, N // tn),
        in_specs=[
            pl.BlockSpec((1, tm, D), lambda b, m, n: (b, m, 0)),
            pl.BlockSpec((1, 1, D), lambda b, m, n: (row0 + 6 * b + 1, 0, 0)),
            pl.BlockSpec((1, 1, D), lambda b, m, n: (row0 + 6 * b, 0, 0)),
            pl.BlockSpec((1, D, tn), lambda b, m, n: (layer, 0, n)),
        ],
        out_specs=[pl.BlockSpec((1, tm, tn), lambda b, m, n: (b, m, jnp.minimum(n, n1 - 1))),
                   pl.BlockSpec((1, tm, tn), lambda b, m, n: (b, m, jnp.maximum(n - n1, 0)))],
        scratch_shapes=[pltpu.VMEM((tm, D), BF16)],
        compiler_params=_cparams(("parallel", "parallel", "arbitrary")),
        name="inproj",
    )(x, mod3, mod3, w_bf16)


def _rwkv_prep_kernel(*refs, tt, first_layer):
    if first_layer:
        (u_ref, up_ref, mu_ref, w0_ref, w2_ref, a0_ref, a2_ref, g2_ref, kk_ref, ka_ref,
         r_o, ld_o, cs_o, k_o, v_o, kk_o, a_o, g_o, sh_scr) = refs
    else:
        (u_ref, up_ref, mu_ref, w0_ref, w2_ref, a0_ref, a2_ref, g2_ref, kk_ref, ka_ref,
         vf_ref, v0_ref, v1_ref, v2_ref,
         r_o, ld_o, cs_o, k_o, v_o, kk_o, a_o, g_o, sh_scr) = refs
    ti = pl.program_id(1)
    u = u_ref[0]
    prev = jnp.where(ti > 0, up_ref[0], 0.0)
    sh_scr[0:8, :] = prev
    sh_scr[8:8 + tt, :] = u
    us = sh_scr[7:7 + tt, :]
    x = u + (us - u) * mu_ref[0]
    r = x[:, 0:D_RWKV]
    k = x[:, D_RWKV:2 * D_RWKV]
    v = x[:, 2 * D_RWKV:3 * D_RWKV]
    xw = x[:, RW_XW:RW_XA]
    xa = x[:, RW_XA:RW_XG]
    xg = x[:, RW_XG:RW_COLS]
    w = -jax.nn.softplus(-(w0_ref[0] + _dot(jnp.tanh(xw).astype(BF16), w2_ref[0]))) - 0.5
    ld = -jnp.exp(w)
    ld_o[0] = ld
    row = lax.broadcasted_iota(jnp.int32, (tt, tt), 0)
    col = lax.broadcasted_iota(jnp.int32, (tt, tt), 1)
    chunk_ltri = jnp.where((row // CHUNK == col // CHUNK) & (col <= row), 1.0, 0.0)
    cs_o[0] = _dot(chunk_ltri, ld, "r3")
    a = jax.nn.sigmoid(a0_ref[0] + _dot(xa.astype(BF16), a2_ref[0]))
    g_o[0] = _dot(jax.nn.sigmoid(xg).astype(BF16), g2_ref[0]).astype(g_o.dtype)
    if not first_layer:
        lo = _dot(v.astype(BF16), v1_ref[0])
        gate = jax.nn.sigmoid(v0_ref[0] + _dot(lo.astype(BF16), v2_ref[0]))
        v = v + (vf_ref[0].astype(F32) - v) * gate
    r_o[0] = r.astype(r_o.dtype)
    v_o[0] = v.astype(v_o.dtype)
    a_o[0] = a.astype(a_o.dtype)
    kk_o[0] = (k * kk_ref[0]).astype(kk_o.dtype)
    k_o[0] = (k * (1.0 + (a - 1.0) * ka_ref[0])).astype(k_o.dtype)


def _rwkv_prep_call(u_rw, p, layer, v_first, tt):
    B, T, _ = u_rw.shape
    first_layer = v_first is None
    tile = pl.BlockSpec((1, tt, D_RWKV), lambda b, t: (b, t, 0))
    names = ["mu", "w0", "w2", "a0", "a2", "g2", "k_k", "k_a"]
    in_specs = [
        pl.BlockSpec((1, tt, RW_COLS), lambda b, t: (b, t, 0)),
        pl.BlockSpec((1, 8, RW_COLS), lambda b, t: (b, jnp.maximum(t * (tt // 8) - 1, 0), 0)),
    ] + [_layer_spec(p[nm], layer) for nm in names]
    args = [u_rw, u_rw] + [p[nm] for nm in names]
    if not first_layer:
        in_specs += [tile] + [_layer_spec(p[nm], layer - 1) for nm in ("v0", "v1", "v2")]
        args += [v_first, p["v0"], p["v1"], p["v2"]]
    out = lambda dt: jax.ShapeDtypeStruct((B, T, D_RWKV), dt)
    return pl.pallas_call(
        functools.partial(_rwkv_prep_kernel, tt=tt, first_layer=first_layer),
        out_shape=[out(BF16), out(F32), out(F32)] + [out(BF16)] * 5,
        grid=(B, T // tt),
        in_specs=in_specs,
        out_specs=[tile] * 8,
        scratch_shapes=[pltpu.VMEM((tt + 8, RW_COLS), F32)],
        compiler_params=_cparams(("parallel", "parallel")),
        name="rwkv_prep",
    )(*args)


def _rwkv_scan_kernel(r_ref, ld_ref, cs_ref, k_ref, v_ref, kk_ref, a_ref, g_ref, rk_ref, lg_ref, lb_ref,
                      w1_ref, w2_ref, o_ref, w1b_ref, w2b_ref, s_scr, wr_scr, o0_scr, pm_scr, qm_scr, bn_scr, g_scr,
                      *, ts, nt, hpb):
    C = CHUNK
    nc = ts // C
    nb = hpb * nc
    N = HEAD_DIM
    n = pl.program_id(0)

    w1b_ref[...] = w1_ref[0].astype(BF16)
    w2b_ref[...] = w2_ref[0].astype(BF16)

    @pl.when(n == 0)
    def _():
        for scr in (s_scr, wr_scr, o0_scr, pm_scr, qm_scr, bn_scr, g_scr):
            scr[...] = jnp.zeros_like(scr)

    first = lax.rem(jnp.maximum(n - 1, 0), nt) == 0
    outs = []
    for hh in range(hpb):
        S = jnp.where(first, 0.0, s_scr[hh])
        for c in range(nc):
            i = hh * nc + c
            Sb = S.astype(BF16)
            outs.append(_dot_nt(wr_scr[i], Sb) + o0_scr[i])
            S = _dot(Sb, pm_scr[i]) + qm_scr[i]
        s_scr[hh] = S
    o = jnp.stack(outs, axis=0)

    def split_row(ref):
        x = ref[0]
        return jnp.concatenate([jnp.broadcast_to(x[None, :, h * N:(h + 1) * N], (nc, 1, N)) for h in range(hpb)],
                               axis=0)

    mu = jnp.mean(o, axis=-1, keepdims=True)
    oc = o - mu
    var = jnp.mean(oc * oc, axis=-1, keepdims=True)
    on = oc * lax.rsqrt(var + GN_EPS) * split_row(lg_ref) + split_row(lb_ref)
    res = (on + bn_scr[...]) * g_scr[...]
    o_ref[0] = jnp.concatenate([res[h * nc:(h + 1) * nc].reshape(ts, N) for h in range(hpb)],
                               axis=-1).astype(o_ref.dtype)

    def split(ref):
        x = ref[0].astype(F32).reshape(nc, C, hpb * N)
        return jnp.concatenate([x[:, :, h * N:(h + 1) * N] for h in range(hpb)], axis=0)

    r, ld, cs, k, v, kkr, a, g = (split(z) for z in (r_ref, ld_ref, cs_ref, k_ref, v_ref, kk_ref, a_ref, g_ref))
    kkn = kkr / jnp.maximum(jnp.sqrt(jnp.sum(kkr * kkr, axis=-1, keepdims=True)), 1e-12)
    row = lax.broadcasted_iota(jnp.int32, (C, C), 0)
    col = lax.broadcasted_iota(jnp.int32, (C, C), 1)
    incl = col <= row
    strict = col < row
    eye = col == row
    cs_last = cs[:, C - 1:C, :]
    e_in = jnp.exp(cs)
    e_ex = jnp.exp(cs - ld)
    e_neg = jnp.exp(-cs)
    e_hat = jnp.exp(cs_last - cs)
    at = -kkn * e_ex
    b = kkn * a
    bt = b * e_neg
    kt = k * e_neg
    rt = r * e_in
    bh = (b * e_hat).astype(BF16)
    kh = (k * e_hat).astype(BF16)
    vb = v.astype(BF16)
    A = _bmm_nt(jnp.concatenate([at, rt], axis=1).astype(BF16), jnp.concatenate([bt, kt], axis=1).astype(BF16))
    a_ab = jnp.where(strict, A[:, :C, :C], 0.0)
    a_ak = jnp.where(strict, A[:, :C, C:], 0.0).astype(BF16)
    a_rb = jnp.where(incl, A[:, C:, :C], 0.0).astype(BF16)
    a_rk = jnp.where(incl, A[:, C:, C:], 0.0).astype(BF16)
    npow = a_ab.astype(BF16)
    tinv = jnp.where(eye, 1.0, 0.0).astype(F32) + a_ab
    p2 = 2
    while p2 < C:
        npow_f = _bmm(npow, npow)
        npow = npow_f.astype(BF16)
        tinv = tinv + _bmm(npow, tinv.astype(BF16))
        p2 *= 2
    akv = _bmm(a_ak, vb)
    x = _bmm(tinv.astype(BF16), jnp.concatenate([at, akv], axis=-1).astype(BF16))
    xb = x.astype(BF16)
    y = _bmm(a_rb, xb)
    pq = _bmm_tn(xb, bh)
    wr_scr[...] = (rt + y[:, :, :N]).astype(BF16)
    o0_scr[...] = y[:, :, N:] + _bmm(a_rk, vb)
    pm_scr[...] = (pq[:, :N, :] + jnp.where(eye, jnp.exp(cs_last), 0.0)).astype(BF16)
    qm_scr[...] = pq[:, N:, :] + _bmm_tn(vb, kh)
    bn_scr[...] = jnp.sum(r * k * split_row(rk_ref), axis=-1, keepdims=True) * v
    g_scr[...] = g


def _rwkv_scan_call(r, ld, cs, k, v, kk, a, g, p, w1, w2, layer, ts, hpb):
    B, T, _ = r.shape
    nt = T // ts
    hg = RWKV_HEADS // hpb
    nblk = B * hg * nt
    nb = hpb * (ts // CHUNK)
    w = hpb * HEAD_DIM

    def blk(n):
        return n // (hg * nt), lax.rem(n, nt), lax.rem(n // nt, hg)

    cur = lambda n: blk(jnp.minimum(n, nblk - 1))
    prev = lambda n: blk(jnp.maximum(n - 1, 0))
    tile = pl.BlockSpec((1, ts, w), cur)
    rowp = lambda f: pl.BlockSpec((1, 1, w), lambda n: (layer, 0, f(n)[2]))
    sq = lambda dt: pltpu.VMEM((nb, HEAD_DIM, HEAD_DIM), dt)
    _, d_in, d_ff = w1.shape
    rows1, rows2 = d_in // nblk, d_ff // nblk
    assert rows1 * nblk == d_in and rows2 * nblk == d_ff and rows1 % 16 == 0 and rows2 % 16 == 0
    slab = lambda n: jnp.minimum(n, nblk - 1)
    return pl.pallas_call(
        functools.partial(_rwkv_scan_kernel, ts=ts, nt=nt, hpb=hpb),
        out_shape=[jax.ShapeDtypeStruct((B, T, D_RWKV), BF16),
                   jax.ShapeDtypeStruct((d_in, d_ff), BF16), jax.ShapeDtypeStruct((d_ff, d_in), BF16)],
        grid=(nblk + 1,),
        in_specs=[tile] * 8 + [rowp(cur), rowp(prev), rowp(prev),
                               pl.BlockSpec((1, rows1, d_ff), lambda n: (layer, slab(n), 0)),
                               pl.BlockSpec((1, rows2, d_in), lambda n: (layer, slab(n), 0))],
        out_specs=[pl.BlockSpec((1, ts, w), prev), pl.BlockSpec((rows1, d_ff), lambda n: (slab(n), 0)),
                   pl.BlockSpec((rows2, d_in), lambda n: (slab(n), 0))],
        scratch_shapes=[pltpu.VMEM((hpb, HEAD_DIM, HEAD_DIM), F32), pltpu.VMEM((nb, CHUNK, HEAD_DIM), BF16),
                        pltpu.VMEM((nb, CHUNK, HEAD_DIM), F32), sq(BF16), sq(F32),
                        pltpu.VMEM((nb, CHUNK, HEAD_DIM), F32), pltpu.VMEM((nb, CHUNK, HEAD_DIM), F32)],
        compiler_params=_cparams(("arbitrary",)),
        name="rwkv_scan",
    )(r, ld, cs, k, v, kk, a, g, p["r_k"], p["lnx_g"], p["lnx_b"], w1, w2)


def _nsa_prep_kernel(u_ref, cos_ref, sin_ref, q_o, kc_o, vc_o, ks_o, vs_o, kw_o, vw_o, gate_o):
    cos = cos_ref[...]
    sin = sin_ref[...]
    lane = lax.broadcasted_iota(jnp.int32, cos.shape, 1)
    first_half = (lane % HEAD_DIM) < (HEAD_DIM // 2)

    def rope(x):
        other = jnp.where(first_half, pltpu.roll(x, LANES - HEAD_DIM // 2, 1), pltpu.roll(x, HEAD_DIM // 2, 1))
        return x * cos + other * sin

    def put(out_ref, col0, nheads, roped, scale=None):
        for j in range(nheads // 2):
            x = u_ref[0, :, col0 + j * LANES:col0 + (j + 1) * LANES]
            if roped:
                x = rope(x)
            if scale is not None:
                x = x * scale
            out_ref[0, 2 * j] = x[:, :HEAD_DIM].astype(out_ref.dtype)
            out_ref[0, 2 * j + 1] = x[:, HEAD_DIM:].astype(out_ref.dtype)

    low = lane < HEAD_DIM

    def put_wide(out_ref, col0, nheads, scale, tail):
        for j in range(nheads // 2):
            x = rope(u_ref[0, :, col0 + j * LANES:col0 + (j + 1) * LANES])
            if scale is not None:
                x = x * scale
            out_ref[0, 2 * j] = jnp.where(low, x, tail).astype(out_ref.dtype)
            out_ref[0, 2 * j + 1] = jnp.where(low, pltpu.roll(x, HEAD_DIM, 1), tail).astype(out_ref.dtype)

    def put_t(out_ref, col0, nheads):
        extra = out_ref.shape[2] - HEAD_DIM
        ones_row = jnp.where(lax.broadcasted_iota(jnp.int32, (extra, cos.shape[0]), 0) == 0, 1.0, 0.0)
        for j in range(nheads // 2):
            xt = u_ref[0, :, col0 + j * LANES:col0 + (j + 1) * LANES].T
            for i, part in enumerate((xt[:HEAD_DIM, :], xt[HEAD_DIM:, :])):
                out_ref[0, 2 * j + i, :HEAD_DIM, :] = part.astype(out_ref.dtype)
                out_ref[0, 2 * j + i, HEAD_DIM:, :] = ones_row.astype(out_ref.dtype)

    pos = pl.program_id(1) * cos.shape[0] + lax.broadcasted_iota(jnp.int32, cos.shape, 0)
    blk_onehot = jnp.where(lane - HEAD_DIM == pos // SEL_BLOCK, 1.0, 0.0)
    put_wide(q_o, 0, NSA_HEADS, HEAD_DIM ** -0.5 * LOG2_E, 0.0)
    put(kc_o, NS_KC, NSA_KV_HEADS, True)
    put(vc_o, NS_KC + NSA_KV, NSA_KV_HEADS, False)
    put_wide(ks_o, NS_KC + 2 * NSA_KV, NSA_KV_HEADS, None, blk_onehot)
    put_t(vs_o, NS_KC + 3 * NSA_KV, NSA_KV_HEADS)
    put_wide(kw_o, NS_KC + 4 * NSA_KV, NSA_KV_HEADS, None, 0.0)
    put_t(vw_o, NS_KC + 5 * NSA_KV, NSA_KV_HEADS)
    gates_t = jax.nn.sigmoid(u_ref[0, :, NS_GATE:NS_GATE + LANES]).T
    for gi in range(NSA_KV_HEADS):
        gate_o[0, gi] = gates_t[gi * GATE_PAD:(gi + 1) * GATE_PAD, :]


def _nsa_prep_call(u_ns, cos_t, sin_t, tt):
    B, T, _ = u_ns.shape
    G, H, N = NSA_KV_HEADS, NSA_HEADS, HEAD_DIM
    assert T // SEL_BLOCK <= LANES - N
    kv = jax.ShapeDtypeStruct((B, G, T, N), F32)
    kv_wide = jax.ShapeDtypeStruct((B, G, T, LANES), BF16)
    kv_t = jax.ShapeDtypeStruct((B, G, VT_ROWS, T), BF16)
    kv_spec = pl.BlockSpec((1, G, tt, N), lambda b, t: (b, 0, t, 0))
    kvw_spec = pl.BlockSpec((1, G, tt, LANES), lambda b, t: (b, 0, t, 0))
    kvt_spec = pl.BlockSpec((1, G, VT_ROWS, tt), lambda b, t: (b, 0, 0, t))
    return pl.pallas_call(
        _nsa_prep_kernel,
        out_shape=[jax.ShapeDtypeStruct((B, H, T, LANES), BF16), kv, kv, kv_wide, kv_t,
                   kv_wide, kv_t, jax.ShapeDtypeStruct((B, G, GATE_PAD, T), F32)],
        grid=(B, T // tt),
        in_specs=[
            pl.BlockSpec((1, tt, NS_COLS), lambda b, t: (b, t, 0)),
            pl.BlockSpec((tt, LANES), lambda b, t: (t, 0)),
            pl.BlockSpec((tt, LANES), lambda b, t: (t, 0)),
        ],
        out_specs=[pl.BlockSpec((1, H, tt, LANES), lambda b, t: (b, 0, t, 0)), kv_spec, kv_spec, kvw_spec, kvt_spec,
                   kvw_spec, kvt_spec, pl.BlockSpec((1, G, GATE_PAD, tt), lambda b, t: (b, 0, 0, t))],
        compiler_params=_cparams(("parallel", "parallel")),
        name="nsa_prep",
    )(u_ns, cos_t, sin_t)


def _compress_kernel(x_ref, pos_ref, w1_ref, w2_ref, o_ref):
    x = x_ref[0, 0]
    half = x.shape[1]
    y_top = _dot(x + pos_ref[0, 0:1, :], w1_ref[0, :half, :], "x3")
    y_bot = _dot(x + pos_ref[0, 1:2, :], w1_ref[0, half:, :], "x3")
    pre = y_top + pltpu.roll(y_bot, x.shape[0] - 1, 0)
    o_ref[0, 0] = _dot(jax.nn.gelu(pre), w2_ref[0], "x3")


def _compress_call(kv, pos, w1, w2, which):
    B, G, T, N = kv.shape
    n_half = T // CMP_STRIDE
    width = CMP_STRIDE * N
    hidden = w1.shape[-1]
    x = kv.reshape(B, G, n_half, width)
    return pl.pallas_call(
        _compress_kernel,
        out_shape=jax.ShapeDtypeStruct((B, G, n_half, w2.shape[-1]), F32),
        grid=(B, G),
        in_specs=[
            pl.BlockSpec((1, 1, n_half, width), lambda b, g: (b, g, 0, 0)),
            pl.BlockSpec((1, 2, width), lambda b, g: (which, 0, 0)),
            pl.BlockSpec((1, 2 * width, hidden), lambda b, g: (which, 0, 0)),
            pl.BlockSpec((1, hidden, w2.shape[-1]), lambda b, g: (which, 0, 0)),
        ],
        out_specs=pl.BlockSpec((1, 1, n_half, w2.shape[-1]), lambda b, g: (b, g, 0, 0)),
        compiler_params=_cparams(("parallel", "parallel")),
        name="nsa_compress",
    )(x, pos, w1, w2)


def _nsa_attn_kernel(q_ref, kc_ref, vc_ref, ks_ref, vst_ref, kw_ref, vwt_ref, gate_ref, ov_ref, put_ref,
                     o_ref, s0_scr, s1_scr, *, tq, tk):
    HPG, N = HEADS_PER_GROUP, HEAD_DIM
    qi = pl.program_id(2)
    q0 = qi * tq
    k_len = ks_ref.shape[2]

    wq = HPG * tq
    q_win = q_ref[0].reshape(wq, LANES)
    rel = lax.broadcasted_iota(jnp.int32, (tk, tq), 1) - lax.broadcasted_iota(jnp.int32, (tk, tq), 0)
    init = (jnp.full((1, wq), NEG, F32), jnp.zeros((VT_ROWS, wq), F32))
    n_tiles = k_len // tk

    def scores(q_all, k_ref, k0, dst):
        dst[...] = _dot_nt(k_ref[0, 0, pl.ds(k0, tk), :], q_all)

    def update(carry, src, vt_ref, k0, bias):
        m_i, acc = carry
        s = src[...]
        if bias is not None:
            s = s + (bias if bias.ndim == 0 else jnp.concatenate([bias] * HPG, axis=1))
        m_n = jnp.maximum(m_i, jnp.max(s, axis=0, keepdims=True))
        p = jnp.exp2(s - m_n)
        return m_n, jnp.exp2(m_i - m_n) * acc + _dot(vt_ref[0, 0, :, pl.ds(k0, tk)], p.astype(BF16))

    def split(o):
        return [o[:, h * tq:(h + 1) * tq] for h in range(HPG)]

    def finish(carry):
        return split(carry[1][:N] / carry[1][N:N + 1])

    def window_tile(e):
        koff = e * tk - WINDOW
        lo, hi = -(tk - 1) - koff, (tq - 1) - koff
        bias = None
        if lo < 0 or hi >= WINDOW:
            dist = rel - koff
            bias = jnp.where((dist >= 0) & (dist < WINDOW), 0.0, NEG)
        if koff < 0:
            bias = jnp.where(q0 + koff >= 0, 0.0 if bias is None else bias, NEG)
        return pl.multiple_of(jnp.maximum(q0 + koff, 0), tk), bias

    win = [window_tile(e) for e in range((WINDOW + tq) // tk)]
    assert len(win) == 3
    bufs = (s0_scr, s1_scr)
    scores(q_win, kw_ref, win[0][0], bufs[0])

    n_cmp = kc_ref.shape[2]
    c_c = lax.broadcasted_iota(jnp.int32, (n_cmp, tq), 0)
    t_c = q0 + lax.broadcasted_iota(jnp.int32, (n_cmp, tq), 1)
    cmask = jnp.concatenate([c_c * CMP_STRIDE + (CMP_BLOCK - 1) <= t_c] * HPG, axis=1)
    s = jnp.where(cmask, _dot_nt(kc_ref[0, 0].astype(BF16), q_win), NEG)
    scores(q_win, kw_ref, win[1][0], bufs[1])
    m = jnp.max(s, axis=0, keepdims=True)
    e = jnp.where(cmask, jnp.exp2(s - m), 0.0)
    l = jnp.sum(e, axis=0, keepdims=True)
    p = e / jnp.maximum(l, 1e-30)
    o_cmp = split(_dot(vc_ref[0, 0].T[:N].astype(BF16), p.astype(BF16)))
    psum = p[:, :tq]
    for h in range(1, HPG):
        psum = psum + p[:, h * tq:(h + 1) * tq]

    n_sel = ov_ref.shape[0]
    imp = _dot(ov_ref[...], psum, "r3")
    carry = update(init, bufs[0], vwt_ref, *win[0])
    scores(q_win, kw_ref, win[2][0], bufs[0])
    blk = lax.broadcasted_iota(jnp.int32, (n_sel, tq), 0)
    t_s = q0 + lax.broadcasted_iota(jnp.int32, (n_sel, tq), 1)
    cur = t_s // SEL_BLOCK
    forced = (blk == 0) | (blk == cur) | (blk == cur - 1)
    val = jnp.where(forced, FORCE, jnp.where(blk * SEL_BLOCK <= t_s, imp, -1.0))
    rank = jnp.zeros((n_sel, tq), F32)
    for i in range(n_sel):
        vi = val[i:i + 1, :]
        ahead = (vi > val) | ((vi == val) & (blk > i))
        rank = rank + jnp.where(ahead, 1.0, 0.0)
    unsel_t = jnp.where(rank < float(min(N_SELECT, n_sel)), 0.0, NEG)
    q_bias = lax.dot_general(unsel_t.astype(BF16), put_ref[...], (((0,), (0,)), ((), ())),
                             preferred_element_type=F32)
    carry = update(carry, bufs[1], vwt_ref, *win[1])
    q_sel = jnp.concatenate([(q_ref[0, h].astype(F32) + q_bias).astype(BF16) for h in range(HPG)], axis=0)
    o_win = finish(update(carry, bufs[0], vwt_ref, *win[2]))

    def key0(j):
        return pl.multiple_of(j * tk, tk)

    scores(q_sel, ks_ref, key0(0), s0_scr)

    def pair(jj, carry):
        scores(q_sel, ks_ref, key0(2 * jj + 1), s1_scr)
        carry = update(carry, s0_scr, vst_ref, key0(2 * jj), None)
        scores(q_sel, ks_ref, key0(2 * jj + 2), s0_scr)
        return update(carry, s1_scr, vst_ref, key0(2 * jj + 1), None)

    carry = lax.fori_loop(0, qi // 2, pair, init)
    t1 = 2 * (qi // 2)
    t2 = jnp.minimum(t1 + 1, n_tiles - 1)

    def causal(t):
        return jnp.where(rel >= (t - qi) * tk, 0.0, NEG)

    scores(q_sel, ks_ref, key0(t2), s1_scr)
    carry = update(carry, s0_scr, vst_ref, key0(t1), causal(t1))
    o_slc = finish(update(carry, s1_scr, vst_ref, key0(t2), causal(t1 + 1)))

    gate = gate_ref[0, 0]
    outs = [gate[3 * h:3 * h + 1, :] * o_cmp[h] + gate[3 * h + 1:3 * h + 2, :] * o_slc[h]
            + gate[3 * h + 2:3 * h + 3, :] * o_win[h] for h in range(HPG)]
    o_ref[0] = jnp.concatenate(outs, axis=0).T.astype(o_ref.dtype)


def _nsa_attn_call(q, kc, vc, ks, vs_t, kw, vw_t, gates_t, ov_t, put, tq, tk):
    B, H, T, _ = q.shape
    N = HEAD_DIM
    G, HPG = NSA_KV_HEADS, HEADS_PER_GROUP
    assert WINDOW % tk == 0 and tq == tk
    n_half = kc.shape[2]
    cmp_spec = pl.BlockSpec((1, 1, n_half, LANES), lambda b, g, t: (b, g, 0, 0))
    kv_spec = pl.BlockSpec((1, 1, T, LANES), lambda b, g, t: (b, g, 0, 0))
    kvt_spec = pl.BlockSpec((1, 1, VT_ROWS, T), lambda b, g, t: (b, g, 0, 0))
    return pl.pallas_call(
        functools.partial(_nsa_attn_kernel, tq=tq, tk=tk),
        out_shape=jax.ShapeDtypeStruct((B, T, D_NSA), BF16),
        grid=(B, G, T // tq),
        in_specs=[
            pl.BlockSpec((1, HPG, tq, LANES), lambda b, g, t: (b, g, t, 0)),
            cmp_spec, cmp_spec, kv_spec, kvt_spec, kv_spec, kvt_spec,
            pl.BlockSpec((1, 1, GATE_PAD, tq), lambda b, g, t: (b, g, 0, t)),
            pl.BlockSpec(ov_t.shape, lambda b, g, t: (0, 0)),
            pl.BlockSpec(put.shape, lambda b, g, t: (0, 0)),
        ],
        out_specs=pl.BlockSpec((1, tq, HPG * N), lambda b, g, t: (b, t, g)),
        scratch_shapes=[pltpu.VMEM((tk, HPG * tq), F32), pltpu.VMEM((tk, HPG * tq), F32)],
        compiler_params=_cparams(("parallel", "parallel", "arbitrary")),
        name="nsa_attn",
    )(q, kc, vc, ks, vs_t, kw, vw_t, gates_t, ov_t, put)


def _outproj_kernel(orw_ref, ons_ref, x_ref, gt_ref, w_ref, g_ref, b_ref, o_ref, *, alpha):
    half = orw_ref.shape[-1]
    y = _dot(orw_ref[0].astype(BF16), w_ref[0, :half, :]) + _dot(ons_ref[0].astype(BF16), w_ref[0, half:, :])
    z = alpha * x_ref[0] + (1.0 + gt_ref[0]) * y
    o_ref[0] = _layer_norm_rows(z, g_ref[0], b_ref[0])


def _outproj_call(o_rw, o_ns, x, mod3, row0, w_bf16, ln_g, ln_b, layer, alpha, tm):
    B, T, D = x.shape
    half = o_rw.shape[-1]
    return pl.pallas_call(
        functools.partial(_outproj_kernel, alpha=alpha),
        out_shape=jax.ShapeDtypeStruct((B, T, D), F32),
        grid=(B, T // tm),
        in_specs=[
            pl.BlockSpec((1, tm, half), lambda b, m: (b, m, 0)),
            pl.BlockSpec((1, tm, half), lambda b, m: (b, m, 0)),
            pl.BlockSpec((1, tm, D), lambda b, m: (b, m, 0)),
            pl.BlockSpec((1, 1, D), lambda b, m: (row0 + 6 * b + 2, 0, 0)),
            _layer_spec(w_bf16, layer), _layer_spec(ln_g, layer), _layer_spec(ln_b, layer),
        ],
        out_specs=pl.BlockSpec((1, tm, D), lambda b, m: (b, m, 0)),
        compiler_params=_cparams(("parallel", "parallel")),
        name="outproj_ln",
    )(o_rw, o_ns, x, mod3, w_bf16, ln_g, ln_b)


def _mlp_kernel(x_ref, sc_ref, sh_ref, gt_ref, w1_ref, w2_ref, g_ref, b_ref, o_ref, h_scr, acc_scr, *, alpha):
    f = pl.program_id(2)

    @pl.when(f == 0)
    def _():
        h = x_ref[0] * (1.0 + sc_ref[0]) + sh_ref[0]
        h_scr[...] = h.astype(BF16)
        acc_scr[...] = jnp.zeros_like(acc_scr)

    a = jnp.maximum(_dot(h_scr[...], w1_ref[...]), 0.0)
    acc_scr[...] += _dot((a * a).astype(BF16), w2_ref[...])

    @pl.when(f == pl.num_programs(2) - 1)
    def _():
        z = alpha * x_ref[0] + (1.0 + gt_ref[0]) * acc_scr[...]
        o_ref[0] = _layer_norm_rows(z, g_ref[0], b_ref[0])


def _mlp_call(x, mod3, row0, w1_bf16, w2_bf16, ln_g, ln_b, layer, alpha, tm, tf):
    B, T, D = x.shape
    FF = w1_bf16.shape[1]
    modspec = lambda j: pl.BlockSpec((1, 1, D), lambda b, m, f: (row0 + 6 * b + j, 0, 0))
    return pl.pallas_call(
        functools.partial(_mlp_kernel, alpha=alpha),
        out_shape=jax.ShapeDtypeStruct((B, T, D), F32),
        grid=(B, T // tm, FF // tf),
        in_specs=[
            pl.BlockSpec((1, tm, D), lambda b, m, f: (b, m, 0)),
            modspec(4), modspec(3), modspec(5),
            pl.BlockSpec((D, tf), lambda b, m, f: (0, f)),
            pl.BlockSpec((tf, D), lambda b, m, f: (f, 0)),
            _layer_spec(ln_g, layer), _layer_spec(ln_b, layer),
        ],
        out_specs=pl.BlockSpec((1, tm, D), lambda b, m, f: (b, m, 0)),
        scratch_shapes=[pltpu.VMEM((tm, D), BF16), pltpu.VMEM((tm, D), F32)],
        compiler_params=_cparams(("parallel", "parallel", "arbitrary")),
        name="mlp_ln",
    )(x, mod3, mod3, mod3, w1_bf16, w2_bf16, ln_g, ln_b)


def _pad_last(w, n):
    return jnp.pad(w, [(0, 0)] * (w.ndim - 1) + [(0, n - w.shape[-1])])


def _pad_rows(w, n):
    return jnp.pad(w, [(0, 0)] * (w.ndim - 2) + [(0, n - w.shape[-2]), (0, 0)])


def _regroup_kernel(w_ref, put_ref, o_ref):
    c = np.cumsum([0, 3 * D_RWKV, DECAY_LORA, AAA_LORA, GATE_LORA, NS_GATE]).tolist()
    rows = w_ref.shape[1]
    zeros = lambda n: jnp.zeros((rows, n), BF16)
    o_ref[0, :, 0:RW_XW] = w_ref[0, :, c[0]:c[1]].astype(BF16)
    o_ref[0, :, RW_XW:RW_XA] = jnp.concatenate([w_ref[0, :, c[1]:c[2]].astype(BF16), zeros(LORA_PAD - DECAY_LORA)], 1)
    o_ref[0, :, RW_XA:RW_XG] = jnp.concatenate([w_ref[0, :, c[2]:c[3]].astype(BF16), zeros(LORA_PAD - AAA_LORA)], 1)
    o_ref[0, :, RW_XG:RW_COLS] = w_ref[0, :, c[3]:c[4]].astype(BF16)
    o_ref[0, :, RW_COLS:RW_COLS + NS_GATE] = w_ref[0, :, c[4]:c[5]].astype(BF16)
    gates = w_ref[0, :, c[5]:].astype(BF16)
    o_ref[0, :, RW_COLS + NS_GATE:] = _dot(gates, put_ref[...]).astype(BF16)


def _split_w_in(w_in):
    L, D, n_in = w_in.shape
    ng = 3 * HEADS_PER_GROUP
    n_gate = NSA_KV_HEADS * ng
    src = np.arange(n_gate)
    put = np.zeros((n_gate, LANES), np.float32)
    put[src, (src // ng) * GATE_PAD + src % ng] = 1.0
    tr = 256
    return pl.pallas_call(
        _regroup_kernel,
        out_shape=jax.ShapeDtypeStruct((L, D, RW_COLS + NS_COLS), BF16),
        grid=(L, D // tr),
        in_specs=[pl.BlockSpec((1, tr, n_in), lambda l, t: (l, t, 0)),
                  pl.BlockSpec(put.shape, lambda l, t: (0, 0))],
        out_specs=pl.BlockSpec((1, tr, RW_COLS + NS_COLS), lambda l, t: (l, t, 0)),
        compiler_params=_cparams(("parallel", "parallel")),
        name="w_in_regroup",
    )(w_in, jnp.asarray(put, BF16))


def _pad_mu(mu):
    c = np.cumsum([0, 3 * D_RWKV, DECAY_LORA, AAA_LORA, GATE_LORA]).tolist()
    parts = [mu[:, c[0]:c[1]], _pad_last(mu[:, c[1]:c[2]], LORA_PAD), _pad_last(mu[:, c[2]:c[3]], LORA_PAD),
             mu[:, c[3]:c[4]]]
    return jnp.concatenate(parts, axis=-1)[:, None, :]


def _rope_tables(T):
    half = HEAD_DIM // 2
    inv = ROPE_THETA ** (-jnp.arange(half, dtype=F32) / half)
    ang = jnp.arange(T, dtype=F32)[:, None] * inv[None]
    cos, sin = jnp.cos(ang), jnp.sin(ang)
    cos_t = jnp.tile(cos, (1, LANES // half))
    sin_t = jnp.tile(jnp.concatenate([-sin, sin], axis=1), (1, LANES // HEAD_DIM))
    return cos_t, sin_t


def _selection_constants(T):
    n_half = T // CMP_STRIDE
    n_cmp = (T - CMP_BLOCK) // CMP_STRIDE + 1
    n_sel = T // SEL_BLOCK
    pos = np.arange(n_cmp)[:, None] * CMP_STRIDE + np.arange(CMP_BLOCK)[None]
    ov = ((pos // SEL_BLOCK)[..., None] == np.arange(n_sel)).sum(1) / CMP_BLOCK
    ov_t = np.zeros((n_sel, n_half), np.float32)
    ov_t[:, :n_cmp] = ov.T
    put = (np.arange(LANES)[None, :] == HEAD_DIM + np.arange(n_sel)[:, None]).astype(np.float32)
    return jnp.asarray(ov_t), jnp.asarray(put, BF16)


def kernel(x, c, w_ada, b_ada, w_in, rwkv_mu, rwkv_w0, rwkv_w2, rwkv_a0, rwkv_a2, rwkv_g2, rwkv_k_k, rwkv_k_a, rwkv_r_k, rwkv_lnx_g, rwkv_lnx_b, rwkv_v0, rwkv_v1, rwkv_v2, nsa_cmp_pos, nsa_cmp_w1, nsa_cmp_w2, w_out, ln1_g, ln1_b, mlp_w1, mlp_w2, ln2_g, ln2_b):
    B, T, D = x.shape
    L = w_ada.shape[0]
    alpha = (2 * L) ** 0.25

    c_pad = jnp.pad(c, ((0, -B % 8), (0, 0)))
    mod = _ada_call(c_pad, w_ada, b_ada)[:, :B]
    mod3 = mod.reshape(L * B * 6, 1, D)
    cos_t, sin_t = _rope_tables(T)
    ov_t, sel_put = _selection_constants(T)

    rows = lambda z: z.reshape(z.shape[0], 1, -1)
    w_in_b = _split_w_in(w_in)
    p = {
        "mu": _pad_mu(rwkv_mu), "w0": rows(rwkv_w0), "a0": rows(rwkv_a0),
        "w2": _pad_rows(rwkv_w2, LORA_PAD).astype(BF16), "a2": _pad_rows(rwkv_a2, LORA_PAD).astype(BF16),
        "g2": rwkv_g2.astype(BF16), "k_k": rows(rwkv_k_k), "k_a": rows(rwkv_k_a),
        "r_k": rows(rwkv_r_k), "lnx_g": rows(rwkv_lnx_g), "lnx_b": rows(rwkv_lnx_b),
        "v0": rows(rwkv_v0), "v1": _pad_last(rwkv_v1, LORA_PAD).astype(BF16),
        "v2": _pad_rows(rwkv_v2, LORA_PAD).astype(BF16),
    }
    half_block = CMP_STRIDE * HEAD_DIM
    cmp_pos = nsa_cmp_pos.reshape(2 * L, 2, half_block)
    cmp_w1 = nsa_cmp_w1.reshape((2 * L,) + nsa_cmp_w1.shape[2:])
    cmp_w2 = _pad_last(nsa_cmp_w2.reshape((2 * L,) + nsa_cmp_w2.shape[2:]), LANES)
    w_out_b = w_out.astype(BF16)
    ln1 = (rows(ln1_g), rows(ln1_b))
    ln2 = (rows(ln2_g), rows(ln2_b))

    v_first = None
    for i in range(L):
        row0 = i * B * 6
        u_rw, u_ns = _modmm_call(x, mod3, row0, w_in_b, i, RW_COLS, TILES["proj_m"], PROJ_TN)

        r, ld, cs, k, v, kk, a, g = _rwkv_prep_call(u_rw, p, i, v_first, TILES["prep"])
        if i == 0:
            v_first = v
        o_rw, w1_b, w2_b = _rwkv_scan_call(r, ld, cs, k, v, kk, a, g, p, mlp_w1, mlp_w2, i, TILES["scan_rows"],
                                           TILES["scan_heads"])

        q, kc_in, vc_in, ks, vs_t, kw, vw_t, gates_t = _nsa_prep_call(u_ns, cos_t, sin_t, TILES["prep"])
        kc = _compress_call(kc_in, cmp_pos, cmp_w1, cmp_w2, 2 * i)
        vc = _compress_call(vc_in, cmp_pos, cmp_w1, cmp_w2, 2 * i + 1)
        o_ns = _nsa_attn_call(q, kc, vc, ks, vs_t, kw, vw_t, gates_t, ov_t, sel_put, TILES["attn_q"], TILES["attn_k"])

        x = _outproj_call(o_rw, o_ns, x, mod3, row0, w_out_b, *ln1, i, alpha, TILES["out_m"])
        x = _mlp_call(x, mod3, row0, w1_b, w2_b, *ln2, i, alpha, TILES["mlp_m"], TILES["mlp_f"])
    return x
```

```python
import functools

import numpy as np
import jax
import jax.numpy as jnp
from jax import lax
from jax.experimental import pallas as pl
from jax.experimental.pallas import tpu as pltpu

F32 = jnp.float32
BF16 = jnp.bfloat16
HI = lax.Precision.HIGHEST

HEAD_DIM = 64
RWKV_HEADS = 16
NSA_HEADS = 16
NSA_KV_HEADS = 4
HEADS_PER_GROUP = NSA_HEADS // NSA_KV_HEADS
D_RWKV = RWKV_HEADS * HEAD_DIM
D_NSA = NSA_HEADS * HEAD_DIM
NSA_KV = NSA_KV_HEADS * HEAD_DIM
DECAY_LORA = 96
AAA_LORA = 96
MV_LORA = 64
GATE_LORA = 256
GN_EPS = 64e-5
CMP_BLOCK = 32
CMP_STRIDE = 16
SEL_BLOCK = 64
N_SELECT = 8
WINDOW = 512
ROPE_THETA = 10000.0
NEG = -1e30
FORCE = 1e4
LN_EPS = 1e-5
LOG2_E = 1.4426950408889634

LANES = 128
LORA_PAD = LANES
RW_XW = 3 * D_RWKV
RW_XA = RW_XW + LORA_PAD
RW_XG = RW_XA + LORA_PAD
RW_COLS = RW_XG + GATE_LORA
NS_KC = D_NSA
NS_GATE = D_NSA + 6 * NSA_KV
NS_COLS = NS_GATE + LANES
GATE_PAD = 16
VT_ROWS = HEAD_DIM + 16
PROJ_TN = 896

CHUNK = 64
V7X_VMEM_BYTES = 64 * 1024 * 1024
VMEM_LIMIT = V7X_VMEM_BYTES - 8 * 1024 * 1024
TILES = dict(proj_m=1024, prep=256, scan_rows=128, scan_heads=8, attn_q=256, attn_k=256, out_m=512, mlp_m=512,
             mlp_f=1024)


def _layer_spec(arr, layer):
    tail = (0,) * (arr.ndim - 1)
    return pl.BlockSpec((1,) + arr.shape[1:], lambda *_: (layer,) + tail)


def _cparams(sem):
    return pltpu.CompilerParams(dimension_semantics=sem, vmem_limit_bytes=VMEM_LIMIT)


def _mm(fn, a, b, mode):
    if mode is None:
        return fn(a, b, None)
    if mode == "hi":
        return fn(a, b, HI)
    ah, bh = a.astype(BF16), b.astype(BF16)
    if mode == "bf":
        return fn(ah, bh, None)
    if mode == "r3":
        r1 = b - bh.astype(F32)
        bm = r1.astype(BF16)
        bl = (r1 - bm.astype(F32)).astype(BF16)
        return fn(ah, bh, None) + (fn(ah, bm, None) + fn(ah, bl, None))
    al = (a - ah.astype(F32)).astype(BF16)
    bl = (b - bh.astype(F32)).astype(BF16)
    return fn(ah, bh, None) + (fn(ah, bl, None) + fn(al, bh, None))


def _dot(a, b, mode=None):
    return _mm(lambda p, q, pr: jnp.dot(p, q, preferred_element_type=F32, precision=pr), a, b, mode)


def _dot_nt(a, b, mode=None):
    return _mm(lambda p, q, pr: lax.dot_general(p, q, (((1,), (1,)), ((), ())), preferred_element_type=F32,
                                               precision=pr), a, b, mode)


def _bmm(a, b, mode=None):
    return _mm(lambda p, q, pr: jnp.einsum("bij,bjk->bik", p, q, preferred_element_type=F32, precision=pr),
               a, b, mode)


def _bmm_nt(a, b, mode=None):
    return _mm(lambda p, q, pr: jnp.einsum("bik,bjk->bij", p, q, preferred_element_type=F32, precision=pr),
               a, b, mode)


def _bmm_tn(a, b, mode=None):
    return _mm(lambda p, q, pr: jnp.einsum("bci,bcj->bij", p, q, preferred_element_type=F32, precision=pr),
               a, b, mode)


def _layer_norm_rows(z, g, b):
    mu = jnp.mean(z, axis=-1, keepdims=True)
    zc = z - mu
    var = jnp.mean(zc * zc, axis=-1, keepdims=True)
    return zc * lax.rsqrt(var + LN_EPS) * g + b


def _ada_kernel(c_ref, w_ref, b_ref, o_ref):
    c = c_ref[...]
    cond = c * jax.nn.sigmoid(c)
    o_ref[0] = _dot(cond.astype(BF16), w_ref[0].astype(BF16)) + b_ref[0]


def _ada_call(c_pad, w_ada, b_ada):
    L, D, N = w_ada.shape
    tn = 1024
    return pl.pallas_call(
        _ada_kernel,
        out_shape=jax.ShapeDtypeStruct((L, c_pad.shape[0], N), F32),
        grid=(L, N // tn),
        in_specs=[
            pl.BlockSpec(c_pad.shape, lambda l, n: (0, 0)),
            pl.BlockSpec((1, D, tn), lambda l, n: (l, 0, n)),
            pl.BlockSpec((1, 1, tn), lambda l, n: (l, 0, n)),
        ],
        out_specs=pl.BlockSpec((1, c_pad.shape[0], tn), lambda l, n: (l, 0, n)),
        compiler_params=_cparams(("parallel", "parallel")),
        name="adaln_mod",
    )(c_pad, w_ada, b_ada.reshape(L, 1, N))


def _modmm_kernel(x_ref, sc_ref, sh_ref, w_ref, o1_ref, o2_ref, h_scr, *, n1):
    n = pl.program_id(2)

    @pl.when(n == 0)
    def _():
        h = x_ref[0] * (1.0 + sc_ref[0]) + sh_ref[0]
        h_scr[...] = h.astype(BF16)

    y = _dot(h_scr[...], w_ref[0]).astype(o1_ref.dtype)

    @pl.when(n < n1)
    def _():
        o1_ref[0] = y

    @pl.when(n >= n1)
    def _():
        o2_ref[0] = y


def _modmm_call(x, mod3, row0, w_bf16, layer, n_first, tm, tn):
    B, T, D = x.shape
    N = w_bf16.shape[2]
    n1 = n_first // tn
    return pl.pallas_call(
        functools.partial(_modmm_kernel, n1=n1),
        out_shape=[jax.ShapeDtypeStruct((B, T, n_first), BF16), jax.ShapeDtypeStruct((B, T, N - n_first), BF16)],
        grid=(B, T // tm, N // tn),
        in_specs=[
            pl.BlockSpec((1, tm, D), lambda b, m, n: (b, m, 0)),
            pl.BlockSpec((1, 1, D), lambda b, m, n: (row0 + 6 * b + 1, 0, 0)),
            pl.BlockSpec((1, 1, D), lambda b, m, n: (row0 + 6 * b, 0, 0)),
            pl.BlockSpec((1, D, tn), lambda b, m, n: (layer, 0, n)),
        ],
        out_specs=[pl.BlockSpec((1, tm, tn), lambda b, m, n: (b, m, jnp.minimum(n, n1 - 1))),
                   pl.BlockSpec((1, tm, tn), lambda b, m, n: (b, m, jnp.maximum(n - n1, 0)))],
        scratch_shapes=[pltpu.VMEM((tm, D), BF16)],
        compiler_params=_cparams(("parallel", "parallel", "arbitrary")),
        name="inproj",
    )(x, mod3, mod3, w_bf16)


def _rwkv_prep_kernel(*refs, tt, first_layer):
    if first_layer:
        (u_ref, up_ref, mu_ref, w0_ref, w2_ref, a0_ref, a2_ref, g2_ref, kk_ref, ka_ref,
         r_o, ld_o, cs_o, k_o, v_o, kk_o, a_o, g_o, sh_scr) = refs
    else:
        (u_ref, up_ref, mu_ref, w0_ref, w2_ref, a0_ref, a2_ref, g2_ref, kk_ref, ka_ref,
         vf_ref, v0_ref, v1_ref, v2_ref,
         r_o, ld_o, cs_o, k_o, v_o, kk_o, a_o, g_o, sh_scr) = refs
    ti = pl.program_id(1)
    u = u_ref[0].astype(F32)
    prev = jnp.where(ti > 0, up_ref[0, 8:16, :].astype(F32), 0.0)
    sh_scr[0:8, :] = prev
    sh_scr[8:8 + tt, :] = u
    us = sh_scr[7:7 + tt, :]
    x = u + (us - u) * mu_ref[0]
    r = x[:, 0:D_RWKV]
    k = x[:, D_RWKV:2 * D_RWKV]
    v = x[:, 2 * D_RWKV:3 * D_RWKV]
    xw = x[:, RW_XW:RW_XA]
    xa = x[:, RW_XA:RW_XG]
    xg = x[:, RW_XG:RW_COLS]
    w = -jax.nn.softplus(-(w0_ref[0] + _dot(jnp.tanh(xw).astype(BF16), w2_ref[0]))) - 0.5
    ld = -jnp.exp(w)
    ld_o[0] = ld
    row = lax.broadcasted_iota(jnp.int32, (tt, tt), 0)
    col = lax.broadcasted_iota(jnp.int32, (tt, tt), 1)
    chunk_ltri = jnp.where((row // CHUNK == col // CHUNK) & (col <= row), 1.0, 0.0)
    cs_o[0] = _dot(chunk_ltri, ld, "r3")
    a = jax.nn.sigmoid(a0_ref[0] + _dot(xa.astype(BF16), a2_ref[0]))
    g_o[0] = _dot(jax.nn.sigmoid(xg).astype(BF16), g2_ref[0]).astype(g_o.dtype)
    if not first_layer:
        lo = _dot(v.astype(BF16), v1_ref[0])
        gate = jax.nn.sigmoid(v0_ref[0] + _dot(lo.astype(BF16), v2_ref[0]))
        v = v + (vf_ref[0].astype(F32) - v) * gate
    r_o[0] = r.astype(r_o.dtype)
    v_o[0] = v.astype(v_o.dtype)
    a_o[0] = a.astype(a_o.dtype)
    kk_o[0] = (k * kk_ref[0]).astype(kk_o.dtype)
    k_o[0] = (k * (1.0 + (a - 1.0) * ka_ref[0])).astype(k_o.dtype)


def _rwkv_prep_call(u_rw, p, layer, v_first, tt):
    B, T, _ = u_rw.shape
    first_layer = v_first is None
    tile = pl.BlockSpec((1, tt, D_RWKV), lambda b, t: (b, t, 0))
    names = ["mu", "w0", "w2", "a0", "a2", "g2", "k_k", "k_a"]
    in_specs = [
        pl.BlockSpec((1, tt, RW_COLS), lambda b, t: (b, t, 0)),
        pl.BlockSpec((1, 16, RW_COLS), lambda b, t: (b, jnp.maximum(t * (tt // 16) - 1, 0), 0)),
    ] + [_layer_spec(p[nm], layer) for nm in names]
    args = [u_rw, u_rw] + [p[nm] for nm in names]
    if not first_layer:
        in_specs += [tile] + [_layer_spec(p[nm], layer - 1) for nm in ("v0", "v1", "v2")]
        args += [v_first, p["v0"], p["v1"], p["v2"]]
    out = lambda dt: jax.ShapeDtypeStruct((B, T, D_RWKV), dt)
    return pl.pallas_call(
        functools.partial(_rwkv_prep_kernel, tt=tt, first_layer=first_layer),
        out_shape=[out(BF16), out(F32), out(F32)] + [out(BF16)] * 5,
        grid=(B, T // tt),
        in_specs=in_specs,
        out_specs=[tile] * 8,
        scratch_shapes=[pltpu.VMEM((tt + 8, RW_COLS), F32)],
        compiler_params=_cparams(("parallel", "parallel")),
        name="rwkv_prep",
    )(*args)


def _rwkv_scan_kernel(r_ref, ld_ref, cs_ref, k_ref, v_ref, kk_ref, a_ref, g_ref, rk_ref, lg_ref, lb_ref,
                      w1_ref, w2_ref, o_ref, w1b_ref, w2b_ref, s_scr, wr_scr, o0_scr, pm_scr, qm_scr, bn_scr, g_scr,
                      *, ts, nt, hpb):
    C = CHUNK
    nc = ts // C
    nb = hpb * nc
    N = HEAD_DIM
    n = pl.program_id(0)

    w1b_ref[...] = w1_ref[0].astype(BF16)
    w2b_ref[...] = w2_ref[0].astype(BF16)

    @pl.when(n == 0)
    def _():
        for scr in (s_scr, wr_scr, o0_scr, pm_scr, qm_scr, bn_scr, g_scr):
            scr[...] = jnp.zeros_like(scr)

    first = lax.rem(jnp.maximum(n - 1, 0), nt) == 0
    outs = []
    for hh in range(hpb):
        S = jnp.where(first, 0.0, s_scr[hh])
        for c in range(nc):
            i = hh * nc + c
            Sb = S.astype(BF16)
            outs.append(_dot_nt(wr_scr[i], Sb) + o0_scr[i])
            S = _dot(Sb, pm_scr[i]) + qm_scr[i]
        s_scr[hh] = S
    o = jnp.stack(outs, axis=0)

    def split_row(ref):
        x = ref[0]
        return jnp.concatenate([jnp.broadcast_to(x[None, :, h * N:(h + 1) * N], (nc, 1, N)) for h in range(hpb)],
                               axis=0)

    mu = jnp.mean(o, axis=-1, keepdims=True)
    oc = o - mu
    var = jnp.mean(oc * oc, axis=-1, keepdims=True)
    on = oc * lax.rsqrt(var + GN_EPS) * split_row(lg_ref) + split_row(lb_ref)
    res = (on + bn_scr[...]) * g_scr[...]
    o_ref[0] = jnp.concatenate([res[h * nc:(h + 1) * nc].reshape(ts, N) for h in range(hpb)],
                               axis=-1).astype(o_ref.dtype)

    def split(ref):
        x = ref[0].astype(F32).reshape(nc, C, hpb * N)
        return jnp.concatenate([x[:, :, h * N:(h + 1) * N] for h in range(hpb)], axis=0)

    r, ld, cs, k, v, kkr, a, g = (split(z) for z in (r_ref, ld_ref, cs_ref, k_ref, v_ref, kk_ref, a_ref, g_ref))
    kkn = kkr / jnp.maximum(jnp.sqrt(jnp.sum(kkr * kkr, axis=-1, keepdims=True)), 1e-12)
    row = lax.broadcasted_iota(jnp.int32, (C, C), 0)
    col = lax.broadcasted_iota(jnp.int32, (C, C), 1)
    incl = col <= row
    strict = col < row
    eye = col == row
    cs_last = cs[:, C - 1:C, :]
    e_in = jnp.exp(cs)
    e_ex = jnp.exp(cs - ld)
    e_neg = jnp.exp(-cs)
    e_hat = jnp.exp(cs_last - cs)
    at = -kkn * e_ex
    b = kkn * a
    bt = b * e_neg
    kt = k * e_neg
    rt = r * e_in
    bh = (b * e_hat).astype(BF16)
    kh = (k * e_hat).astype(BF16)
    vb = v.astype(BF16)
    A = _bmm_nt(jnp.concatenate([at, rt], axis=1).astype(BF16), jnp.concatenate([bt, kt], axis=1).astype(BF16))
    a_ab = jnp.where(strict, A[:, :C, :C], 0.0)
    a_ak = jnp.where(strict, A[:, :C, C:], 0.0).astype(BF16)
    a_rb = jnp.where(incl, A[:, C:, :C], 0.0).astype(BF16)
    a_rk = jnp.where(incl, A[:, C:, C:], 0.0).astype(BF16)
    npow = a_ab.astype(BF16)
    tinv = jnp.where(eye, 1.0, 0.0).astype(F32) + a_ab
    p2 = 2
    while p2 < C:
        npow_f = _bmm(npow, npow)
        npow = npow_f.astype(BF16)
        tinv = tinv + _bmm(npow, tinv.astype(BF16))
        p2 *= 2
    akv = _bmm(a_ak, vb)
    x = _bmm(tinv.astype(BF16), jnp.concatenate([at, akv], axis=-1).astype(BF16))
    xb = x.astype(BF16)
    y = _bmm(a_rb, xb)
    pq = _bmm_tn(xb, bh)
    wr_scr[...] = (rt + y[:, :, :N]).astype(BF16)
    o0_scr[...] = y[:, :, N:] + _bmm(a_rk, vb)
    pm_scr[...] = (pq[:, :N, :] + jnp.where(eye, jnp.exp(cs_last), 0.0)).astype(BF16)
    qm_scr[...] = pq[:, N:, :] + _bmm_tn(vb, kh)
    bn_scr[...] = jnp.sum(r * k * split_row(rk_ref), axis=-1, keepdims=True) * v
    g_scr[...] = g


def _rwkv_scan_call(r, ld, cs, k, v, kk, a, g, p, w1, w2, layer, ts, hpb):
    B, T, _ = r.shape
    nt = T // ts
    hg = RWKV_HEADS // hpb
    nblk = B * hg * nt
    nb = hpb * (ts // CHUNK)
    w = hpb * HEAD_DIM

    def blk(n):
        return n // (hg * nt), lax.rem(n, nt), lax.rem(n // nt, hg)

    cur = lambda n: blk(jnp.minimum(n, nblk - 1))
    prev = lambda n: blk(jnp.maximum(n - 1, 0))
    tile = pl.BlockSpec((1, ts, w), cur)
    rowp = lambda f: pl.BlockSpec((1, 1, w), lambda n: (layer, 0, f(n)[2]))
    sq = lambda dt: pltpu.VMEM((nb, HEAD_DIM, HEAD_DIM), dt)
    _, d_in, d_ff = w1.shape
    rows1, rows2 = d_in // nblk, d_ff // nblk
    assert rows1 * nblk == d_in and rows2 * nblk == d_ff and rows1 % 16 == 0 and rows2 % 16 == 0
    slab = lambda n: jnp.minimum(n, nblk - 1)
    return pl.pallas_call(
        functools.partial(_rwkv_scan_kernel, ts=ts, nt=nt, hpb=hpb),
        out_shape=[jax.ShapeDtypeStruct((B, T, D_RWKV), BF16),
                   jax.ShapeDtypeStruct((d_in, d_ff), BF16), jax.ShapeDtypeStruct((d_ff, d_in), BF16)],
        grid=(nblk + 1,),
        in_specs=[tile] * 8 + [rowp(cur), rowp(prev), rowp(prev),
                               pl.BlockSpec((1, rows1, d_ff), lambda n: (layer, slab(n), 0)),
                               pl.BlockSpec((1, rows2, d_in), lambda n: (layer, slab(n), 0))],
        out_specs=[pl.BlockSpec((1, ts, w), prev), pl.BlockSpec((rows1, d_ff), lambda n: (slab(n), 0)),
                   pl.BlockSpec((rows2, d_in), lambda n: (slab(n), 0))],
        scratch_shapes=[pltpu.VMEM((hpb, HEAD_DIM, HEAD_DIM), F32), pltpu.VMEM((nb, CHUNK, HEAD_DIM), BF16),
                        pltpu.VMEM((nb, CHUNK, HEAD_DIM), F32), sq(BF16), sq(F32),
                        pltpu.VMEM((nb, CHUNK, HEAD_DIM), F32), pltpu.VMEM((nb, CHUNK, HEAD_DIM), F32)],
        compiler_params=_cparams(("arbitrary",)),
        name="rwkv_scan",
    )(r, ld, cs, k, v, kk, a, g, p["r_k"], p["lnx_g"], p["lnx_b"], w1, w2)


def _nsa_prep_kernel(u_ref, cos_ref, sin_ref, q_o, kc_o, vc_o, ks_o, vs_o, kw_o, vw_o, gate_o):
    cos = cos_ref[...]
    sin = sin_ref[...]
    lane = lax.broadcasted_iota(jnp.int32, cos.shape, 1)
    first_half = (lane % HEAD_DIM) < (HEAD_DIM // 2)

    def rope(x):
        other = jnp.where(first_half, pltpu.roll(x, LANES - HEAD_DIM // 2, 1), pltpu.roll(x, HEAD_DIM // 2, 1))
        return x * cos + other * sin

    def put(out_ref, col0, nheads, roped, scale=None):
        for j in range(nheads // 2):
            x = u_ref[0, :, col0 + j * LANES:col0 + (j + 1) * LANES].astype(F32)
            if roped:
                x = rope(x)
            if scale is not None:
                x = x * scale
            out_ref[0, 2 * j] = x[:, :HEAD_DIM].astype(out_ref.dtype)
            out_ref[0, 2 * j + 1] = x[:, HEAD_DIM:].astype(out_ref.dtype)

    low = lane < HEAD_DIM

    def put_wide(out_ref, col0, nheads, scale, tail):
        for j in range(nheads // 2):
            x = rope(u_ref[0, :, col0 + j * LANES:col0 + (j + 1) * LANES].astype(F32))
            if scale is not None:
                x = x * scale
            out_ref[0, 2 * j] = jnp.where(low, x, tail).astype(out_ref.dtype)
            out_ref[0, 2 * j + 1] = jnp.where(low, pltpu.roll(x, HEAD_DIM, 1), tail).astype(out_ref.dtype)

    def put_t(out_ref, col0, nheads):
        extra = out_ref.shape[2] - HEAD_DIM
        ones_row = jnp.where(lax.broadcasted_iota(jnp.int32, (extra, cos.shape[0]), 0) == 0, 1.0, 0.0)
        for j in range(nheads // 2):
            xt = u_ref[0, :, col0 + j * LANES:col0 + (j + 1) * LANES].astype(F32).T
            for i, part in enumerate((xt[:HEAD_DIM, :], xt[HEAD_DIM:, :])):
                out_ref[0, 2 * j + i, :HEAD_DIM, :] = part.astype(out_ref.dtype)
                out_ref[0, 2 * j + i, HEAD_DIM:, :] = ones_row.astype(out_ref.dtype)

    pos = pl.program_id(1) * cos.shape[0] + lax.broadcasted_iota(jnp.int32, cos.shape, 0)
    blk_onehot = jnp.where(lane - HEAD_DIM == pos // SEL_BLOCK, 1.0, 0.0)
    put_wide(q_o, 0, NSA_HEADS, HEAD_DIM ** -0.5 * LOG2_E, 0.0)
    put(kc_o, NS_KC, NSA_KV_HEADS, True)
    put(vc_o, NS_KC + NSA_KV, NSA_KV_HEADS, False)
    put_wide(ks_o, NS_KC + 2 * NSA_KV, NSA_KV_HEADS, None, blk_onehot)
    put_t(vs_o, NS_KC + 3 * NSA_KV, NSA_KV_HEADS)
    put_wide(kw_o, NS_KC + 4 * NSA_KV, NSA_KV_HEADS, None, 0.0)
    put_t(vw_o, NS_KC + 5 * NSA_KV, NSA_KV_HEADS)
    gates_t = jax.nn.sigmoid(u_ref[0, :, NS_GATE:NS_GATE + LANES].astype(F32)).T
    for gi in range(NSA_KV_HEADS):
        gate_o[0, gi] = gates_t[gi * GATE_PAD:(gi + 1) * GATE_PAD, :]


def _nsa_prep_call(u_ns, cos_t, sin_t, tt):
    B, T, _ = u_ns.shape
    G, H, N = NSA_KV_HEADS, NSA_HEADS, HEAD_DIM
    assert T // SEL_BLOCK <= LANES - N
    kv = jax.ShapeDtypeStruct((B, G, T, N), F32)
    kv_wide = jax.ShapeDtypeStruct((B, G, T, LANES), BF16)
    kv_t = jax.ShapeDtypeStruct((B, G, VT_ROWS, T), BF16)
    kv_spec = pl.BlockSpec((1, G, tt, N), lambda b, t: (b, 0, t, 0))
    kvw_spec = pl.BlockSpec((1, G, tt, LANES), lambda b, t: (b, 0, t, 0))
    kvt_spec = pl.BlockSpec((1, G, VT_ROWS, tt), lambda b, t: (b, 0, 0, t))
    return pl.pallas_call(
        _nsa_prep_kernel,
        out_shape=[jax.ShapeDtypeStruct((B, H, T, LANES), BF16), kv, kv, kv_wide, kv_t,
                   kv_wide, kv_t, jax.ShapeDtypeStruct((B, G, GATE_PAD, T), F32)],
        grid=(B, T // tt),
        in_specs=[
            pl.BlockSpec((1, tt, NS_COLS), lambda b, t: (b, t, 0)),
            pl.BlockSpec((tt, LANES), lambda b, t: (t, 0)),
            pl.BlockSpec((tt, LANES), lambda b, t: (t, 0)),
        ],
        out_specs=[pl.BlockSpec((1, H, tt, LANES), lambda b, t: (b, 0, t, 0)), kv_spec, kv_spec, kvw_spec, kvt_spec,
                   kvw_spec, kvt_spec, pl.BlockSpec((1, G, GATE_PAD, tt), lambda b, t: (b, 0, 0, t))],
        compiler_params=_cparams(("parallel", "parallel")),
        name="nsa_prep",
    )(u_ns, cos_t, sin_t)


def _compress_kernel(x_ref, pos_ref, w1_ref, w2_ref, o_ref):
    x = x_ref[0, 0]
    half = x.shape[1]
    y_top = _dot(x + pos_ref[0, 0:1, :], w1_ref[0, :half, :], "x3")
    y_bot = _dot(x + pos_ref[0, 1:2, :], w1_ref[0, half:, :], "x3")
    pre = y_top + pltpu.roll(y_bot, x.shape[0] - 1, 0)
    o_ref[0, 0] = _dot(jax.nn.gelu(pre), w2_ref[0], "x3")


def _compress_call(kv, pos, w1, w2, which):
    B, G, T, N = kv.shape
    n_half = T // CMP_STRIDE
    width = CMP_STRIDE * N
    hidden = w1.shape[-1]
    x = kv.reshape(B, G, n_half, width)
    return pl.pallas_call(
        _compress_kernel,
        out_shape=jax.ShapeDtypeStruct((B, G, n_half, w2.shape[-1]), F32),
        grid=(B, G),
        in_specs=[
            pl.BlockSpec((1, 1, n_half, width), lambda b, g: (b, g, 0, 0)),
            pl.BlockSpec((1, 2, width), lambda b, g: (which, 0, 0)),
            pl.BlockSpec((1, 2 * width, hidden), lambda b, g: (which, 0, 0)),
            pl.BlockSpec((1, hidden, w2.shape[-1]), lambda b, g: (which, 0, 0)),
        ],
        out_specs=pl.BlockSpec((1, 1, n_half, w2.shape[-1]), lambda b, g: (b, g, 0, 0)),
        compiler_params=_cparams(("parallel", "parallel")),
        name="nsa_compress",
    )(x, pos, w1, w2)


def _nsa_attn_kernel(q_ref, kc_ref, vc_ref, ks_ref, vst_ref, kw_ref, vwt_ref, gate_ref, ov_ref, put_ref,
                     o_ref, s0_scr, s1_scr, *, tq, tk):
    HPG, N = HEADS_PER_GROUP, HEAD_DIM
    qi = pl.program_id(2)
    q0 = qi * tq
    k_len = ks_ref.shape[2]

    wq = HPG * tq
    q_win = q_ref[0].reshape(wq, LANES)
    rel = lax.broadcasted_iota(jnp.int32, (tk, tq), 1) - lax.broadcasted_iota(jnp.int32, (tk, tq), 0)
    init = (jnp.full((1, wq), NEG, F32), jnp.zeros((VT_ROWS, wq), F32))
    n_tiles = k_len // tk

    def scores(q_all, k_ref, k0, dst):
        dst[...] = _dot_nt(k_ref[0, 0, pl.ds(k0, tk), :], q_all)

    def update(carry, src, vt_ref, k0, bias):
        m_i, acc = carry
        s = src[...]
        if bias is not None:
            s = s + (bias if bias.ndim == 0 else jnp.concatenate([bias] * HPG, axis=1))
        m_n = jnp.maximum(m_i, jnp.max(s, axis=0, keepdims=True))
        p = jnp.exp2(s - m_n)
        return m_n, jnp.exp2(m_i - m_n) * acc + _dot(vt_ref[0, 0, :, pl.ds(k0, tk)], p.astype(BF16))

    def split(o):
        return [o[:, h * tq:(h + 1) * tq] for h in range(HPG)]

    def finish(carry):
        return split(carry[1][:N] / carry[1][N:N + 1])

    def window_tile(e):
        koff = e * tk - WINDOW
        lo, hi = -(tk - 1) - koff, (tq - 1) - koff
        bias = None
        if lo < 0 or hi >= WINDOW:
            dist = rel - koff
            bias = jnp.where((dist >= 0) & (dist < WINDOW), 0.0, NEG)
        if koff < 0:
            bias = jnp.where(q0 + koff >= 0, 0.0 if bias is None else bias, NEG)
        return pl.multiple_of(jnp.maximum(q0 + koff, 0), tk), bias

    win = [window_tile(e) for e in range((WINDOW + tq) // tk)]
    assert len(win) in (2, 3)
    bufs = (s0_scr, s1_scr)
    scores(q_win, kw_ref, win[0][0], bufs[0])

    n_cmp = kc_ref.shape[2]
    c_c = lax.broadcasted_iota(jnp.int32, (n_cmp, tq), 0)
    t_c = q0 + lax.broadcasted_iota(jnp.int32, (n_cmp, tq), 1)
    cmask = jnp.concatenate([c_c * CMP_STRIDE + (CMP_BLOCK - 1) <= t_c] * HPG, axis=1)
    s = jnp.where(cmask, _dot_nt(kc_ref[0, 0].astype(BF16), q_win), NEG)
    scores(q_win, kw_ref, win[1][0], bufs[1])
    m = jnp.max(s, axis=0, keepdims=True)
    e = jnp.where(cmask, jnp.exp2(s - m), 0.0)
    l = jnp.sum(e, axis=0, keepdims=True)
    p = e / jnp.maximum(l, 1e-30)
    o_cmp = split(_dot(vc_ref[0, 0].T[:N].astype(BF16), p.astype(BF16)))
    psum = p[:, :tq]
    for h in range(1, HPG):
        psum = psum + p[:, h * tq:(h + 1) * tq]

    n_sel = ov_ref.shape[0]
    imp = _dot(ov_ref[...], psum, "r3")
    carry = update(init, bufs[0], vwt_ref, *win[0])
    if len(win) == 3:
        scores(q_win, kw_ref, win[2][0], bufs[0])
    blk = lax.broadcasted_iota(jnp.int32, (n_sel, tq), 0)
    t_s = q0 + lax.broadcasted_iota(jnp.int32, (n_sel, tq), 1)
    cur = t_s // SEL_BLOCK
    forced = (blk == 0) | (blk == cur) | (blk == cur - 1)
    val = jnp.where(forced, FORCE, jnp.where(blk * SEL_BLOCK <= t_s, imp, -1.0))
    rank = jnp.zeros((n_sel, tq), F32)
    for i in range(n_sel):
        vi = val[i:i + 1, :]
        ahead = (vi > val) | ((vi == val) & (blk > i))
        rank = rank + jnp.where(ahead, 1.0, 0.0)
    unsel_t = jnp.where(rank < float(min(N_SELECT, n_sel)), 0.0, NEG)
    q_bias = lax.dot_general(unsel_t.astype(BF16), put_ref[...], (((0,), (0,)), ((), ())),
                             preferred_element_type=F32)
    carry = update(carry, bufs[1], vwt_ref, *win[1])
    q_sel = jnp.concatenate([(q_ref[0, h].astype(F32) + q_bias).astype(BF16) for h in range(HPG)], axis=0)
    o_win = finish(update(carry, bufs[0], vwt_ref, *win[2]) if len(win) == 3 else carry)

    def key0(j):
        return pl.multiple_of(j * tk, tk)

    scores(q_sel, ks_ref, key0(0), s0_scr)

    def pair(jj, carry):
        scores(q_sel, ks_ref, key0(2 * jj + 1), s1_scr)
        carry = update(carry, s0_scr, vst_ref, key0(2 * jj), None)
        scores(q_sel, ks_ref, key0(2 * jj + 2), s0_scr)
        return update(carry, s1_scr, vst_ref, key0(2 * jj + 1), None)

    carry = lax.fori_loop(0, qi // 2, pair, init)
    t1 = 2 * (qi // 2)
    t2 = jnp.minimum(t1 + 1, n_tiles - 1)

    def causal(t):
        return jnp.where(rel >= (t - qi) * tk, 0.0, NEG)

    scores(q_sel, ks_ref, key0(t2), s1_scr)
    carry = update(carry, s0_scr, vst_ref, key0(t1), causal(t1))
    o_slc = finish(update(carry, s1_scr, vst_ref, key0(t2), causal(t1 + 1)))

    gate = gate_ref[0, 0]
    outs = [gate[3 * h:3 * h + 1, :] * o_cmp[h] + gate[3 * h + 1:3 * h + 2, :] * o_slc[h]
            + gate[3 * h + 2:3 * h + 3, :] * o_win[h] for h in range(HPG)]
    o_ref[0] = jnp.concatenate(outs, axis=0).T.astype(o_ref.dtype)


def _nsa_attn_call(q, kc, vc, ks, vs_t, kw, vw_t, gates_t, ov_t, put, tq, tk):
    B, H, T, _ = q.shape
    N = HEAD_DIM
    G, HPG = NSA_KV_HEADS, HEADS_PER_GROUP
    assert WINDOW % tk == 0 and tq == tk
    n_half = kc.shape[2]
    cmp_spec = pl.BlockSpec((1, 1, n_half, LANES), lambda b, g, t: (b, g, 0, 0))
    kv_spec = pl.BlockSpec((1, 1, T, LANES), lambda b, g, t: (b, g, 0, 0))
    kvt_spec = pl.BlockSpec((1, 1, VT_ROWS, T), lambda b, g, t: (b, g, 0, 0))
    return pl.pallas_call(
        functools.partial(_nsa_attn_kernel, tq=tq, tk=tk),
        out_shape=jax.ShapeDtypeStruct((B, T, D_NSA), BF16),
        grid=(B, G, T // tq),
        in_specs=[
            pl.BlockSpec((1, HPG, tq, LANES), lambda b, g, t: (b, g, t, 0)),
            cmp_spec, cmp_spec, kv_spec, kvt_spec, kv_spec, kvt_spec,
            pl.BlockSpec((1, 1, GATE_PAD, tq), lambda b, g, t: (b, g, 0, t)),
            pl.BlockSpec(ov_t.shape, lambda b, g, t: (0, 0)),
            pl.BlockSpec(put.shape, lambda b, g, t: (0, 0)),
        ],
        out_specs=pl.BlockSpec((1, tq, HPG * N), lambda b, g, t: (b, t, g)),
        scratch_shapes=[pltpu.VMEM((tk, HPG * tq), F32), pltpu.VMEM((tk, HPG * tq), F32)],
        compiler_params=_cparams(("parallel", "parallel", "arbitrary")),
        name="nsa_attn",
    )(q, kc, vc, ks, vs_t, kw, vw_t, gates_t, ov_t, put)


def _outproj_kernel(orw_ref, ons_ref, x_ref, gt_ref, w_ref, g_ref, b_ref, o_ref, *, alpha):
    half = orw_ref.shape[-1]
    y = _dot(orw_ref[0].astype(BF16), w_ref[0, :half, :]) + _dot(ons_ref[0].astype(BF16), w_ref[0, half:, :])
    z = alpha * x_ref[0] + (1.0 + gt_ref[0]) * y
    o_ref[0] = _layer_norm_rows(z, g_ref[0], b_ref[0])


def _outproj_call(o_rw, o_ns, x, mod3, row0, w_bf16, ln_g, ln_b, layer, alpha, tm):
    B, T, D = x.shape
    half = o_rw.shape[-1]
    return pl.pallas_call(
        functools.partial(_outproj_kernel, alpha=alpha),
        out_shape=jax.ShapeDtypeStruct((B, T, D), F32),
        grid=(B, T // tm),
        in_specs=[
            pl.BlockSpec((1, tm, half), lambda b, m: (b, m, 0)),
            pl.BlockSpec((1, tm, half), lambda b, m: (b, m, 0)),
            pl.BlockSpec((1, tm, D), lambda b, m: (b, m, 0)),
            pl.BlockSpec((1, 1, D), lambda b, m: (row0 + 6 * b + 2, 0, 0)),
            _layer_spec(w_bf16, layer), _layer_spec(ln_g, layer), _layer_spec(ln_b, layer),
        ],
        out_specs=pl.BlockSpec((1, tm, D), lambda b, m: (b, m, 0)),
        compiler_params=_cparams(("parallel", "parallel")),
        name="outproj_ln",
    )(o_rw, o_ns, x, mod3, w_bf16, ln_g, ln_b)


def _mlp_kernel(x_ref, sc_ref, sh_ref, gt_ref, w1_ref, w2_ref, g_ref, b_ref, o_ref, h_scr, acc_scr, *, alpha):
    f = pl.program_id(2)

    @pl.when(f == 0)
    def _():
        h = x_ref[0] * (1.0 + sc_ref[0]) + sh_ref[0]
        h_scr[...] = h.astype(BF16)
        acc_scr[...] = jnp.zeros_like(acc_scr)

    a = jnp.maximum(_dot(h_scr[...], w1_ref[...]), 0.0)
    acc_scr[...] += _dot((a * a).astype(BF16), w2_ref[...])

    @pl.when(f == pl.num_programs(2) - 1)
    def _():
        z = alpha * x_ref[0] + (1.0 + gt_ref[0]) * acc_scr[...]
        o_ref[0] = _layer_norm_rows(z, g_ref[0], b_ref[0])


def _mlp_call(x, mod3, row0, w1_bf16, w2_bf16, ln_g, ln_b, layer, alpha, tm, tf):
    B, T, D = x.shape
    FF = w1_bf16.shape[1]
    modspec = lambda j: pl.BlockSpec((1, 1, D), lambda b, m, f: (row0 + 6 * b + j, 0, 0))
    return pl.pallas_call(
        functools.partial(_mlp_kernel, alpha=alpha),
        out_shape=jax.ShapeDtypeStruct((B, T, D), F32),
        grid=(B, T // tm, FF // tf),
        in_specs=[
            pl.BlockSpec((1, tm, D), lambda b, m, f: (b, m, 0)),
            modspec(4), modspec(3), modspec(5),
            pl.BlockSpec((D, tf), lambda b, m, f: (0, f)),
            pl.BlockSpec((tf, D), lambda b, m, f: (f, 0)),
            _layer_spec(ln_g, layer), _layer_spec(ln_b, layer),
        ],
        out_specs=pl.BlockSpec((1, tm, D), lambda b, m, f: (b, m, 0)),
        scratch_shapes=[pltpu.VMEM((tm, D), BF16), pltpu.VMEM((tm, D), F32)],
        compiler_params=_cparams(("parallel", "parallel", "arbitrary")),
        name="mlp_ln",
    )(x, mod3, mod3, mod3, w1_bf16, w2_bf16, ln_g, ln_b)


def _pad_last(w, n):
    return jnp.pad(w, [(0, 0)] * (w.ndim - 1) + [(0, n - w.shape[-1])])


def _pad_rows(w, n):
    return jnp.pad(w, [(0, 0)] * (w.ndim - 2) + [(0, n - w.shape[-2]), (0, 0)])


def _split_w_in(w_in):
    c = np.cumsum([0, D_RWKV, D_RWKV, D_RWKV, DECAY_LORA, AAA_LORA, GATE_LORA]).tolist()
    w_in = w_in.astype(BF16)
    rw = jnp.concatenate([w_in[..., c[0]:c[3]], _pad_last(w_in[..., c[3]:c[4]], LORA_PAD),
                          _pad_last(w_in[..., c[4]:c[5]], LORA_PAD), w_in[..., c[5]:c[6]]], axis=-1)
    ng = 3 * HEADS_PER_GROUP
    lead = w_in.shape[:-1]
    gates = _pad_last(w_in[..., c[6] + NS_GATE:].reshape(lead + (NSA_KV_HEADS, ng)), GATE_PAD)
    gates = _pad_last(gates.reshape(lead + (NSA_KV_HEADS * GATE_PAD,)), LANES)
    ns = jnp.concatenate([w_in[..., c[6]:c[6] + NS_GATE], gates], axis=-1)
    return jnp.concatenate([rw, ns], axis=-1)


def _pad_mu(mu):
    c = np.cumsum([0, 3 * D_RWKV, DECAY_LORA, AAA_LORA, GATE_LORA]).tolist()
    parts = [mu[:, c[0]:c[1]], _pad_last(mu[:, c[1]:c[2]], LORA_PAD), _pad_last(mu[:, c[2]:c[3]], LORA_PAD),
             mu[:, c[3]:c[4]]]
    return jnp.concatenate(parts, axis=-1)[:, None, :]


def _rope_tables(T):
    half = HEAD_DIM // 2
    inv = ROPE_THETA ** (-jnp.arange(half, dtype=F32) / half)
    ang = jnp.arange(T, dtype=F32)[:, None] * inv[None]
    cos, sin = jnp.cos(ang), jnp.sin(ang)
    cos_t = jnp.tile(cos, (1, LANES // half))
    sin_t = jnp.tile(jnp.concatenate([-sin, sin], axis=1), (1, LANES // HEAD_DIM))
    return cos_t, sin_t


def _selection_constants(T):
    n_half = T // CMP_STRIDE
    n_cmp = (T - CMP_BLOCK) // CMP_STRIDE + 1
    n_sel = T // SEL_BLOCK
    pos = np.arange(n_cmp)[:, None] * CMP_STRIDE + np.arange(CMP_BLOCK)[None]
    ov = ((pos // SEL_BLOCK)[..., None] == np.arange(n_sel)).sum(1) / CMP_BLOCK
    ov_t = np.zeros((n_sel, n_half), np.float32)
    ov_t[:, :n_cmp] = ov.T
    put = (np.arange(LANES)[None, :] == HEAD_DIM + np.arange(n_sel)[:, None]).astype(np.float32)
    return jnp.asarray(ov_t), jnp.asarray(put, BF16)


def kernel(x, c, w_ada, b_ada, w_in, rwkv_mu, rwkv_w0, rwkv_w2, rwkv_a0, rwkv_a2, rwkv_g2, rwkv_k_k, rwkv_k_a, rwkv_r_k, rwkv_lnx_g, rwkv_lnx_b, rwkv_v0, rwkv_v1, rwkv_v2, nsa_cmp_pos, nsa_cmp_w1, nsa_cmp_w2, w_out, ln1_g, ln1_b, mlp_w1, mlp_w2, ln2_g, ln2_b):
    B, T, D = x.shape
    L = w_ada.shape[0]
    alpha = (2 * L) ** 0.25

    c_pad = jnp.pad(c, ((0, -B % 8), (0, 0)))
    mod = _ada_call(c_pad, w_ada, b_ada)[:, :B]
    mod3 = mod.reshape(L * B * 6, 1, D)
    cos_t, sin_t = _rope_tables(T)
    ov_t, sel_put = _selection_constants(T)

    rows = lambda z: z.reshape(z.shape[0], 1, -1)
    w_in_b = _split_w_in(w_in)
    p = {
        "mu": _pad_mu(rwkv_mu), "w0": rows(rwkv_w0), "a0": rows(rwkv_a0),
        "w2": _pad_rows(rwkv_w2, LORA_PAD).astype(BF16), "a2": _pad_rows(rwkv_a2, LORA_PAD).astype(BF16),
        "g2": rwkv_g2.astype(BF16), "k_k": rows(rwkv_k_k), "k_a": rows(rwkv_k_a),
        "r_k": rows(rwkv_r_k), "lnx_g": rows(rwkv_lnx_g), "lnx_b": rows(rwkv_lnx_b),
        "v0": rows(rwkv_v0), "v1": _pad_last(rwkv_v1, LORA_PAD).astype(BF16),
        "v2": _pad_rows(rwkv_v2, LORA_PAD).astype(BF16),
    }
    half_block = CMP_STRIDE * HEAD_DIM
    cmp_pos = nsa_cmp_pos.reshape(2 * L, 2, half_block)
    cmp_w1 = nsa_cmp_w1.reshape((2 * L,) + nsa_cmp_w1.shape[2:])
    cmp_w2 = _pad_last(nsa_cmp_w2.reshape((2 * L,) + nsa_cmp_w2.shape[2:]), LANES)
    w_out_b = w_out.astype(BF16)
    ln1 = (rows(ln1_g), rows(ln1_b))
    ln2 = (rows(ln2_g), rows(ln2_b))

    v_first = None
    for i in range(L):
        row0 = i * B * 6
        u_rw, u_ns = _modmm_call(x, mod3, row0, w_in_b, i, RW_COLS, TILES["proj_m"], PROJ_TN)

        r, ld, cs, k, v, kk, a, g = _rwkv_prep_call(u_rw, p, i, v_first, TILES["prep"])
        if i == 0:
            v_first = v
        o_rw, w1_b, w2_b = _rwkv_scan_call(r, ld, cs, k, v, kk, a, g, p, mlp_w1, mlp_w2, i, TILES["scan_rows"],
                                           TILES["scan_heads"])

        q, kc_in, vc_in, ks, vs_t, kw, vw_t, gates_t = _nsa_prep_call(u_ns, cos_t, sin_t, TILES["prep"])
        kc = _compress_call(kc_in, cmp_pos, cmp_w1, cmp_w2, 2 * i)
        vc = _compress_call(vc_in, cmp_pos, cmp_w1, cmp_w2, 2 * i + 1)
        o_ns = _nsa_attn_call(q, kc, vc, ks, vs_t, kw, vw_t, gates_t, ov_t, sel_put, TILES["attn_q"], TILES["attn_k"])

        x = _outproj_call(o_rw, o_ns, x, mod3, row0, w_out_b, *ln1, i, alpha, TILES["out_m"])
        x = _mlp_call(x, mod3, row0, w1_b, w2_b, *ln2, i, alpha, TILES["mlp_m"], TILES["mlp_f"])
    return x
```

```python
import functools

import numpy as np
import jax
import jax.numpy as jnp
from jax import lax
from jax.experimental import pallas as pl
from jax.experimental.pallas import tpu as pltpu

F32 = jnp.float32
BF16 = jnp.bfloat16
HI = lax.Precision.HIGHEST

HEAD_DIM = 64
RWKV_HEADS = 16
NSA_HEADS = 16
NSA_KV_HEADS = 4
HEADS_PER_GROUP = NSA_HEADS // NSA_KV_HEADS
D_RWKV = RWKV_HEADS * HEAD_DIM
D_NSA = NSA_HEADS * HEAD_DIM
NSA_KV = NSA_KV_HEADS * HEAD_DIM
DECAY_LORA = 96
AAA_LORA = 96
MV_LORA = 64
GATE_LORA = 256
GN_EPS = 64e-5
CMP_BLOCK = 32
CMP_STRIDE = 16
SEL_BLOCK = 64
N_SELECT = 8
WINDOW = 512
ROPE_THETA = 10000.0
NEG = -1e30
FORCE = 1e4
LN_EPS = 1e-5
LOG2_E = 1.4426950408889634
EXP_NEG_HALF = 0.6065306597126334

LANES = 128
LORA_PAD = LANES
RW_XW = 3 * D_RWKV
RW_XA = RW_XW + LORA_PAD
RW_XG = RW_XA + LORA_PAD
RW_COLS = RW_XG + GATE_LORA
NS_KC = D_NSA
NS_GATE = D_NSA + 6 * NSA_KV
NS_COLS = NS_GATE + LANES
GATE_PAD = 16
VT_ROWS = HEAD_DIM + 16
PROJ_TN = 896

CHUNK = 64
V7X_VMEM_BYTES = 64 * 1024 * 1024
VMEM_LIMIT = V7X_VMEM_BYTES - 8 * 1024 * 1024
TILES = dict(proj_m=1024, prep=256, scan_rows=128, scan_heads=8, attn_q=256, attn_k=256, out_m=512, mlp_m=512,
             mlp_f=1024)


def _layer_spec(arr, layer):
    tail = (0,) * (arr.ndim - 1)
    return pl.BlockSpec((1,) + arr.shape[1:], lambda *_: (layer,) + tail)


def _cparams(sem):
    return pltpu.CompilerParams(dimension_semantics=sem, vmem_limit_bytes=VMEM_LIMIT)


def _mm(fn, a, b, mode):
    if mode is None:
        return fn(a, b, None)
    if mode == "hi":
        return fn(a, b, HI)
    ah, bh = a.astype(BF16), b.astype(BF16)
    if mode == "bf":
        return fn(ah, bh, None)
    if mode == "r3":
        r1 = b - bh.astype(F32)
        bm = r1.astype(BF16)
        bl = (r1 - bm.astype(F32)).astype(BF16)
        return fn(ah, bh, None) + (fn(ah, bm, None) + fn(ah, bl, None))
    al = (a - ah.astype(F32)).astype(BF16)
    bl = (b - bh.astype(F32)).astype(BF16)
    return fn(ah, bh, None) + (fn(ah, bl, None) + fn(al, bh, None))


def _dot(a, b, mode=None):
    return _mm(lambda p, q, pr: jnp.dot(p, q, preferred_element_type=F32, precision=pr), a, b, mode)


def _dot_nt(a, b, mode=None):
    return _mm(lambda p, q, pr: lax.dot_general(p, q, (((1,), (1,)), ((), ())), preferred_element_type=F32,
                                               precision=pr), a, b, mode)


def _bmm(a, b, mode=None):
    return _mm(lambda p, q, pr: jnp.einsum("bij,bjk->bik", p, q, preferred_element_type=F32, precision=pr),
               a, b, mode)


def _bmm_nt(a, b, mode=None):
    return _mm(lambda p, q, pr: jnp.einsum("bik,bjk->bij", p, q, preferred_element_type=F32, precision=pr),
               a, b, mode)


def _bmm_tn(a, b, mode=None):
    return _mm(lambda p, q, pr: jnp.einsum("bci,bcj->bij", p, q, preferred_element_type=F32, precision=pr),
               a, b, mode)


def _layer_norm_rows(z, g, b):
    mu = jnp.mean(z, axis=-1, keepdims=True)
    zc = z - mu
    var = jnp.mean(zc * zc, axis=-1, keepdims=True)
    return zc * lax.rsqrt(var + LN_EPS) * g + b


def _ada_kernel(c_ref, w_ref, b_ref, o_ref):
    c = c_ref[...]
    cond = c * jax.nn.sigmoid(c)
    o_ref[0] = _dot(cond.astype(BF16), w_ref[0].astype(BF16)) + b_ref[0]


def _ada_call(c_pad, w_ada, b_ada):
    L, D, N = w_ada.shape
    tn = 1024
    return pl.pallas_call(
        _ada_kernel,
        out_shape=jax.ShapeDtypeStruct((L, c_pad.shape[0], N), F32),
        grid=(L, N // tn),
        in_specs=[
            pl.BlockSpec(c_pad.shape, lambda l, n: (0, 0)),
            pl.BlockSpec((1, D, tn), lambda l, n: (l, 0, n)),
            pl.BlockSpec((1, 1, tn), lambda l, n: (l, 0, n)),
        ],
        out_specs=pl.BlockSpec((1, c_pad.shape[0], tn), lambda l, n: (l, 0, n)),
        compiler_params=_cparams(("parallel", "parallel")),
        name="adaln_mod",
    )(c_pad, w_ada, b_ada.reshape(L, 1, N))


def _modmm_kernel(x_ref, sc_ref, sh_ref, w_ref, o1_ref, o2_ref, h_scr, *, n1):
    n = pl.program_id(2)

    @pl.when(n == 0)
    def _():
        h = x_ref[0] * (1.0 + sc_ref[0]) + sh_ref[0]
        h_scr[...] = h.astype(BF16)

    y = _dot(h_scr[...], w_ref[0]).astype(o1_ref.dtype)

    @pl.when(n < n1)
    def _():
        o1_ref[0] = y

    @pl.when(n >= n1)
    def _():
        o2_ref[0] = y


def _modmm_call(x, mod3, row0, w_bf16, layer, n_first, tm, tn):
    B, T, D = x.shape
    N = w_bf16.shape[2]
    n1 = n_first // tn
    return pl.pallas_call(
        functools.partial(_modmm_kernel, n1=n1),
        out_shape=[jax.ShapeDtypeStruct((B, T, n_first), BF16), jax.ShapeDtypeStruct((B, T, N - n_first), BF16)],
        grid=(B, T // tm, N // tn),
        in_specs=[
            pl.BlockSpec((1, tm, D), lambda b, m, n: (b, m, 0)),
            pl.BlockSpec((1, 1, D), lambda b, m, n: (row0 + 6 * b + 1, 0, 0)),
            pl.BlockSpec((1, 1, D), lambda b, m, n: (row0 + 6 * b, 0, 0)),
            pl.BlockSpec((1, D, tn), lambda b, m, n: (layer, 0, n)),
        ],
        out_specs=[pl.BlockSpec((1, tm, tn), lambda b, m, n: (b, m, jnp.minimum(n, n1 - 1))),
                   pl.BlockSpec((1, tm, tn), lambda b, m, n: (b, m, jnp.maximum(n - n1, 0)))],
        scratch_shapes=[pltpu.VMEM((tm, D), BF16)],
        compiler_params=_cparams(("parallel", "parallel", "arbitrary")),
        name="inproj",
    )(x, mod3, mod3, w_bf16)


def _rwkv_prep_kernel(*refs, tt, first_layer):
    if first_layer:
        (u_ref, up_ref, mu_ref, w0_ref, w2_ref, a0_ref, a2_ref, g2_ref, kk_ref, ka_ref,
         r_o, ld_o, cs_o, k_o, v_o, kk_o, a_o, g_o) = refs
    else:
        (u_ref, up_ref, mu_ref, w0_ref, w2_ref, a0_ref, a2_ref, g2_ref, kk_ref, ka_ref,
         vf_ref, v0_ref, v1_ref, v2_ref,
         r_o, ld_o, cs_o, k_o, v_o, kk_o, a_o, g_o) = refs
    ti = pl.program_id(1)
    ub = u_ref[0]
    u = ub.astype(F32)
    row = lax.broadcasted_iota(jnp.int32, (tt, tt), 0)
    col = lax.broadcasted_iota(jnp.int32, (tt, tt), 1)
    us = _dot(jnp.where(col + 1 == row, 1.0, 0.0).astype(BF16), ub)
    prev_row = jnp.where(ti > 0, up_ref[0, 15:16, :].astype(F32), 0.0)
    us = jnp.where(lax.broadcasted_iota(jnp.int32, (tt, 1), 0) == 0, prev_row, us)
    x = u + (us - u) * mu_ref[0]
    r = x[:, 0:D_RWKV]
    k = x[:, D_RWKV:2 * D_RWKV]
    v = x[:, 2 * D_RWKV:3 * D_RWKV]
    xw = x[:, RW_XW:RW_XA]
    xa = x[:, RW_XA:RW_XG]
    xg = x[:, RW_XG:RW_COLS]
    z = w0_ref[0] + _dot(jnp.tanh(xw).astype(BF16), w2_ref[0])
    ld = -EXP_NEG_HALF * jax.nn.sigmoid(z)
    ld_o[0] = ld
    chunk_ltri = jnp.where((row // CHUNK == col // CHUNK) & (col <= row), 1.0, 0.0)
    cs_o[0] = _dot(chunk_ltri, ld, "r3")
    a = jax.nn.sigmoid(a0_ref[0] + _dot(xa.astype(BF16), a2_ref[0]))
    g_o[0] = _dot(jax.nn.sigmoid(xg).astype(BF16), g2_ref[0]).astype(g_o.dtype)
    if not first_layer:
        lo = _dot(v.astype(BF16), v1_ref[0])
        gate = jax.nn.sigmoid(v0_ref[0] + _dot(lo.astype(BF16), v2_ref[0]))
        v = v + (vf_ref[0].astype(F32) - v) * gate
    r_o[0] = r.astype(r_o.dtype)
    v_o[0] = v.astype(v_o.dtype)
    a_o[0] = a.astype(a_o.dtype)
    kk_o[0] = (k * kk_ref[0]).astype(kk_o.dtype)
    k_o[0] = (k * (1.0 + (a - 1.0) * ka_ref[0])).astype(k_o.dtype)


def _rwkv_prep_call(u_rw, p, layer, v_first, tt):
    B, T, _ = u_rw.shape
    first_layer = v_first is None
    tile = pl.BlockSpec((1, tt, D_RWKV), lambda b, t: (b, t, 0))
    names = ["mu", "w0", "w2", "a0", "a2", "g2", "k_k", "k_a"]
    in_specs = [
        pl.BlockSpec((1, tt, RW_COLS), lambda b, t: (b, t, 0)),
        pl.BlockSpec((1, 16, RW_COLS), lambda b, t: (b, jnp.maximum(t * (tt // 16) - 1, 0), 0)),
    ] + [_layer_spec(p[nm], layer) for nm in names]
    args = [u_rw, u_rw] + [p[nm] for nm in names]
    if not first_layer:
        in_specs += [tile] + [_layer_spec(p[nm], layer - 1) for nm in ("v0", "v1", "v2")]
        args += [v_first, p["v0"], p["v1"], p["v2"]]
    out = lambda dt: jax.ShapeDtypeStruct((B, T, D_RWKV), dt)
    return pl.pallas_call(
        functools.partial(_rwkv_prep_kernel, tt=tt, first_layer=first_layer),
        out_shape=[out(BF16), out(F32), out(F32)] + [out(BF16)] * 5,
        grid=(B, T // tt),
        in_specs=in_specs,
        out_specs=[tile] * 8,
        compiler_params=_cparams(("parallel", "parallel")),
        name="rwkv_prep",
    )(*args)


def _rwkv_scan_kernel(r_ref, ld_ref, cs_ref, k_ref, v_ref, kk_ref, a_ref, g_ref, rk_ref, lg_ref, lb_ref,
                      w1_ref, w2_ref, o_ref, w1b_ref, w2b_ref, s_scr, wr_scr, o0_scr, pm_scr, qm_scr, bn_scr, g_scr,
                      *, ts, nt, hpb):
    C = CHUNK
    nc = ts // C
    nb = hpb * nc
    N = HEAD_DIM
    n = pl.program_id(0)

    w1b_ref[...] = w1_ref[0].astype(BF16)
    w2b_ref[...] = w2_ref[0].astype(BF16)

    @pl.when(n == 0)
    def _():
        for scr in (s_scr, wr_scr, o0_scr, pm_scr, qm_scr, bn_scr, g_scr):
            scr[...] = jnp.zeros_like(scr)

    first = lax.rem(jnp.maximum(n - 1, 0), nt) == 0
    outs = []
    for hh in range(hpb):
        S = jnp.where(first, 0.0, s_scr[hh])
        for c in range(nc):
            i = hh * nc + c
            Sb = S.astype(BF16)
            outs.append(_dot_nt(wr_scr[i], Sb) + o0_scr[i])
            S = _dot(Sb, pm_scr[i]) + qm_scr[i]
        s_scr[hh] = S
    o = jnp.stack(outs, axis=0)

    def split_row(ref):
        x = ref[0]
        return jnp.concatenate([jnp.broadcast_to(x[None, :, h * N:(h + 1) * N], (nc, 1, N)) for h in range(hpb)],
                               axis=0)

    mu = jnp.mean(o, axis=-1, keepdims=True)
    oc = o - mu
    var = jnp.mean(oc * oc, axis=-1, keepdims=True)
    on = oc * lax.rsqrt(var + GN_EPS) * split_row(lg_ref) + split_row(lb_ref)
    res = (on + bn_scr[...]) * g_scr[...]
    o_ref[0] = jnp.concatenate([res[h * nc:(h + 1) * nc].reshape(ts, N) for h in range(hpb)],
                               axis=-1).astype(o_ref.dtype)

    def split(ref):
        x = ref[0].astype(F32).reshape(nc, C, hpb * N)
        return jnp.concatenate([x[:, :, h * N:(h + 1) * N] for h in range(hpb)], axis=0)

    r, ld, cs, k, v, kkr, a, g = (split(z) for z in (r_ref, ld_ref, cs_ref, k_ref, v_ref, kk_ref, a_ref, g_ref))
    kkn = kkr / jnp.maximum(jnp.sqrt(jnp.sum(kkr * kkr, axis=-1, keepdims=True)), 1e-12)
    row = lax.broadcasted_iota(jnp.int32, (C, C), 0)
    col = lax.broadcasted_iota(jnp.int32, (C, C), 1)
    incl = col <= row
    strict = col < row
    eye = col == row
    cs_last = cs[:, C - 1:C, :]
    e_in = jnp.exp(cs)
    e_ex = jnp.exp(cs - ld)
    e_neg = jnp.exp(-cs)
    e_hat = jnp.exp(cs_last - cs)
    at = -kkn * e_ex
    b = kkn * a
    bt = b * e_neg
    kt = k * e_neg
    rt = r * e_in
    bh = (b * e_hat).astype(BF16)
    kh = (k * e_hat).astype(BF16)
    vb = v.astype(BF16)
    A = _bmm_nt(jnp.concatenate([at, rt], axis=1).astype(BF16), jnp.concatenate([bt, kt], axis=1).astype(BF16))
    a_ab = jnp.where(strict, A[:, :C, :C], 0.0)
    a_ak = jnp.where(strict, A[:, :C, C:], 0.0).astype(BF16)
    a_rb = jnp.where(incl, A[:, C:, :C], 0.0).astype(BF16)
    a_rk = jnp.where(incl, A[:, C:, C:], 0.0).astype(BF16)
    npow = a_ab.astype(BF16)
    tinv = jnp.where(eye, 1.0, 0.0).astype(F32) + a_ab
    p2 = 2
    while p2 < C:
        npow_f = _bmm(npow, npow)
        npow = npow_f.astype(BF16)
        tinv = tinv + _bmm(npow, tinv.astype(BF16))
        p2 *= 2
    akv = _bmm(a_ak, vb)
    x = _bmm(tinv.astype(BF16), jnp.concatenate([at, akv], axis=-1).astype(BF16))
    xb = x.astype(BF16)
    y = _bmm(a_rb, xb)
    pq = _bmm_tn(xb, bh)
    wr_scr[...] = (rt + y[:, :, :N]).astype(BF16)
    o0_scr[...] = y[:, :, N:] + _bmm(a_rk, vb)
    pm_scr[...] = (pq[:, :N, :] + jnp.where(eye, jnp.exp(cs_last), 0.0)).astype(BF16)
    qm_scr[...] = pq[:, N:, :] + _bmm_tn(vb, kh)
    bn_scr[...] = jnp.sum(r * k * split_row(rk_ref), axis=-1, keepdims=True) * v
    g_scr[...] = g


def _rwkv_scan_call(r, ld, cs, k, v, kk, a, g, p, w1, w2, layer, ts, hpb):
    B, T, _ = r.shape
    nt = T // ts
    hg = RWKV_HEADS // hpb
    nblk = B * hg * nt
    nb = hpb * (ts // CHUNK)
    w = hpb * HEAD_DIM

    def blk(n):
        return n // (hg * nt), lax.rem(n, nt), lax.rem(n // nt, hg)

    cur = lambda n: blk(jnp.minimum(n, nblk - 1))
    prev = lambda n: blk(jnp.maximum(n - 1, 0))
    tile = pl.BlockSpec((1, ts, w), cur)
    rowp = lambda f: pl.BlockSpec((1, 1, w), lambda n: (layer, 0, f(n)[2]))
    sq = lambda dt: pltpu.VMEM((nb, HEAD_DIM, HEAD_DIM), dt)
    _, d_in, d_ff = w1.shape
    rows1, rows2 = d_in // nblk, d_ff // nblk
    assert rows1 * nblk == d_in and rows2 * nblk == d_ff and rows1 % 16 == 0 and rows2 % 16 == 0
    slab = lambda n: jnp.minimum(n, nblk - 1)
    return pl.pallas_call(
        functools.partial(_rwkv_scan_kernel, ts=ts, nt=nt, hpb=hpb),
        out_shape=[jax.ShapeDtypeStruct((B, T, D_RWKV), BF16),
                   jax.ShapeDtypeStruct((d_in, d_ff), BF16), jax.ShapeDtypeStruct((d_ff, d_in), BF16)],
        grid=(nblk + 1,),
        in_specs=[tile] * 8 + [rowp(cur), rowp(prev), rowp(prev),
                               pl.BlockSpec((1, rows1, d_ff), lambda n: (layer, slab(n), 0)),
                               pl.BlockSpec((1, rows2, d_in), lambda n: (layer, slab(n), 0))],
        out_specs=[pl.BlockSpec((1, ts, w), prev), pl.BlockSpec((rows1, d_ff), lambda n: (slab(n), 0)),
                   pl.BlockSpec((rows2, d_in), lambda n: (slab(n), 0))],
        scratch_shapes=[pltpu.VMEM((hpb, HEAD_DIM, HEAD_DIM), F32), pltpu.VMEM((nb, CHUNK, HEAD_DIM), BF16),
                        pltpu.VMEM((nb, CHUNK, HEAD_DIM), F32), sq(BF16), sq(F32),
                        pltpu.VMEM((nb, CHUNK, HEAD_DIM), F32), pltpu.VMEM((nb, CHUNK, HEAD_DIM), F32)],
        compiler_params=_cparams(("arbitrary",)),
        name="rwkv_scan",
    )(r, ld, cs, k, v, kk, a, g, p["r_k"], p["lnx_g"], p["lnx_b"], w1, w2)


def _nsa_prep_kernel(u_ref, cos_ref, sin_ref, q_o, kc_o, vc_o, ks_o, vs_o, kw_o, vw_o, gate_o):
    cos = cos_ref[...]
    sin = sin_ref[...]
    lane = lax.broadcasted_iota(jnp.int32, cos.shape, 1)
    first_half = (lane % HEAD_DIM) < (HEAD_DIM // 2)

    def rope(x):
        other = jnp.where(first_half, pltpu.roll(x, LANES - HEAD_DIM // 2, 1), pltpu.roll(x, HEAD_DIM // 2, 1))
        return x * cos + other * sin

    def put(out_ref, col0, nheads, roped, scale=None):
        for j in range(nheads // 2):
            x = u_ref[0, :, col0 + j * LANES:col0 + (j + 1) * LANES].astype(F32)
            if roped:
                x = rope(x)
            if scale is not None:
                x = x * scale
            out_ref[0, 2 * j] = x[:, :HEAD_DIM].astype(out_ref.dtype)
            out_ref[0, 2 * j + 1] = x[:, HEAD_DIM:].astype(out_ref.dtype)

    low = lane < HEAD_DIM

    def put_wide(out_ref, col0, nheads, scale, tail):
        for j in range(nheads // 2):
            x = rope(u_ref[0, :, col0 + j * LANES:col0 + (j + 1) * LANES].astype(F32))
            if scale is not None:
                x = x * scale
            out_ref[0, 2 * j] = jnp.where(low, x, tail).astype(out_ref.dtype)
            out_ref[0, 2 * j + 1] = jnp.where(low, pltpu.roll(x, HEAD_DIM, 1), tail).astype(out_ref.dtype)

    def put_t(out_ref, col0, nheads):
        extra = out_ref.shape[2] - HEAD_DIM
        ones_row = jnp.where(lax.broadcasted_iota(jnp.int32, (extra, cos.shape[0]), 0) == 0, 1.0, 0.0)
        for j in range(nheads // 2):
            xt = u_ref[0, :, col0 + j * LANES:col0 + (j + 1) * LANES].astype(F32).T
            for i, part in enumerate((xt[:HEAD_DIM, :], xt[HEAD_DIM:, :])):
                out_ref[0, 2 * j + i, :HEAD_DIM, :] = part.astype(out_ref.dtype)
                out_ref[0, 2 * j + i, HEAD_DIM:, :] = ones_row.astype(out_ref.dtype)

    pos = pl.program_id(1) * cos.shape[0] + lax.broadcasted_iota(jnp.int32, cos.shape, 0)
    blk_onehot = jnp.where(lane - HEAD_DIM == pos // SEL_BLOCK, 1.0, 0.0)
    put_wide(q_o, 0, NSA_HEADS, HEAD_DIM ** -0.5 * LOG2_E, 0.0)
    put(kc_o, NS_KC, NSA_KV_HEADS, True)
    put(vc_o, NS_KC + NSA_KV, NSA_KV_HEADS, False)
    put_wide(ks_o, NS_KC + 2 * NSA_KV, NSA_KV_HEADS, None, blk_onehot)
    put_t(vs_o, NS_KC + 3 * NSA_KV, NSA_KV_HEADS)
    put_wide(kw_o, NS_KC + 4 * NSA_KV, NSA_KV_HEADS, None, 0.0)
    put_t(vw_o, NS_KC + 5 * NSA_KV, NSA_KV_HEADS)
    gates_t = jax.nn.sigmoid(u_ref[0, :, NS_GATE:NS_GATE + LANES].astype(F32)).T
    for gi in range(NSA_KV_HEADS):
        gate_o[0, gi] = gates_t[gi * GATE_PAD:(gi + 1) * GATE_PAD, :]


def _nsa_prep_call(u_ns, cos_t, sin_t, tt):
    B, T, _ = u_ns.shape
    G, H, N = NSA_KV_HEADS, NSA_HEADS, HEAD_DIM
    assert T // SEL_BLOCK <= LANES - N
    kv = jax.ShapeDtypeStruct((B, G, T, N), F32)
    kv_wide = jax.ShapeDtypeStruct((B, G, T, LANES), BF16)
    kv_t = jax.ShapeDtypeStruct((B, G, VT_ROWS, T), BF16)
    kv_spec = pl.BlockSpec((1, G, tt, N), lambda b, t: (b, 0, t, 0))
    kvw_spec = pl.BlockSpec((1, G, tt, LANES), lambda b, t: (b, 0, t, 0))
    kvt_spec = pl.BlockSpec((1, G, VT_ROWS, tt), lambda b, t: (b, 0, 0, t))
    return pl.pallas_call(
        _nsa_prep_kernel,
        out_shape=[jax.ShapeDtypeStruct((B, H, T, LANES), BF16), kv, kv, kv_wide, kv_t,
                   kv_wide, kv_t, jax.ShapeDtypeStruct((B, G, GATE_PAD, T), F32)],
        grid=(B, T // tt),
        in_specs=[
            pl.BlockSpec((1, tt, NS_COLS), lambda b, t: (b, t, 0)),
            pl.BlockSpec((tt, LANES), lambda b, t: (t, 0)),
            pl.BlockSpec((tt, LANES), lambda b, t: (t, 0)),
        ],
        out_specs=[pl.BlockSpec((1, H, tt, LANES), lambda b, t: (b, 0, t, 0)), kv_spec, kv_spec, kvw_spec, kvt_spec,
                   kvw_spec, kvt_spec, pl.BlockSpec((1, G, GATE_PAD, tt), lambda b, t: (b, 0, 0, t))],
        compiler_params=_cparams(("parallel", "parallel")),
        name="nsa_prep",
    )(u_ns, cos_t, sin_t)


def _compress_kernel(x_ref, pos_ref, w1_ref, w2_ref, o_ref):
    x = x_ref[0, 0]
    half = x.shape[1]
    y_top = _dot(x + pos_ref[0, 0:1, :], w1_ref[0, :half, :], "x3")
    y_bot = _dot(x + pos_ref[0, 1:2, :], w1_ref[0, half:, :], "x3")
    pre = y_top + pltpu.roll(y_bot, x.shape[0] - 1, 0)
    o_ref[0, 0] = _dot(jax.nn.gelu(pre), w2_ref[0], "x3")


def _compress_call(kv, pos, w1, w2, which):
    B, G, T, N = kv.shape
    n_half = T // CMP_STRIDE
    width = CMP_STRIDE * N
    hidden = w1.shape[-1]
    x = kv.reshape(B, G, n_half, width)
    return pl.pallas_call(
        _compress_kernel,
        out_shape=jax.ShapeDtypeStruct((B, G, n_half, w2.shape[-1]), F32),
        grid=(B, G),
        in_specs=[
            pl.BlockSpec((1, 1, n_half, width), lambda b, g: (b, g, 0, 0)),
            pl.BlockSpec((1, 2, width), lambda b, g: (which, 0, 0)),
            pl.BlockSpec((1, 2 * width, hidden), lambda b, g: (which, 0, 0)),
            pl.BlockSpec((1, hidden, w2.shape[-1]), lambda b, g: (which, 0, 0)),
        ],
        out_specs=pl.BlockSpec((1, 1, n_half, w2.shape[-1]), lambda b, g: (b, g, 0, 0)),
        compiler_params=_cparams(("parallel", "parallel")),
        name="nsa_compress",
    )(x, pos, w1, w2)


def _nsa_attn_kernel(q_ref, kc_ref, vc_ref, ks_ref, vst_ref, kw_ref, vwt_ref, gate_ref, ov_ref, put_ref,
                     o_ref, s0_scr, s1_scr, *, tq, tk):
    HPG, N = HEADS_PER_GROUP, HEAD_DIM
    qi = pl.program_id(2)
    q0 = qi * tq
    k_len = ks_ref.shape[2]

    wq = HPG * tq
    q_win = q_ref[0].reshape(wq, LANES)
    rel = lax.broadcasted_iota(jnp.int32, (tk, tq), 1) - lax.broadcasted_iota(jnp.int32, (tk, tq), 0)
    init = (jnp.full((1, wq), NEG, F32), jnp.zeros((VT_ROWS, wq), F32))
    n_tiles = k_len // tk

    def scores(q_all, k_ref, k0, dst):
        dst[...] = _dot_nt(k_ref[0, 0, pl.ds(k0, tk), :], q_all)

    def update(carry, src, vt_ref, k0, bias):
        m_i, acc = carry
        s = src[...]
        if bias is not None:
            s = s + (bias if bias.ndim == 0 else jnp.concatenate([bias] * HPG, axis=1))
        m_n = jnp.maximum(m_i, jnp.max(s, axis=0, keepdims=True))
        p = jnp.exp2(s - m_n)
        return m_n, jnp.exp2(m_i - m_n) * acc + _dot(vt_ref[0, 0, :, pl.ds(k0, tk)], p.astype(BF16))

    def split(o):
        return [o[:, h * tq:(h + 1) * tq] for h in range(HPG)]

    def finish(carry):
        return split(carry[1][:N] / carry[1][N:N + 1])

    def window_tile(e):
        koff = e * tk - WINDOW
        lo, hi = -(tk - 1) - koff, (tq - 1) - koff
        bias = None
        if lo < 0 or hi >= WINDOW:
            dist = rel - koff
            bias = jnp.where((dist >= 0) & (dist < WINDOW), 0.0, NEG)
        if koff < 0:
            bias = jnp.where(q0 + koff >= 0, 0.0 if bias is None else bias, NEG)
        return pl.multiple_of(jnp.maximum(q0 + koff, 0), tk), bias

    win = [window_tile(e) for e in range((WINDOW + tq) // tk)]
    assert len(win) in (2, 3)
    bufs = (s0_scr, s1_scr)
    scores(q_win, kw_ref, win[0][0], bufs[0])

    n_cmp = kc_ref.shape[2]
    c_c = lax.broadcasted_iota(jnp.int32, (n_cmp, tq), 0)
    t_c = q0 + lax.broadcasted_iota(jnp.int32, (n_cmp, tq), 1)
    cmask = jnp.concatenate([c_c * CMP_STRIDE + (CMP_BLOCK - 1) <= t_c] * HPG, axis=1)
    s = jnp.where(cmask, _dot_nt(kc_ref[0, 0].astype(BF16), q_win), NEG)
    scores(q_win, kw_ref, win[1][0], bufs[1])
    m = jnp.max(s, axis=0, keepdims=True)
    e = jnp.where(cmask, jnp.exp2(s - m), 0.0)
    l = jnp.sum(e, axis=0, keepdims=True)
    p = e / jnp.maximum(l, 1e-30)
    o_cmp = split(_dot(vc_ref[0, 0].T[:N].astype(BF16), p.astype(BF16)))
    psum = p[:, :tq]
    for h in range(1, HPG):
        psum = psum + p[:, h * tq:(h + 1) * tq]

    n_sel = ov_ref.shape[0]
    imp = _dot(ov_ref[...], psum, "r3")
    carry = update(init, bufs[0], vwt_ref, *win[0])
    if len(win) == 3:
        scores(q_win, kw_ref, win[2][0], bufs[0])
    blk = lax.broadcasted_iota(jnp.int32, (n_sel, tq), 0)
    t_s = q0 + lax.broadcasted_iota(jnp.int32, (n_sel, tq), 1)
    cur = t_s // SEL_BLOCK
    forced = (blk == 0) | (blk == cur) | (blk == cur - 1)
    val = jnp.where(forced, FORCE, jnp.where(blk * SEL_BLOCK <= t_s, imp, -1.0))
    rank = jnp.zeros((n_sel, tq), F32)
    for i in range(n_sel):
        vi = val[i:i + 1, :]
        ahead = (vi > val) | ((vi == val) & (blk > i))
        rank = rank + jnp.where(ahead, 1.0, 0.0)
    unsel_t = jnp.where(rank < float(min(N_SELECT, n_sel)), 0.0, NEG)
    q_bias = lax.dot_general(unsel_t.astype(BF16), put_ref[...], (((0,), (0,)), ((), ())),
                             preferred_element_type=F32)
    carry = update(carry, bufs[1], vwt_ref, *win[1])
    q_sel = jnp.concatenate([(q_ref[0, h].astype(F32) + q_bias).astype(BF16) for h in range(HPG)], axis=0)
    o_win = finish(update(carry, bufs[0], vwt_ref, *win[2]) if len(win) == 3 else carry)

    def key0(j):
        return pl.multiple_of(j * tk, tk)

    scores(q_sel, ks_ref, key0(0), s0_scr)

    def pair(jj, carry):
        scores(q_sel, ks_ref, key0(2 * jj + 1), s1_scr)
        carry = update(carry, s0_scr, vst_ref, key0(2 * jj), None)
        scores(q_sel, ks_ref, key0(2 * jj + 2), s0_scr)
        return update(carry, s1_scr, vst_ref, key0(2 * jj + 1), None)

    carry = lax.fori_loop(0, qi // 2, pair, init)
    t1 = 2 * (qi // 2)
    t2 = jnp.minimum(t1 + 1, n_tiles - 1)

    def causal(t):
        return jnp.where(rel >= (t - qi) * tk, 0.0, NEG)

    scores(q_sel, ks_ref, key0(t2), s1_scr)
    carry = update(carry, s0_scr, vst_ref, key0(t1), causal(t1))
    o_slc = finish(update(carry, s1_scr, vst_ref, key0(t2), causal(t1 + 1)))

    gate = gate_ref[0, 0]
    outs = [gate[3 * h:3 * h + 1, :] * o_cmp[h] + gate[3 * h + 1:3 * h + 2, :] * o_slc[h]
            + gate[3 * h + 2:3 * h + 3, :] * o_win[h] for h in range(HPG)]
    o_ref[0] = jnp.concatenate(outs, axis=0).T.astype(o_ref.dtype)


def _nsa_attn_call(q, kc, vc, ks, vs_t, kw, vw_t, gates_t, ov_t, put, tq, tk):
    B, H, T, _ = q.shape
    N = HEAD_DIM
    G, HPG = NSA_KV_HEADS, HEADS_PER_GROUP
    assert WINDOW % tk == 0 and tq == tk
    n_half = kc.shape[2]
    cmp_spec = pl.BlockSpec((1, 1, n_half, LANES), lambda b, g, t: (b, g, 0, 0))
    kv_spec = pl.BlockSpec((1, 1, T, LANES), lambda b, g, t: (b, g, 0, 0))
    kvt_spec = pl.BlockSpec((1, 1, VT_ROWS, T), lambda b, g, t: (b, g, 0, 0))
    return pl.pallas_call(
        functools.partial(_nsa_attn_kernel, tq=tq, tk=tk),
        out_shape=jax.ShapeDtypeStruct((B, T, D_NSA), BF16),
        grid=(B, G, T // tq),
        in_specs=[
            pl.BlockSpec((1, HPG, tq, LANES), lambda b, g, t: (b, g, t, 0)),
            cmp_spec, cmp_spec, kv_spec, kvt_spec, kv_spec, kvt_spec,
            pl.BlockSpec((1, 1, GATE_PAD, tq), lambda b, g, t: (b, g, 0, t)),
            pl.BlockSpec(ov_t.shape, lambda b, g, t: (0, 0)),
            pl.BlockSpec(put.shape, lambda b, g, t: (0, 0)),
        ],
        out_specs=pl.BlockSpec((1, tq, HPG * N), lambda b, g, t: (b, t, g)),
        scratch_shapes=[pltpu.VMEM((tk, HPG * tq), F32), pltpu.VMEM((tk, HPG * tq), F32)],
        compiler_params=_cparams(("parallel", "parallel", "arbitrary")),
        name="nsa_attn",
    )(q, kc, vc, ks, vs_t, kw, vw_t, gates_t, ov_t, put)


def _outproj_kernel(orw_ref, ons_ref, x_ref, gt_ref, w_ref, g_ref, b_ref, o_ref, *, alpha):
    half = orw_ref.shape[-1]
    y = _dot(orw_ref[0].astype(BF16), w_ref[0, :half, :]) + _dot(ons_ref[0].astype(BF16), w_ref[0, half:, :])
    z = alpha * x_ref[0] + (1.0 + gt_ref[0]) * y
    o_ref[0] = _layer_norm_rows(z, g_ref[0], b_ref[0])


def _outproj_call(o_rw, o_ns, x, mod3, row0, w_bf16, ln_g, ln_b, layer, alpha, tm):
    B, T, D = x.shape
    half = o_rw.shape[-1]
    return pl.pallas_call(
        functools.partial(_outproj_kernel, alpha=alpha),
        out_shape=jax.ShapeDtypeStruct((B, T, D), F32),
        grid=(B, T // tm),
        in_specs=[
            pl.BlockSpec((1, tm, half), lambda b, m: (b, m, 0)),
            pl.BlockSpec((1, tm, half), lambda b, m: (b, m, 0)),
            pl.BlockSpec((1, tm, D), lambda b, m: (b, m, 0)),
            pl.BlockSpec((1, 1, D), lambda b, m: (row0 + 6 * b + 2, 0, 0)),
            _layer_spec(w_bf16, layer), _layer_spec(ln_g, layer), _layer_spec(ln_b, layer),
        ],
        out_specs=pl.BlockSpec((1, tm, D), lambda b, m: (b, m, 0)),
        compiler_params=_cparams(("parallel", "parallel")),
        name="outproj_ln",
    )(o_rw, o_ns, x, mod3, w_bf16, ln_g, ln_b)


def _mlp_kernel(x_ref, sc_ref, sh_ref, gt_ref, w1_ref, w2_ref, g_ref, b_ref, o_ref, h_scr, acc_scr, *, alpha):
    f = pl.program_id(2)

    @pl.when(f == 0)
    def _():
        h = x_ref[0] * (1.0 + sc_ref[0]) + sh_ref[0]
        h_scr[...] = h.astype(BF16)
        acc_scr[...] = jnp.zeros_like(acc_scr)

    a = jnp.maximum(_dot(h_scr[...], w1_ref[...]), 0.0)
    acc_scr[...] += _dot((a * a).astype(BF16), w2_ref[...])

    @pl.when(f == pl.num_programs(2) - 1)
    def _():
        z = alpha * x_ref[0] + (1.0 + gt_ref[0]) * acc_scr[...]
        o_ref[0] = _layer_norm_rows(z, g_ref[0], b_ref[0])


def _mlp_call(x, mod3, row0, w1_bf16, w2_bf16, ln_g, ln_b, layer, alpha, tm, tf):
    B, T, D = x.shape
    FF = w1_bf16.shape[1]
    modspec = lambda j: pl.BlockSpec((1, 1, D), lambda b, m, f: (row0 + 6 * b + j, 0, 0))
    return pl.pallas_call(
        functools.partial(_mlp_kernel, alpha=alpha),
        out_shape=jax.ShapeDtypeStruct((B, T, D), F32),
        grid=(B, T // tm, FF // tf),
        in_specs=[
            pl.BlockSpec((1, tm, D), lambda b, m, f: (b, m, 0)),
            modspec(4), modspec(3), modspec(5),
            pl.BlockSpec((D, tf), lambda b, m, f: (0, f)),
            pl.BlockSpec((tf, D), lambda b, m, f: (f, 0)),
            _layer_spec(ln_g, layer), _layer_spec(ln_b, layer),
        ],
        out_specs=pl.BlockSpec((1, tm, D), lambda b, m, f: (b, m, 0)),
        scratch_shapes=[pltpu.VMEM((tm, D), BF16), pltpu.VMEM((tm, D), F32)],
        compiler_params=_cparams(("parallel", "parallel", "arbitrary")),
        name="mlp_ln",
    )(x, mod3, mod3, mod3, w1_bf16, w2_bf16, ln_g, ln_b)


def _pad_last(w, n):
    return jnp.pad(w, [(0, 0)] * (w.ndim - 1) + [(0, n - w.shape[-1])])


def _pad_rows(w, n):
    return jnp.pad(w, [(0, 0)] * (w.ndim - 2) + [(0, n - w.shape[-2]), (0, 0)])


def _split_w_in(w_in):
    c = np.cumsum([0, D_RWKV, D_RWKV, D_RWKV, DECAY_LORA, AAA_LORA, GATE_LORA]).tolist()
    w_in = w_in.astype(BF16)
    rw = jnp.concatenate([w_in[..., c[0]:c[3]], _pad_last(w_in[..., c[3]:c[4]], LORA_PAD),
                          _pad_last(w_in[..., c[4]:c[5]], LORA_PAD), w_in[..., c[5]:c[6]]], axis=-1)
    ng = 3 * HEADS_PER_GROUP
    lead = w_in.shape[:-1]
    gates = _pad_last(w_in[..., c[6] + NS_GATE:].reshape(lead + (NSA_KV_HEADS, ng)), GATE_PAD)
    gates = _pad_last(gates.reshape(lead + (NSA_KV_HEADS * GATE_PAD,)), LANES)
    ns = jnp.concatenate([w_in[..., c[6]:c[6] + NS_GATE], gates], axis=-1)
    return jnp.concatenate([rw, ns], axis=-1)


def _pad_mu(mu):
    c = np.cumsum([0, 3 * D_RWKV, DECAY_LORA, AAA_LORA, GATE_LORA]).tolist()
    parts = [mu[:, c[0]:c[1]], _pad_last(mu[:, c[1]:c[2]], LORA_PAD), _pad_last(mu[:, c[2]:c[3]], LORA_PAD),
             mu[:, c[3]:c[4]]]
    return jnp.concatenate(parts, axis=-1)[:, None, :]


def _rope_tables(T):
    half = HEAD_DIM // 2
    inv = ROPE_THETA ** (-jnp.arange(half, dtype=F32) / half)
    ang = jnp.arange(T, dtype=F32)[:, None] * inv[None]
    cos, sin = jnp.cos(ang), jnp.sin(ang)
    cos_t = jnp.tile(cos, (1, LANES // half))
    sin_t = jnp.tile(jnp.concatenate([-sin, sin], axis=1), (1, LANES // HEAD_DIM))
    return cos_t, sin_t


def _selection_constants(T):
    n_half = T // CMP_STRIDE
    n_cmp = (T - CMP_BLOCK) // CMP_STRIDE + 1
    n_sel = T // SEL_BLOCK
    pos = np.arange(n_cmp)[:, None] * CMP_STRIDE + np.arange(CMP_BLOCK)[None]
    ov = ((pos // SEL_BLOCK)[..., None] == np.arange(n_sel)).sum(1) / CMP_BLOCK
    ov_t = np.zeros((n_sel, n_half), np.float32)
    ov_t[:, :n_cmp] = ov.T
    put = (np.arange(LANES)[None, :] == HEAD_DIM + np.arange(n_sel)[:, None]).astype(np.float32)
    return jnp.asarray(ov_t), jnp.asarray(put, BF16)


def kernel(x, c, w_ada, b_ada, w_in, rwkv_mu, rwkv_w0, rwkv_w2, rwkv_a0, rwkv_a2, rwkv_g2, rwkv_k_k, rwkv_k_a, rwkv_r_k, rwkv_lnx_g, rwkv_lnx_b, rwkv_v0, rwkv_v1, rwkv_v2, nsa_cmp_pos, nsa_cmp_w1, nsa_cmp_w2, w_out, ln1_g, ln1_b, mlp_w1, mlp_w2, ln2_g, ln2_b):
    B, T, D = x.shape
    L = w_ada.shape[0]
    alpha = (2 * L) ** 0.25

    c_pad = jnp.pad(c, ((0, -B % 8), (0, 0)))
    mod = _ada_call(c_pad, w_ada, b_ada)[:, :B]
    mod3 = mod.reshape(L * B * 6, 1, D)
    cos_t, sin_t = _rope_tables(T)
    ov_t, sel_put = _selection_constants(T)

    rows = lambda z: z.reshape(z.shape[0], 1, -1)
    w_in_b = _split_w_in(w_in)
    p = {
        "mu": _pad_mu(rwkv_mu), "w0": rows(rwkv_w0), "a0": rows(rwkv_a0),
        "w2": _pad_rows(rwkv_w2, LORA_PAD).astype(BF16), "a2": _pad_rows(rwkv_a2, LORA_PAD).astype(BF16),
        "g2": rwkv_g2.astype(BF16), "k_k": rows(rwkv_k_k), "k_a": rows(rwkv_k_a),
        "r_k": rows(rwkv_r_k), "lnx_g": rows(rwkv_lnx_g), "lnx_b": rows(rwkv_lnx_b),
        "v0": rows(rwkv_v0), "v1": _pad_last(rwkv_v1, LORA_PAD).astype(BF16),
        "v2": _pad_rows(rwkv_v2, LORA_PAD).astype(BF16),
    }
    half_block = CMP_STRIDE * HEAD_DIM
    cmp_pos = nsa_cmp_pos.reshape(2 * L, 2, half_block)
    cmp_w1 = nsa_cmp_w1.reshape((2 * L,) + nsa_cmp_w1.shape[2:])
    cmp_w2 = _pad_last(nsa_cmp_w2.reshape((2 * L,) + nsa_cmp_w2.shape[2:]), LANES)
    w_out_b = w_out.astype(BF16)
    ln1 = (rows(ln1_g), rows(ln1_b))
    ln2 = (rows(ln2_g), rows(ln2_b))

    v_first = None
    for i in range(L):
        row0 = i * B * 6
        u_rw, u_ns = _modmm_call(x, mod3, row0, w_in_b, i, RW_COLS, TILES["proj_m"], PROJ_TN)

        r, ld, cs, k, v, kk, a, g = _rwkv_prep_call(u_rw, p, i, v_first, TILES["prep"])
        if i == 0:
            v_first = v
        o_rw, w1_b, w2_b = _rwkv_scan_call(r, ld, cs, k, v, kk, a, g, p, mlp_w1, mlp_w2, i, TILES["scan_rows"],
                                           TILES["scan_heads"])

        q, kc_in, vc_in, ks, vs_t, kw, vw_t, gates_t = _nsa_prep_call(u_ns, cos_t, sin_t, TILES["prep"])
        kc = _compress_call(kc_in, cmp_pos, cmp_w1, cmp_w2, 2 * i)
        vc = _compress_call(vc_in, cmp_pos, cmp_w1, cmp_w2, 2 * i + 1)
        o_ns = _nsa_attn_call(q, kc, vc, ks, vs_t, kw, vw_t, gates_t, ov_t, sel_put, TILES["attn_q"], TILES["attn_k"])

        x = _outproj_call(o_rw, o_ns, x, mod3, row0, w_out_b, *ln1, i, alpha, TILES["out_m"])
        x = _mlp_call(x, mod3, row0, w1_b, w2_b, *ln2, i, alpha, TILES["mlp_m"], TILES["mlp_f"])
    return x
```

```python
import functools

import numpy as np
import jax
import jax.numpy as jnp
from jax import lax
from jax.experimental import pallas as pl
from jax.experimental.pallas import tpu as pltpu

F32 = jnp.float32
BF16 = jnp.bfloat16

HEAD_DIM = 64
RWKV_HEADS = 16
NSA_HEADS = 16
NSA_KV_HEADS = 4
HEADS_PER_GROUP = NSA_HEADS // NSA_KV_HEADS
D_RWKV = RWKV_HEADS * HEAD_DIM
D_NSA = NSA_HEADS * HEAD_DIM
NSA_KV = NSA_KV_HEADS * HEAD_DIM
DECAY_LORA = 96
AAA_LORA = 96
GATE_LORA = 256
GN_EPS = 64e-5
CMP_BLOCK = 32
CMP_STRIDE = 16
SEL_BLOCK = 64
N_SELECT = 8
WINDOW = 512
ROPE_THETA = 10000.0
NEG = -1e30
FORCE = 1e4
LN_EPS = 1e-5
LOG2_E = 1.4426950408889634
EXP_NEG_HALF = 0.6065306597126334

LANES = 128
LORA_PAD = LANES
RW_XW = 3 * D_RWKV
RW_XA = RW_XW + LORA_PAD
RW_XG = RW_XA + LORA_PAD
RW_COLS = RW_XG + GATE_LORA
NS_KC = D_NSA
NS_GATE = D_NSA + 6 * NSA_KV
NS_COLS = NS_GATE + LANES
GATE_PAD = 16
VT_ROWS = HEAD_DIM + 16
PROJ_TN = 896

CHUNK = 64
V7X_VMEM_BYTES = 64 * 1024 * 1024
VMEM_LIMIT = V7X_VMEM_BYTES - 8 * 1024 * 1024
TILES = dict(proj_m=1024, prep=256, scan_rows=128, scan_heads=8, attn_q=256, attn_k=256, out_m=512, mlp_m=512,
             mlp_f=1024)


def _layer_spec(arr, layer):
    tail = (0,) * (arr.ndim - 1)
    return pl.BlockSpec((1,) + arr.shape[1:], lambda *_: (layer,) + tail)


def _cparams(sem):
    return pltpu.CompilerParams(dimension_semantics=sem, vmem_limit_bytes=VMEM_LIMIT)


def _mm(fn, a, b, mode):
    if mode is None:
        return fn(a, b)
    ah, bh = a.astype(BF16), b.astype(BF16)
    if mode == "r3":
        r1 = b - bh.astype(F32)
        bm = r1.astype(BF16)
        bl = (r1 - bm.astype(F32)).astype(BF16)
        return fn(ah, bh) + (fn(ah, bm) + fn(ah, bl))
    assert mode == "x3"
    al = (a - ah.astype(F32)).astype(BF16)
    bl = (b - bh.astype(F32)).astype(BF16)
    return fn(ah, bh) + (fn(ah, bl) + fn(al, bh))


def _dot(a, b, mode=None):
    return _mm(lambda p, q: jnp.dot(p, q, preferred_element_type=F32), a, b, mode)


def _dot_nt(a, b):
    return lax.dot_general(a, b, (((1,), (1,)), ((), ())), preferred_element_type=F32)


def _bmm(a, b):
    return jnp.einsum("bij,bjk->bik", a, b, preferred_element_type=F32)


def _bmm_nt(a, b):
    return jnp.einsum("bik,bjk->bij", a, b, preferred_element_type=F32)


def _bmm_tn(a, b):
    return jnp.einsum("bci,bcj->bij", a, b, preferred_element_type=F32)


def _layer_norm_rows(z, g, b):
    mu = jnp.mean(z, axis=-1, keepdims=True)
    zc = z - mu
    var = jnp.mean(zc * zc, axis=-1, keepdims=True)
    return zc * lax.rsqrt(var + LN_EPS) * g + b


def _ada_kernel(c_ref, w_ref, b_ref, o_ref):
    c = c_ref[...]
    cond = c * jax.nn.sigmoid(c)
    o_ref[0] = _dot(cond.astype(BF16), w_ref[0].astype(BF16)) + b_ref[0]


def _ada_call(c_pad, w_ada, b_ada):
    L, D, N = w_ada.shape
    tn = 1024
    return pl.pallas_call(
        _ada_kernel,
        out_shape=jax.ShapeDtypeStruct((L, c_pad.shape[0], N), F32),
        grid=(L, N // tn),
        in_specs=[
            pl.BlockSpec(c_pad.shape, lambda l, n: (0, 0)),
            pl.BlockSpec((1, D, tn), lambda l, n: (l, 0, n)),
            pl.BlockSpec((1, 1, tn), lambda l, n: (l, 0, n)),
        ],
        out_specs=pl.BlockSpec((1, c_pad.shape[0], tn), lambda l, n: (l, 0, n)),
        compiler_params=_cparams(("parallel", "parallel")),
        name="adaln_mod",
    )(c_pad, w_ada, b_ada.reshape(L, 1, N))


def _modmm_kernel(x_ref, sc_ref, sh_ref, w_ref, o1_ref, o2_ref, h_scr, *, n1):
    n = pl.program_id(2)

    @pl.when(n == 0)
    def _():
        h = x_ref[0] * (1.0 + sc_ref[0]) + sh_ref[0]
        h_scr[...] = h.astype(BF16)

    y = _dot(h_scr[...], w_ref[0]).astype(o1_ref.dtype)

    @pl.when(n < n1)
    def _():
        o1_ref[0] = y

    @pl.when(n >= n1)
    def _():
        o2_ref[0] = y


def _modmm_call(x, mod3, row0, w_bf16, layer, n_first, tm, tn):
    B, T, D = x.shape
    N = w_bf16.shape[2]
    n1 = n_first // tn
    return pl.pallas_call(
        functools.partial(_modmm_kernel, n1=n1),
        out_shape=[jax.ShapeDtypeStruct((B, T, n_first), BF16), jax.ShapeDtypeStruct((B, T, N - n_first), BF16)],
        grid=(B, T // tm, N // tn),
        in_specs=[
            pl.BlockSpec((1, tm, D), lambda b, m, n: (b, m, 0)),
            pl.BlockSpec((1, 1, D), lambda b, m, n: (row0 + 6 * b + 1, 0, 0)),
            pl.BlockSpec((1, 1, D), lambda b, m, n: (row0 + 6 * b, 0, 0)),
            pl.BlockSpec((1, D, tn), lambda b, m, n: (layer, 0, n)),
        ],
        out_specs=[pl.BlockSpec((1, tm, tn), lambda b, m, n: (b, m, jnp.minimum(n, n1 - 1))),
                   pl.BlockSpec((1, tm, tn), lambda b, m, n: (b, m, jnp.maximum(n - n1, 0)))],
        scratch_shapes=[pltpu.VMEM((tm, D), BF16)],
        compiler_params=_cparams(("parallel", "parallel", "arbitrary")),
        name="inproj",
    )(x, mod3, mod3, w_bf16)


def _rwkv_prep_kernel(*refs, tt, first_layer):
    if first_layer:
        (u_ref, up_ref, mu_ref, w0_ref, w2_ref, a0_ref, a2_ref, g2_ref, kk_ref, ka_ref,
         r_o, ld_o, cs_o, k_o, v_o, kk_o, a_o, g_o) = refs
    else:
        (u_ref, up_ref, mu_ref, w0_ref, w2_ref, a0_ref, a2_ref, g2_ref, kk_ref, ka_ref,
         vf_ref, v0_ref, v1_ref, v2_ref,
         r_o, ld_o, cs_o, k_o, v_o, kk_o, a_o, g_o) = refs
    ti = pl.program_id(1)
    ub = u_ref[0]
    u = ub.astype(F32)
    row = lax.broadcasted_iota(jnp.int32, (tt, tt), 0)
    col = lax.broadcasted_iota(jnp.int32, (tt, tt), 1)
    us = _dot(jnp.where(col + 1 == row, 1.0, 0.0).astype(BF16), ub)
    prev_row = jnp.where(ti > 0, up_ref[0, 15:16, :].astype(F32), 0.0)
    us = jnp.where(lax.broadcasted_iota(jnp.int32, (tt, 1), 0) == 0, prev_row, us)
    x = u + (us - u) * mu_ref[0]
    r = x[:, 0:D_RWKV]
    k = x[:, D_RWKV:2 * D_RWKV]
    v = x[:, 2 * D_RWKV:3 * D_RWKV]
    xw = x[:, RW_XW:RW_XA]
    xa = x[:, RW_XA:RW_XG]
    xg = x[:, RW_XG:RW_COLS]
    z = w0_ref[0] + _dot(jnp.tanh(xw).astype(BF16), w2_ref[0])
    ld = -EXP_NEG_HALF * jax.nn.sigmoid(z)
    ld_o[0] = ld
    chunk_ltri = jnp.where((row // CHUNK == col // CHUNK) & (col <= row), 1.0, 0.0)
    cs_o[0] = _dot(chunk_ltri, ld, "r3")
    a = jax.nn.sigmoid(a0_ref[0] + _dot(xa.astype(BF16), a2_ref[0]))
    g_o[0] = _dot(jax.nn.sigmoid(xg).astype(BF16), g2_ref[0]).astype(g_o.dtype)
    if not first_layer:
        lo = _dot(v.astype(BF16), v1_ref[0])
        gate = jax.nn.sigmoid(v0_ref[0] + _dot(lo.astype(BF16), v2_ref[0]))
        v = v + (vf_ref[0].astype(F32) - v) * gate
    r_o[0] = r.astype(r_o.dtype)
    v_o[0] = v.astype(v_o.dtype)
    a_o[0] = a.astype(a_o.dtype)
    kk_o[0] = (k * kk_ref[0]).astype(kk_o.dtype)
    k_o[0] = (k * (1.0 + (a - 1.0) * ka_ref[0])).astype(k_o.dtype)


def _rwkv_prep_call(u_rw, p, layer, v_first, tt):
    B, T, _ = u_rw.shape
    first_layer = v_first is None
    tile = pl.BlockSpec((1, tt, D_RWKV), lambda b, t: (b, t, 0))
    names = ["mu", "w0", "w2", "a0", "a2", "g2", "k_k", "k_a"]
    in_specs = [
        pl.BlockSpec((1, tt, RW_COLS), lambda b, t: (b, t, 0)),
        pl.BlockSpec((1, 16, RW_COLS), lambda b, t: (b, jnp.maximum(t * (tt // 16) - 1, 0), 0)),
    ] + [_layer_spec(p[nm], layer) for nm in names]
    args = [u_rw, u_rw] + [p[nm] for nm in names]
    if not first_layer:
        in_specs += [tile] + [_layer_spec(p[nm], layer - 1) for nm in ("v0", "v1", "v2")]
        args += [v_first, p["v0"], p["v1"], p["v2"]]
    out = lambda dt: jax.ShapeDtypeStruct((B, T, D_RWKV), dt)
    return pl.pallas_call(
        functools.partial(_rwkv_prep_kernel, tt=tt, first_layer=first_layer),
        out_shape=[out(BF16), out(F32), out(F32)] + [out(BF16)] * 5,
        grid=(B, T // tt),
        in_specs=in_specs,
        out_specs=[tile] * 8,
        compiler_params=_cparams(("parallel", "parallel")),
        name="rwkv_prep",
    )(*args)


def _rwkv_scan_kernel(r_ref, ld_ref, cs_ref, k_ref, v_ref, kk_ref, a_ref, g_ref, rk_ref, lg_ref, lb_ref,
                      w1_ref, w2_ref, o_ref, w1b_ref, w2b_ref, s_scr, wr_scr, o0_scr, pm_scr, qm_scr, bn_scr, g_scr,
                      *, ts, nt, hpb):
    C = CHUNK
    nc = ts // C
    nb = hpb * nc
    N = HEAD_DIM
    n = pl.program_id(0)

    w1b_ref[...] = w1_ref[0].astype(BF16)
    w2b_ref[...] = w2_ref[0].astype(BF16)

    @pl.when(n == 0)
    def _():
        for scr in (s_scr, wr_scr, o0_scr, pm_scr, qm_scr, bn_scr, g_scr):
            scr[...] = jnp.zeros_like(scr)

    first = lax.rem(jnp.maximum(n - 1, 0), nt) == 0
    outs = []
    for hh in range(hpb):
        S = jnp.where(first, 0.0, s_scr[hh])
        for c in range(nc):
            i = hh * nc + c
            Sb = S.astype(BF16)
            outs.append(_dot_nt(wr_scr[i], Sb) + o0_scr[i])
            S = _dot(Sb, pm_scr[i]) + qm_scr[i]
        s_scr[hh] = S
    o = jnp.stack(outs, axis=0)

    def split_row(ref):
        x = ref[0]
        return jnp.concatenate([jnp.broadcast_to(x[None, :, h * N:(h + 1) * N], (nc, 1, N)) for h in range(hpb)],
                               axis=0)

    mu = jnp.mean(o, axis=-1, keepdims=True)
    oc = o - mu
    var = jnp.mean(oc * oc, axis=-1, keepdims=True)
    on = oc * lax.rsqrt(var + GN_EPS) * split_row(lg_ref) + split_row(lb_ref)
    res = (on + bn_scr[...]) * g_scr[...]
    o_ref[0] = jnp.concatenate([res[h * nc:(h + 1) * nc].reshape(ts, N) for h in range(hpb)],
                               axis=-1).astype(o_ref.dtype)

    def split(ref):
        x = ref[0].astype(F32).reshape(nc, C, hpb * N)
        return jnp.concatenate([x[:, :, h * N:(h + 1) * N] for h in range(hpb)], axis=0)

    r, ld, cs, k, v, kkr, a, g = (split(z) for z in (r_ref, ld_ref, cs_ref, k_ref, v_ref, kk_ref, a_ref, g_ref))
    kkn = kkr / jnp.maximum(jnp.sqrt(jnp.sum(kkr * kkr, axis=-1, keepdims=True)), 1e-12)
    row = lax.broadcasted_iota(jnp.int32, (C, C), 0)
    col = lax.broadcasted_iota(jnp.int32, (C, C), 1)
    incl = col <= row
    strict = col < row
    eye = col == row
    cs_last = cs[:, C - 1:C, :]
    e_in = jnp.exp(cs)
    e_ex = jnp.exp(cs - ld)
    e_neg = jnp.exp(-cs)
    e_hat = jnp.exp(cs_last - cs)
    at = -kkn * e_ex
    b = kkn * a
    bt = b * e_neg
    kt = k * e_neg
    rt = r * e_in
    bh = (b * e_hat).astype(BF16)
    kh = (k * e_hat).astype(BF16)
    vb = v.astype(BF16)
    A = _bmm_nt(jnp.concatenate([at, rt], axis=1).astype(BF16), jnp.concatenate([bt, kt], axis=1).astype(BF16))
    a_ab = jnp.where(strict, A[:, :C, :C], 0.0)
    a_ak = jnp.where(strict, A[:, :C, C:], 0.0).astype(BF16)
    a_rb = jnp.where(incl, A[:, C:, :C], 0.0).astype(BF16)
    a_rk = jnp.where(incl, A[:, C:, C:], 0.0).astype(BF16)
    npow = a_ab.astype(BF16)
    tinv = jnp.where(eye, 1.0, 0.0).astype(F32) + a_ab
    p2 = 2
    while p2 < C:
        npow_f = _bmm(npow, npow)
        npow = npow_f.astype(BF16)
        tinv = tinv + _bmm(npow, tinv.astype(BF16))
        p2 *= 2
    akv = _bmm(a_ak, vb)
    x = _bmm(tinv.astype(BF16), jnp.concatenate([at, akv], axis=-1).astype(BF16))
    xb = x.astype(BF16)
    y = _bmm(a_rb, xb)
    pq = _bmm_tn(xb, bh)
    wr_scr[...] = (rt + y[:, :, :N]).astype(BF16)
    o0_scr[...] = y[:, :, N:] + _bmm(a_rk, vb)
    pm_scr[...] = (pq[:, :N, :] + jnp.where(eye, jnp.exp(cs_last), 0.0)).astype(BF16)
    qm_scr[...] = pq[:, N:, :] + _bmm_tn(vb, kh)
    bn_scr[...] = jnp.sum(r * k * split_row(rk_ref), axis=-1, keepdims=True) * v
    g_scr[...] = g


def _rwkv_scan_call(r, ld, cs, k, v, kk, a, g, p, w1, w2, layer, ts, hpb):
    B, T, _ = r.shape
    nt = T // ts
    hg = RWKV_HEADS // hpb
    nblk = B * hg * nt
    nb = hpb * (ts // CHUNK)
    w = hpb * HEAD_DIM

    def blk(n):
        return n // (hg * nt), lax.rem(n, nt), lax.rem(n // nt, hg)

    cur = lambda n: blk(jnp.minimum(n, nblk - 1))
    prev = lambda n: blk(jnp.maximum(n - 1, 0))
    tile = pl.BlockSpec((1, ts, w), cur)
    rowp = lambda f: pl.BlockSpec((1, 1, w), lambda n: (layer, 0, f(n)[2]))
    sq = lambda dt: pltpu.VMEM((nb, HEAD_DIM, HEAD_DIM), dt)
    _, d_in, d_ff = w1.shape
    rows1, rows2 = d_in // nblk, d_ff // nblk
    assert rows1 * nblk == d_in and rows2 * nblk == d_ff and rows1 % 16 == 0 and rows2 % 16 == 0
    slab = lambda n: jnp.minimum(n, nblk - 1)
    return pl.pallas_call(
        functools.partial(_rwkv_scan_kernel, ts=ts, nt=nt, hpb=hpb),
        out_shape=[jax.ShapeDtypeStruct((B, T, D_RWKV), BF16),
                   jax.ShapeDtypeStruct((d_in, d_ff), BF16), jax.ShapeDtypeStruct((d_ff, d_in), BF16)],
        grid=(nblk + 1,),
        in_specs=[tile] * 8 + [rowp(cur), rowp(prev), rowp(prev),
                               pl.BlockSpec((1, rows1, d_ff), lambda n: (layer, slab(n), 0)),
                               pl.BlockSpec((1, rows2, d_in), lambda n: (layer, slab(n), 0))],
        out_specs=[pl.BlockSpec((1, ts, w), prev), pl.BlockSpec((rows1, d_ff), lambda n: (slab(n), 0)),
                   pl.BlockSpec((rows2, d_in), lambda n: (slab(n), 0))],
        scratch_shapes=[pltpu.VMEM((hpb, HEAD_DIM, HEAD_DIM), F32), pltpu.VMEM((nb, CHUNK, HEAD_DIM), BF16),
                        pltpu.VMEM((nb, CHUNK, HEAD_DIM), F32), sq(BF16), sq(F32),
                        pltpu.VMEM((nb, CHUNK, HEAD_DIM), F32), pltpu.VMEM((nb, CHUNK, HEAD_DIM), F32)],
        compiler_params=_cparams(("arbitrary",)),
        name="rwkv_scan",
    )(r, ld, cs, k, v, kk, a, g, p["r_k"], p["lnx_g"], p["lnx_b"], w1, w2)


def _nsa_prep_kernel(u_ref, cos_ref, sin_ref, q_o, kc_o, vc_o, ks_o, vs_o, kw_o, vw_o, gate_o):
    cos = cos_ref[...]
    sin = sin_ref[...]
    lane = lax.broadcasted_iota(jnp.int32, cos.shape, 1)
    first_half = (lane % HEAD_DIM) < (HEAD_DIM // 2)

    def rope(x):
        other = jnp.where(first_half, pltpu.roll(x, LANES - HEAD_DIM // 2, 1), pltpu.roll(x, HEAD_DIM // 2, 1))
        return x * cos + other * sin

    def put(out_ref, col0, nheads, roped, scale=None):
        for j in range(nheads // 2):
            x = u_ref[0, :, col0 + j * LANES:col0 + (j + 1) * LANES].astype(F32)
            if roped:
                x = rope(x)
            if scale is not None:
                x = x * scale
            out_ref[0, 2 * j] = x[:, :HEAD_DIM].astype(out_ref.dtype)
            out_ref[0, 2 * j + 1] = x[:, HEAD_DIM:].astype(out_ref.dtype)

    low = lane < HEAD_DIM

    def put_wide(out_ref, col0, nheads, scale, tail):
        for j in range(nheads // 2):
            x = rope(u_ref[0, :, col0 + j * LANES:col0 + (j + 1) * LANES].astype(F32))
            if scale is not None:
                x = x * scale
            out_ref[0, 2 * j] = jnp.where(low, x, tail).astype(out_ref.dtype)
            out_ref[0, 2 * j + 1] = jnp.where(low, pltpu.roll(x, HEAD_DIM, 1), tail).astype(out_ref.dtype)

    def put_t(out_ref, col0, nheads):
        extra = out_ref.shape[2] - HEAD_DIM
        ones_row = jnp.where(lax.broadcasted_iota(jnp.int32, (extra, cos.shape[0]), 0) == 0, 1.0, 0.0)
        for j in range(nheads // 2):
            xt = u_ref[0, :, col0 + j * LANES:col0 + (j + 1) * LANES].astype(F32).T
            for i, part in enumerate((xt[:HEAD_DIM, :], xt[HEAD_DIM:, :])):
                out_ref[0, 2 * j + i, :HEAD_DIM, :] = part.astype(out_ref.dtype)
                out_ref[0, 2 * j + i, HEAD_DIM:, :] = ones_row.astype(out_ref.dtype)

    pos = pl.program_id(1) * cos.shape[0] + lax.broadcasted_iota(jnp.int32, cos.shape, 0)
    blk_onehot = jnp.where(lane - HEAD_DIM == pos // SEL_BLOCK, 1.0, 0.0)
    put_wide(q_o, 0, NSA_HEADS, HEAD_DIM ** -0.5 * LOG2_E, 0.0)
    put(kc_o, NS_KC, NSA_KV_HEADS, True)
    put(vc_o, NS_KC + NSA_KV, NSA_KV_HEADS, False)
    put_wide(ks_o, NS_KC + 2 * NSA_KV, NSA_KV_HEADS, None, blk_onehot)
    put_t(vs_o, NS_KC + 3 * NSA_KV, NSA_KV_HEADS)
    put_wide(kw_o, NS_KC + 4 * NSA_KV, NSA_KV_HEADS, None, 0.0)
    put_t(vw_o, NS_KC + 5 * NSA_KV, NSA_KV_HEADS)
    gates_t = jax.nn.sigmoid(u_ref[0, :, NS_GATE:NS_GATE + LANES].astype(F32)).T
    for gi in range(NSA_KV_HEADS):
        gate_o[0, gi] = gates_t[gi * GATE_PAD:(gi + 1) * GATE_PAD, :]


def _nsa_prep_call(u_ns, cos_t, sin_t, tt):
    B, T, _ = u_ns.shape
    G, H, N = NSA_KV_HEADS, NSA_HEADS, HEAD_DIM
    assert T // SEL_BLOCK <= LANES - N
    kv = jax.ShapeDtypeStruct((B, G, T, N), F32)
    kv_wide = jax.ShapeDtypeStruct((B, G, T, LANES), BF16)
    kv_t = jax.ShapeDtypeStruct((B, G, VT_ROWS, T), BF16)
    kv_spec = pl.BlockSpec((1, G, tt, N), lambda b, t: (b, 0, t, 0))
    kvw_spec = pl.BlockSpec((1, G, tt, LANES), lambda b, t: (b, 0, t, 0))
    kvt_spec = pl.BlockSpec((1, G, VT_ROWS, tt), lambda b, t: (b, 0, 0, t))
    return pl.pallas_call(
        _nsa_prep_kernel,
        out_shape=[jax.ShapeDtypeStruct((B, H, T, LANES), BF16), kv, kv, kv_wide, kv_t,
                   kv_wide, kv_t, jax.ShapeDtypeStruct((B, G, GATE_PAD, T), F32)],
        grid=(B, T // tt),
        in_specs=[
            pl.BlockSpec((1, tt, NS_COLS), lambda b, t: (b, t, 0)),
            pl.BlockSpec((tt, LANES), lambda b, t: (t, 0)),
            pl.BlockSpec((tt, LANES), lambda b, t: (t, 0)),
        ],
        out_specs=[pl.BlockSpec((1, H, tt, LANES), lambda b, t: (b, 0, t, 0)), kv_spec, kv_spec, kvw_spec, kvt_spec,
                   kvw_spec, kvt_spec, pl.BlockSpec((1, G, GATE_PAD, tt), lambda b, t: (b, 0, 0, t))],
        compiler_params=_cparams(("parallel", "parallel")),
        name="nsa_prep",
    )(u_ns, cos_t, sin_t)


def _compress_kernel(x_ref, pos_ref, w1_ref, w2_ref, o_ref):
    x = x_ref[0, 0]
    half = x.shape[1]
    y_top = _dot(x + pos_ref[0, 0:1, :], w1_ref[0, :half, :], "x3")
    y_bot = _dot(x + pos_ref[0, 1:2, :], w1_ref[0, half:, :], "x3")
    pre = y_top + pltpu.roll(y_bot, x.shape[0] - 1, 0)
    o_ref[0, 0] = _dot(jax.nn.gelu(pre), w2_ref[0], "x3")


def _compress_call(kv, pos, w1, w2, which):
    B, G, T, N = kv.shape
    n_half = T // CMP_STRIDE
    width = CMP_STRIDE * N
    hidden = w1.shape[-1]
    x = kv.reshape(B, G, n_half, width)
    return pl.pallas_call(
        _compress_kernel,
        out_shape=jax.ShapeDtypeStruct((B, G, n_half, w2.shape[-1]), F32),
        grid=(B, G),
        in_specs=[
            pl.BlockSpec((1, 1, n_half, width), lambda b, g: (b, g, 0, 0)),
            pl.BlockSpec((1, 2, width), lambda b, g: (which, 0, 0)),
            pl.BlockSpec((1, 2 * width, hidden), lambda b, g: (which, 0, 0)),
            pl.BlockSpec((1, hidden, w2.shape[-1]), lambda b, g: (which, 0, 0)),
        ],
        out_specs=pl.BlockSpec((1, 1, n_half, w2.shape[-1]), lambda b, g: (b, g, 0, 0)),
        compiler_params=_cparams(("parallel", "parallel")),
        name="nsa_compress",
    )(x, pos, w1, w2)


def _nsa_attn_kernel(q_ref, kc_ref, vc_ref, ks_ref, vst_ref, kw_ref, vwt_ref, gate_ref, ov_ref, put_ref,
                     o_ref, s0_scr, s1_scr, *, tq, tk):
    HPG, N = HEADS_PER_GROUP, HEAD_DIM
    qi = pl.program_id(2)
    q0 = qi * tq

    wq = HPG * tq
    q_win = q_ref[0].reshape(wq, LANES)
    rel = lax.broadcasted_iota(jnp.int32, (tk, tq), 1) - lax.broadcasted_iota(jnp.int32, (tk, tq), 0)
    init = (jnp.full((1, wq), NEG, F32), jnp.zeros((VT_ROWS, wq), F32))

    def scores(q_all, k_ref, k0, dst):
        dst[...] = _dot_nt(k_ref[0, 0, pl.ds(k0, tk), :], q_all)

    def update(carry, src, vt_ref, k0, bias):
        m_i, acc = carry
        s = src[...]
        if bias is not None:
            s = s + (bias if bias.ndim == 0 else jnp.concatenate([bias] * HPG, axis=1))
        m_n = jnp.maximum(m_i, jnp.max(s, axis=0, keepdims=True))
        p = jnp.exp2(s - m_n)
        return m_n, jnp.exp2(m_i - m_n) * acc + _dot(vt_ref[0, 0, :, pl.ds(k0, tk)], p.astype(BF16))

    def split(o):
        return [o[:, h * tq:(h + 1) * tq] for h in range(HPG)]

    def finish(carry):
        return split(carry[1][:N] / carry[1][N:N + 1])

    def window_tile(e):
        koff = e * tk - WINDOW
        lo, hi = -(tk - 1) - koff, (tq - 1) - koff
        bias = None
        if lo < 0 or hi >= WINDOW:
            dist = rel - koff
            bias = jnp.where((dist >= 0) & (dist < WINDOW), 0.0, NEG)
        if koff < 0:
            bias = jnp.where(q0 + koff >= 0, 0.0 if bias is None else bias, NEG)
        return pl.multiple_of(jnp.maximum(q0 + koff, 0), tk), bias

    win = [window_tile(e) for e in range((WINDOW + tq) // tk)]
    assert len(win) in (2, 3)
    bufs = (s0_scr, s1_scr)
    scores(q_win, kw_ref, win[0][0], bufs[0])

    n_cmp = kc_ref.shape[2]
    c_c = lax.broadcasted_iota(jnp.int32, (n_cmp, tq), 0)
    t_c = q0 + lax.broadcasted_iota(jnp.int32, (n_cmp, tq), 1)
    cmask = jnp.concatenate([c_c * CMP_STRIDE + (CMP_BLOCK - 1) <= t_c] * HPG, axis=1)
    s = jnp.where(cmask, _dot_nt(kc_ref[0, 0].astype(BF16), q_win), NEG)
    scores(q_win, kw_ref, win[1][0], bufs[1])
    m = jnp.max(s, axis=0, keepdims=True)
    e = jnp.where(cmask, jnp.exp2(s - m), 0.0)
    l = jnp.sum(e, axis=0, keepdims=True)
    p = e / jnp.maximum(l, 1e-30)
    o_cmp = split(_dot(vc_ref[0, 0].T[:N].astype(BF16), p.astype(BF16)))
    psum = p[:, :tq]
    for h in range(1, HPG):
        psum = psum + p[:, h * tq:(h + 1) * tq]

    n_sel = ov_ref.shape[0]
    imp = _dot(ov_ref[...], psum, "r3")
    carry = update(init, bufs[0], vwt_ref, *win[0])
    if len(win) == 3:
        scores(q_win, kw_ref, win[2][0], bufs[0])
    blk = lax.broadcasted_iota(jnp.int32, (n_sel, tq), 0)
    t_s = q0 + lax.broadcasted_iota(jnp.int32, (n_sel, tq), 1)
    cur = t_s // SEL_BLOCK
    forced = (blk == 0) | (blk == cur) | (blk == cur - 1)
    val = jnp.where(forced, FORCE, jnp.where(blk * SEL_BLOCK <= t_s, imp, -1.0))
    rank = jnp.zeros((n_sel, tq), F32)
    for i in range(n_sel):
        vi = val[i:i + 1, :]
        ahead = (vi > val) | ((vi == val) & (blk > i))
        rank = rank + jnp.where(ahead, 1.0, 0.0)
    unsel_t = jnp.where(rank < float(min(N_SELECT, n_sel)), 0.0, NEG)
    q_bias = lax.dot_general(unsel_t.astype(BF16), put_ref[...], (((0,), (0,)), ((), ())),
                             preferred_element_type=F32)
    carry = update(carry, bufs[1], vwt_ref, *win[1])
    q_sel = jnp.concatenate([(q_ref[0, h].astype(F32) + q_bias).astype(BF16) for h in range(HPG)], axis=0)
    o_win = finish(update(carry, bufs[0], vwt_ref, *win[2]) if len(win) == 3 else carry)

    def key0(j):
        return pl.multiple_of(j * tk, tk)

    scores(q_sel, ks_ref, key0(0), s0_scr)

    def pair(jj, carry):
        scores(q_sel, ks_ref, key0(2 * jj + 1), s1_scr)
        carry = update(carry, s0_scr, vst_ref, key0(2 * jj), None)
        scores(q_sel, ks_ref, key0(2 * jj + 2), s0_scr)
        return update(carry, s1_scr, vst_ref, key0(2 * jj + 1), None)

    carry = lax.fori_loop(0, qi // 2, pair, init)
    causal = jnp.where(rel >= 0, 0.0, NEG)

    def tail_even(c):
        return update(c, s0_scr, vst_ref, key0(qi), causal)

    def tail_odd(c):
        scores(q_sel, ks_ref, key0(qi), s1_scr)
        c = update(c, s0_scr, vst_ref, key0(qi - 1), None)
        return update(c, s1_scr, vst_ref, key0(qi), causal)

    o_slc = finish(lax.cond(qi % 2 == 1, tail_odd, tail_even, carry))

    gate = gate_ref[0, 0]
    outs = [gate[3 * h:3 * h + 1, :] * o_cmp[h] + gate[3 * h + 1:3 * h + 2, :] * o_slc[h]
            + gate[3 * h + 2:3 * h + 3, :] * o_win[h] for h in range(HPG)]
    o_ref[0] = jnp.concatenate(outs, axis=0).T.astype(o_ref.dtype)


def _nsa_attn_call(q, kc, vc, ks, vs_t, kw, vw_t, gates_t, ov_t, put, tq, tk):
    B, H, T, _ = q.shape
    N = HEAD_DIM
    G, HPG = NSA_KV_HEADS, HEADS_PER_GROUP
    assert WINDOW % tk == 0 and tq == tk
    n_half = kc.shape[2]
    cmp_spec = pl.BlockSpec((1, 1, n_half, LANES), lambda b, g, t: (b, g, 0, 0))
    kv_spec = pl.BlockSpec((1, 1, T, LANES), lambda b, g, t: (b, g, 0, 0))
    kvt_spec = pl.BlockSpec((1, 1, VT_ROWS, T), lambda b, g, t: (b, g, 0, 0))
    return pl.pallas_call(
        functools.partial(_nsa_attn_kernel, tq=tq, tk=tk),
        out_shape=jax.ShapeDtypeStruct((B, T, D_NSA), BF16),
        grid=(B, G, T // tq),
        in_specs=[
            pl.BlockSpec((1, HPG, tq, LANES), lambda b, g, t: (b, g, t, 0)),
            cmp_spec, cmp_spec, kv_spec, kvt_spec, kv_spec, kvt_spec,
            pl.BlockSpec((1, 1, GATE_PAD, tq), lambda b, g, t: (b, g, 0, t)),
            pl.BlockSpec(ov_t.shape, lambda b, g, t: (0, 0)),
            pl.BlockSpec(put.shape, lambda b, g, t: (0, 0)),
        ],
        out_specs=pl.BlockSpec((1, tq, HPG * N), lambda b, g, t: (b, t, g)),
        scratch_shapes=[pltpu.VMEM((tk, HPG * tq), F32), pltpu.VMEM((tk, HPG * tq), F32)],
        compiler_params=_cparams(("parallel", "parallel", "arbitrary")),
        name="nsa_attn",
    )(q, kc, vc, ks, vs_t, kw, vw_t, gates_t, ov_t, put)


def _outproj_kernel(orw_ref, ons_ref, x_ref, gt_ref, w_ref, g_ref, b_ref, o_ref, *, alpha):
    half = orw_ref.shape[-1]
    y = _dot(orw_ref[0].astype(BF16), w_ref[0, :half, :]) + _dot(ons_ref[0].astype(BF16), w_ref[0, half:, :])
    z = alpha * x_ref[0] + (1.0 + gt_ref[0]) * y
    o_ref[0] = _layer_norm_rows(z, g_ref[0], b_ref[0])


def _outproj_call(o_rw, o_ns, x, mod3, row0, w_bf16, ln_g, ln_b, layer, alpha, tm):
    B, T, D = x.shape
    half = o_rw.shape[-1]
    return pl.pallas_call(
        functools.partial(_outproj_kernel, alpha=alpha),
        out_shape=jax.ShapeDtypeStruct((B, T, D), F32),
        grid=(B, T // tm),
        in_specs=[
            pl.BlockSpec((1, tm, half), lambda b, m: (b, m, 0)),
            pl.BlockSpec((1, tm, half), lambda b, m: (b, m, 0)),
            pl.BlockSpec((1, tm, D), lambda b, m: (b, m, 0)),
            pl.BlockSpec((1, 1, D), lambda b, m: (row0 + 6 * b + 2, 0, 0)),
            _layer_spec(w_bf16, layer), _layer_spec(ln_g, layer), _layer_spec(ln_b, layer),
        ],
        out_specs=pl.BlockSpec((1, tm, D), lambda b, m: (b, m, 0)),
        compiler_params=_cparams(("parallel", "parallel")),
        name="outproj_ln",
    )(o_rw, o_ns, x, mod3, w_bf16, ln_g, ln_b)


def _mlp_kernel(x_ref, sc_ref, sh_ref, gt_ref, w1_ref, w2_ref, g_ref, b_ref, o_ref, h_scr, acc_scr, *, alpha):
    f = pl.program_id(2)

    @pl.when(f == 0)
    def _():
        h = x_ref[0] * (1.0 + sc_ref[0]) + sh_ref[0]
        h_scr[...] = h.astype(BF16)
        acc_scr[...] = jnp.zeros_like(acc_scr)

    a = jnp.maximum(_dot(h_scr[...], w1_ref[...]), 0.0)
    acc_scr[...] += _dot((a * a).astype(BF16), w2_ref[...])

    @pl.when(f == pl.num_programs(2) - 1)
    def _():
        z = alpha * x_ref[0] + (1.0 + gt_ref[0]) * acc_scr[...]
        o_ref[0] = _layer_norm_rows(z, g_ref[0], b_ref[0])


def _mlp_call(x, mod3, row0, w1_bf16, w2_bf16, ln_g, ln_b, layer, alpha, tm, tf):
    B, T, D = x.shape
    FF = w1_bf16.shape[1]
    modspec = lambda j: pl.BlockSpec((1, 1, D), lambda b, m, f: (row0 + 6 * b + j, 0, 0))
    return pl.pallas_call(
        functools.partial(_mlp_kernel, alpha=alpha),
        out_shape=jax.ShapeDtypeStruct((B, T, D), F32),
        grid=(B, T // tm, FF // tf),
        in_specs=[
            pl.BlockSpec((1, tm, D), lambda b, m, f: (b, m, 0)),
            modspec(4), modspec(3), modspec(5),
            pl.BlockSpec((D, tf), lambda b, m, f: (0, f)),
            pl.BlockSpec((tf, D), lambda b, m, f: (f, 0)),
            _layer_spec(ln_g, layer), _layer_spec(ln_b, layer),
        ],
        out_specs=pl.BlockSpec((1, tm, D), lambda b, m, f: (b, m, 0)),
        scratch_shapes=[pltpu.VMEM((tm, D), BF16), pltpu.VMEM((tm, D), F32)],
        compiler_params=_cparams(("parallel", "parallel", "arbitrary")),
        name="mlp_ln",
    )(x, mod3, mod3, mod3, w1_bf16, w2_bf16, ln_g, ln_b)


def _pad_last(w, n):
    return jnp.pad(w, [(0, 0)] * (w.ndim - 1) + [(0, n - w.shape[-1])])


def _pad_rows(w, n):
    return jnp.pad(w, [(0, 0)] * (w.ndim - 2) + [(0, n - w.shape[-2]), (0, 0)])


def _split_w_in(w_in):
    c = np.cumsum([0, D_RWKV, D_RWKV, D_RWKV, DECAY_LORA, AAA_LORA, GATE_LORA]).tolist()
    w_in = w_in.astype(BF16)
    rw = jnp.concatenate([w_in[..., c[0]:c[3]], _pad_last(w_in[..., c[3]:c[4]], LORA_PAD),
                          _pad_last(w_in[..., c[4]:c[5]], LORA_PAD), w_in[..., c[5]:c[6]]], axis=-1)
    ng = 3 * HEADS_PER_GROUP
    lead = w_in.shape[:-1]
    gates = _pad_last(w_in[..., c[6] + NS_GATE:].reshape(lead + (NSA_KV_HEADS, ng)), GATE_PAD)
    gates = _pad_last(gates.reshape(lead + (NSA_KV_HEADS * GATE_PAD,)), LANES)
    ns = jnp.concatenate([w_in[..., c[6]:c[6] + NS_GATE], gates], axis=-1)
    return jnp.concatenate([rw, ns], axis=-1)


def _pad_mu(mu):
    c = np.cumsum([0, 3 * D_RWKV, DECAY_LORA, AAA_LORA, GATE_LORA]).tolist()
    parts = [mu[:, c[0]:c[1]], _pad_last(mu[:, c[1]:c[2]], LORA_PAD), _pad_last(mu[:, c[2]:c[3]], LORA_PAD),
             mu[:, c[3]:c[4]]]
    return jnp.concatenate(parts, axis=-1)[:, None, :]


def _rope_tables(T):
    half = HEAD_DIM // 2
    inv = ROPE_THETA ** (-jnp.arange(half, dtype=F32) / half)
    ang = jnp.arange(T, dtype=F32)[:, None] * inv[None]
    cos, sin = jnp.cos(ang), jnp.sin(ang)
    cos_t = jnp.tile(cos, (1, LANES // half))
    sin_t = jnp.tile(jnp.concatenate([-sin, sin], axis=1), (1, LANES // HEAD_DIM))
    return cos_t, sin_t


def _selection_constants(T):
    n_half = T // CMP_STRIDE
    n_cmp = (T - CMP_BLOCK) // CMP_STRIDE + 1
    n_sel = T // SEL_BLOCK
    pos = np.arange(n_cmp)[:, None] * CMP_STRIDE + np.arange(CMP_BLOCK)[None]
    ov = ((pos // SEL_BLOCK)[..., None] == np.arange(n_sel)).sum(1) / CMP_BLOCK
    ov_t = np.zeros((n_sel, n_half), np.float32)
    ov_t[:, :n_cmp] = ov.T
    put = (np.arange(LANES)[None, :] == HEAD_DIM + np.arange(n_sel)[:, None]).astype(np.float32)
    return jnp.asarray(ov_t), jnp.asarray(put, BF16)


def kernel(x, c, w_ada, b_ada, w_in, rwkv_mu, rwkv_w0, rwkv_w2, rwkv_a0, rwkv_a2, rwkv_g2, rwkv_k_k, rwkv_k_a, rwkv_r_k, rwkv_lnx_g, rwkv_lnx_b, rwkv_v0, rwkv_v1, rwkv_v2, nsa_cmp_pos, nsa_cmp_w1, nsa_cmp_w2, w_out, ln1_g, ln1_b, mlp_w1, mlp_w2, ln2_g, ln2_b):
    B, T, D = x.shape
    L = w_ada.shape[0]
    alpha = (2 * L) ** 0.25

    c_pad = jnp.pad(c, ((0, -B % 8), (0, 0)))
    mod = _ada_call(c_pad, w_ada, b_ada)[:, :B]
    mod3 = mod.reshape(L * B * 6, 1, D)
    cos_t, sin_t = _rope_tables(T)
    ov_t, sel_put = _selection_constants(T)

    rows = lambda z: z.reshape(z.shape[0], 1, -1)
    w_in_b = _split_w_in(w_in)
    p = {
        "mu": _pad_mu(rwkv_mu), "w0": rows(rwkv_w0), "a0": rows(rwkv_a0),
        "w2": _pad_rows(rwkv_w2, LORA_PAD).astype(BF16), "a2": _pad_rows(rwkv_a2, LORA_PAD).astype(BF16),
        "g2": rwkv_g2.astype(BF16), "k_k": rows(rwkv_k_k), "k_a": rows(rwkv_k_a),
        "r_k": rows(rwkv_r_k), "lnx_g": rows(rwkv_lnx_g), "lnx_b": rows(rwkv_lnx_b),
        "v0": rows(rwkv_v0), "v1": _pad_last(rwkv_v1, LORA_PAD).astype(BF16),
        "v2": _pad_rows(rwkv_v2, LORA_PAD).astype(BF16),
    }
    half_block = CMP_STRIDE * HEAD_DIM
    cmp_pos = nsa_cmp_pos.reshape(2 * L, 2, half_block)
    cmp_w1 = nsa_cmp_w1.reshape((2 * L,) + nsa_cmp_w1.shape[2:])
    cmp_w2 = _pad_last(nsa_cmp_w2.reshape((2 * L,) + nsa_cmp_w2.shape[2:]), LANES)
    w_out_b = w_out.astype(BF16)
    ln1 = (rows(ln1_g), rows(ln1_b))
    ln2 = (rows(ln2_g), rows(ln2_b))

    v_first = None
    for i in range(L):
        row0 = i * B * 6
        u_rw, u_ns = _modmm_call(x, mod3, row0, w_in_b, i, RW_COLS, TILES["proj_m"], PROJ_TN)

        r, ld, cs, k, v, kk, a, g = _rwkv_prep_call(u_rw, p, i, v_first, TILES["prep"])
        if i == 0:
            v_first = v
        o_rw, w1_b, w2_b = _rwkv_scan_call(r, ld, cs, k, v, kk, a, g, p, mlp_w1, mlp_w2, i, TILES["scan_rows"],
                                           TILES["scan_heads"])

        q, kc_in, vc_in, ks, vs_t, kw, vw_t, gates_t = _nsa_prep_call(u_ns, cos_t, sin_t, TILES["prep"])
        kc = _compress_call(kc_in, cmp_pos, cmp_w1, cmp_w2, 2 * i)
        vc = _compress_call(vc_in, cmp_pos, cmp_w1, cmp_w2, 2 * i + 1)
        o_ns = _nsa_attn_call(q, kc, vc, ks, vs_t, kw, vw_t, gates_t, ov_t, sel_put, TILES["attn_q"], TILES["attn_k"])

        x = _outproj_call(o_rw, o_ns, x, mod3, row0, w_out_b, *ln1, i, alpha, TILES["out_m"])
        x = _mlp_call(x, mod3, row0, w1_b, w2_b, *ln2, i, alpha, TILES["mlp_m"], TILES["mlp_f"])
    return x
```

```python
import functools

import numpy as np
import jax
import jax.numpy as jnp
from jax import lax
from jax.experimental import pallas as pl
from jax.experimental.pallas import tpu as pltpu

F32 = jnp.float32
BF16 = jnp.bfloat16

HEAD_DIM = 64
RWKV_HEADS = 16
NSA_HEADS = 16
NSA_KV_HEADS = 4
HEADS_PER_GROUP = NSA_HEADS // NSA_KV_HEADS
D_RWKV = RWKV_HEADS * HEAD_DIM
D_NSA = NSA_HEADS * HEAD_DIM
NSA_KV = NSA_KV_HEADS * HEAD_DIM
DECAY_LORA = 96
AAA_LORA = 96
GATE_LORA = 256
GN_EPS = 64e-5
CMP_BLOCK = 32
CMP_STRIDE = 16
SEL_BLOCK = 64
N_SELECT = 8
WINDOW = 512
ROPE_THETA = 10000.0
NEG = -1e30
FORCE = 1e4
LN_EPS = 1e-5
LOG2_E = 1.4426950408889634
EXP_NEG_HALF = 0.6065306597126334

LANES = 128
LORA_PAD = LANES
RW_XW = 3 * D_RWKV
RW_XA = RW_XW + LORA_PAD
RW_XG = RW_XA + LORA_PAD
RW_COLS = RW_XG + GATE_LORA
NS_KC = D_NSA
NS_GATE = D_NSA + 6 * NSA_KV
NS_COLS = NS_GATE + LANES
GATE_PAD = 16
VT_ROWS = HEAD_DIM + 16
PROJ_TN = 896

CHUNK = 64
V7X_VMEM_BYTES = 64 * 1024 * 1024
VMEM_LIMIT = V7X_VMEM_BYTES - 8 * 1024 * 1024
TILES = dict(proj_m=1024, prep=256, scan_rows=128, scan_heads=8, attn_q=256, attn_k=256, out_m=512, mlp_m=512,
             mlp_f=1024)


def _layer_spec(arr, layer):
    tail = (0,) * (arr.ndim - 1)
    return pl.BlockSpec((1,) + arr.shape[1:], lambda *_: (layer,) + tail)


def _cparams(sem):
    return pltpu.CompilerParams(dimension_semantics=sem, vmem_limit_bytes=VMEM_LIMIT)


def _mm(fn, a, b, mode):
    if mode is None:
        return fn(a, b)
    ah, bh = a.astype(BF16), b.astype(BF16)
    if mode == "r3":
        r1 = b - bh.astype(F32)
        bm = r1.astype(BF16)
        bl = (r1 - bm.astype(F32)).astype(BF16)
        return fn(ah, bh) + (fn(ah, bm) + fn(ah, bl))
    assert mode == "x3"
    al = (a - ah.astype(F32)).astype(BF16)
    bl = (b - bh.astype(F32)).astype(BF16)
    return fn(ah, bh) + (fn(ah, bl) + fn(al, bh))


def _dot(a, b, mode=None):
    return _mm(lambda p, q: jnp.dot(p, q, preferred_element_type=F32), a, b, mode)


def _dot_nt(a, b):
    return lax.dot_general(a, b, (((1,), (1,)), ((), ())), preferred_element_type=F32)


def _bmm(a, b):
    return jnp.einsum("bij,bjk->bik", a, b, preferred_element_type=F32)


def _bmm_nt(a, b):
    return jnp.einsum("bik,bjk->bij", a, b, preferred_element_type=F32)


def _bmm_tn(a, b):
    return jnp.einsum("bci,bcj->bij", a, b, preferred_element_type=F32)


def _layer_norm_rows(z, g, b):
    mu = jnp.mean(z, axis=-1, keepdims=True)
    zc = z - mu
    var = jnp.mean(zc * zc, axis=-1, keepdims=True)
    return zc * lax.rsqrt(var + LN_EPS) * g + b


def _ada_kernel(c_ref, w_ref, b_ref, o_ref):
    c = c_ref[...]
    cond = c * jax.nn.sigmoid(c)
    o_ref[0] = _dot(cond.astype(BF16), w_ref[0].astype(BF16)) + b_ref[0]


def _ada_call(c_pad, w_ada, b_ada):
    L, D, N = w_ada.shape
    tn = 1024
    return pl.pallas_call(
        _ada_kernel,
        out_shape=jax.ShapeDtypeStruct((L, c_pad.shape[0], N), F32),
        grid=(L, N // tn),
        in_specs=[
            pl.BlockSpec(c_pad.shape, lambda l, n: (0, 0)),
            pl.BlockSpec((1, D, tn), lambda l, n: (l, 0, n)),
            pl.BlockSpec((1, 1, tn), lambda l, n: (l, 0, n)),
        ],
        out_specs=pl.BlockSpec((1, c_pad.shape[0], tn), lambda l, n: (l, 0, n)),
        compiler_params=_cparams(("parallel", "parallel")),
        name="adaln_mod",
    )(c_pad, w_ada, b_ada.reshape(L, 1, N))


def _modmm_kernel(x_ref, sc_ref, sh_ref, w_ref, o1_ref, o2_ref, h_scr, *, n1):
    n = pl.program_id(2)

    @pl.when(n == 0)
    def _():
        h = x_ref[0] * (1.0 + sc_ref[0]) + sh_ref[0]
        h_scr[...] = h.astype(BF16)

    y = _dot(h_scr[...], w_ref[0]).astype(o1_ref.dtype)

    @pl.when(n < n1)
    def _():
        o1_ref[0] = y

    @pl.when(n >= n1)
    def _():
        o2_ref[0] = y


def _modmm_call(x, mod3, row0, w_bf16, layer, n_first, tm, tn):
    B, T, D = x.shape
    N = w_bf16.shape[2]
    n1 = n_first // tn
    return pl.pallas_call(
        functools.partial(_modmm_kernel, n1=n1),
        out_shape=[jax.ShapeDtypeStruct((B, T, n_first), BF16), jax.ShapeDtypeStruct((B, T, N - n_first), BF16)],
        grid=(B, T // tm, N // tn),
        in_specs=[
            pl.BlockSpec((1, tm, D), lambda b, m, n: (b, m, 0)),
            pl.BlockSpec((1, 1, D), lambda b, m, n: (row0 + 6 * b + 1, 0, 0)),
            pl.BlockSpec((1, 1, D), lambda b, m, n: (row0 + 6 * b, 0, 0)),
            pl.BlockSpec((1, D, tn), lambda b, m, n: (layer, 0, n)),
        ],
        out_specs=[pl.BlockSpec((1, tm, tn), lambda b, m, n: (b, m, jnp.minimum(n, n1 - 1))),
                   pl.BlockSpec((1, tm, tn), lambda b, m, n: (b, m, jnp.maximum(n - n1, 0)))],
        scratch_shapes=[pltpu.VMEM((tm, D), BF16)],
        compiler_params=_cparams(("parallel", "parallel", "arbitrary")),
        name="inproj",
    )(x, mod3, mod3, w_bf16)


def _rwkv_prep_kernel(*refs, tt, first_layer):
    if first_layer:
        (u_ref, up_ref, mu_ref, w0_ref, w2_ref, a0_ref, a2_ref, g2_ref, kk_ref, ka_ref,
         r_o, ld_o, cs_o, k_o, v_o, kk_o, a_o, g_o) = refs
    else:
        (u_ref, up_ref, mu_ref, w0_ref, w2_ref, a0_ref, a2_ref, g2_ref, kk_ref, ka_ref,
         vf_ref, v0_ref, v1_ref, v2_ref,
         r_o, ld_o, cs_o, k_o, v_o, kk_o, a_o, g_o) = refs
    ti = pl.program_id(1)
    ub = u_ref[0]
    u = ub.astype(F32)
    row = lax.broadcasted_iota(jnp.int32, (tt, tt), 0)
    col = lax.broadcasted_iota(jnp.int32, (tt, tt), 1)
    us = _dot(jnp.where(col + 1 == row, 1.0, 0.0).astype(BF16), ub)
    prev_row = jnp.where(ti > 0, up_ref[0, 15:16, :].astype(F32), 0.0)
    us = jnp.where(lax.broadcasted_iota(jnp.int32, (tt, 1), 0) == 0, prev_row, us)
    x = u + (us - u) * mu_ref[0]
    r = x[:, 0:D_RWKV]
    k = x[:, D_RWKV:2 * D_RWKV]
    v = x[:, 2 * D_RWKV:3 * D_RWKV]
    xw = x[:, RW_XW:RW_XA]
    xa = x[:, RW_XA:RW_XG]
    xg = x[:, RW_XG:RW_COLS]
    z = w0_ref[0] + _dot(jnp.tanh(xw).astype(BF16), w2_ref[0])
    ld = -EXP_NEG_HALF * jax.nn.sigmoid(z)
    ld_o[0] = ld
    chunk_ltri = jnp.where((row // CHUNK == col // CHUNK) & (col <= row), 1.0, 0.0)
    cs_o[0] = _dot(chunk_ltri, ld, "r3")
    a = jax.nn.sigmoid(a0_ref[0] + _dot(xa.astype(BF16), a2_ref[0]))
    g_o[0] = _dot(jax.nn.sigmoid(xg).astype(BF16), g2_ref[0]).astype(g_o.dtype)
    if not first_layer:
        lo = _dot(v.astype(BF16), v1_ref[0])
        gate = jax.nn.sigmoid(v0_ref[0] + _dot(lo.astype(BF16), v2_ref[0]))
        v = v + (vf_ref[0].astype(F32) - v) * gate
    r_o[0] = r.astype(r_o.dtype)
    v_o[0] = v.astype(v_o.dtype)
    a_o[0] = a.astype(a_o.dtype)
    kk_o[0] = (k * kk_ref[0]).astype(kk_o.dtype)
    k_o[0] = (k * (1.0 + (a - 1.0) * ka_ref[0])).astype(k_o.dtype)


def _rwkv_prep_call(u_rw, p, layer, v_first, tt):
    B, T, _ = u_rw.shape
    first_layer = v_first is None
    tile = pl.BlockSpec((1, tt, D_RWKV), lambda b, t: (b, t, 0))
    names = ["mu", "w0", "w2", "a0", "a2", "g2", "k_k", "k_a"]
    in_specs = [
        pl.BlockSpec((1, tt, RW_COLS), lambda b, t: (b, t, 0)),
        pl.BlockSpec((1, 16, RW_COLS), lambda b, t: (b, jnp.maximum(t * (tt // 16) - 1, 0), 0)),
    ] + [_layer_spec(p[nm], layer) for nm in names]
    args = [u_rw, u_rw] + [p[nm] for nm in names]
    if not first_layer:
        in_specs += [tile] + [_layer_spec(p[nm], layer - 1) for nm in ("v0", "v1", "v2")]
        args += [v_first, p["v0"], p["v1"], p["v2"]]
    out = lambda dt: jax.ShapeDtypeStruct((B, T, D_RWKV), dt)
    return pl.pallas_call(
        functools.partial(_rwkv_prep_kernel, tt=tt, first_layer=first_layer),
        out_shape=[out(BF16), out(F32), out(F32)] + [out(BF16)] * 5,
        grid=(B, T // tt),
        in_specs=in_specs,
        out_specs=[tile] * 8,
        compiler_params=_cparams(("parallel", "parallel")),
        name="rwkv_prep",
    )(*args)


def _rwkv_scan_kernel(r_ref, ld_ref, cs_ref, k_ref, v_ref, kk_ref, a_ref, g_ref, rk_ref, lg_ref, lb_ref,
                      w1_ref, w2_ref, o_ref, w1b_ref, w2b_ref, s_scr, wr_scr, o0_scr, pm_scr, qm_scr, bn_scr, g_scr,
                      *, ts, nt, hpb):
    C = CHUNK
    nc = ts // C
    nb = hpb * nc
    N = HEAD_DIM
    n = pl.program_id(0)

    w1b_ref[...] = w1_ref[0].astype(BF16)
    w2b_ref[...] = w2_ref[0].astype(BF16)

    @pl.when(n == 0)
    def _():
        for scr in (s_scr, wr_scr, o0_scr, pm_scr, qm_scr, bn_scr, g_scr):
            scr[...] = jnp.zeros_like(scr)

    first = lax.rem(jnp.maximum(n - 1, 0), nt) == 0
    outs = []
    for hh in range(hpb):
        S = jnp.where(first, 0.0, s_scr[hh])
        for c in range(nc):
            i = hh * nc + c
            Sb = S.astype(BF16)
            outs.append(_dot_nt(wr_scr[i], Sb) + o0_scr[i])
            S = _dot(Sb, pm_scr[i]) + qm_scr[i]
        s_scr[hh] = S
    o = jnp.stack(outs, axis=0)

    def split_row(ref):
        x = ref[0]
        return jnp.concatenate([jnp.broadcast_to(x[None, :, h * N:(h + 1) * N], (nc, 1, N)) for h in range(hpb)],
                               axis=0)

    mu = jnp.mean(o, axis=-1, keepdims=True)
    oc = o - mu
    var = jnp.mean(oc * oc, axis=-1, keepdims=True)
    on = oc * lax.rsqrt(var + GN_EPS) * split_row(lg_ref) + split_row(lb_ref)
    res = (on + bn_scr[...]) * g_scr[...]
    o_ref[0] = jnp.concatenate([res[h * nc:(h + 1) * nc].reshape(ts, N) for h in range(hpb)],
                               axis=-1).astype(o_ref.dtype)

    def split(ref):
        x = ref[0].astype(F32).reshape(nc, C, hpb * N)
        return jnp.concatenate([x[:, :, h * N:(h + 1) * N] for h in range(hpb)], axis=0)

    r, ld, cs, k, v, kkr, a, g = (split(z) for z in (r_ref, ld_ref, cs_ref, k_ref, v_ref, kk_ref, a_ref, g_ref))
    kkn = kkr / jnp.maximum(jnp.sqrt(jnp.sum(kkr * kkr, axis=-1, keepdims=True)), 1e-12)
    row = lax.broadcasted_iota(jnp.int32, (C, C), 0)
    col = lax.broadcasted_iota(jnp.int32, (C, C), 1)
    incl = col <= row
    strict = col < row
    eye = col == row
    cs_last = cs[:, C - 1:C, :]
    e_in = jnp.exp(cs)
    e_ex = jnp.exp(cs - ld)
    e_neg = jnp.exp(-cs)
    e_hat = jnp.exp(cs_last - cs)
    at = -kkn * e_ex
    b = kkn * a
    bt = b * e_neg
    kt = k * e_neg
    rt = r * e_in
    bh = (b * e_hat).astype(BF16)
    kh = (k * e_hat).astype(BF16)
    vb = v.astype(BF16)
    A = _bmm_nt(jnp.concatenate([at, rt], axis=1).astype(BF16), jnp.concatenate([bt, kt], axis=1).astype(BF16))
    a_ab = jnp.where(strict, A[:, :C, :C], 0.0)
    a_ak = jnp.where(strict, A[:, :C, C:], 0.0).astype(BF16)
    a_rb = jnp.where(incl, A[:, C:, :C], 0.0).astype(BF16)
    a_rk = jnp.where(incl, A[:, C:, C:], 0.0).astype(BF16)
    npow = a_ab.astype(BF16)
    tinv = jnp.where(eye, 1.0, 0.0).astype(F32) + a_ab
    p2 = 2
    while p2 < C:
        npow_f = _bmm(npow, npow)
        npow = npow_f.astype(BF16)
        tinv = tinv + _bmm(npow, tinv.astype(BF16))
        p2 *= 2
    akv = _bmm(a_ak, vb)
    x = _bmm(tinv.astype(BF16), jnp.concatenate([at, akv], axis=-1).astype(BF16))
    xb = x.astype(BF16)
    y = _bmm(a_rb, xb)
    pq = _bmm_tn(xb, bh)
    wr_scr[...] = (rt + y[:, :, :N]).astype(BF16)
    o0_scr[...] = y[:, :, N:] + _bmm(a_rk, vb)
    pm_scr[...] = (pq[:, :N, :] + jnp.where(eye, jnp.exp(cs_last), 0.0)).astype(BF16)
    qm_scr[...] = pq[:, N:, :] + _bmm_tn(vb, kh)
    bn_scr[...] = jnp.sum(r * k * split_row(rk_ref), axis=-1, keepdims=True) * v
    g_scr[...] = g


def _rwkv_scan_call(r, ld, cs, k, v, kk, a, g, p, w1, w2, layer, ts, hpb):
    B, T, _ = r.shape
    nt = T // ts
    hg = RWKV_HEADS // hpb
    nblk = B * hg * nt
    nb = hpb * (ts // CHUNK)
    w = hpb * HEAD_DIM

    def blk(n):
        return n // (hg * nt), lax.rem(n, nt), lax.rem(n // nt, hg)

    cur = lambda n: blk(jnp.minimum(n, nblk - 1))
    prev = lambda n: blk(jnp.maximum(n - 1, 0))
    tile = pl.BlockSpec((1, ts, w), cur)
    rowp = lambda f: pl.BlockSpec((1, 1, w), lambda n: (layer, 0, f(n)[2]))
    sq = lambda dt: pltpu.VMEM((nb, HEAD_DIM, HEAD_DIM), dt)
    _, d_in, d_ff = w1.shape
    rows1, rows2 = d_in // nblk, d_ff // nblk
    assert rows1 * nblk == d_in and rows2 * nblk == d_ff and rows1 % 16 == 0 and rows2 % 16 == 0
    slab = lambda n: jnp.minimum(n, nblk - 1)
    return pl.pallas_call(
        functools.partial(_rwkv_scan_kernel, ts=ts, nt=nt, hpb=hpb),
        out_shape=[jax.ShapeDtypeStruct((B, T, D_RWKV), BF16),
                   jax.ShapeDtypeStruct((d_in, d_ff), BF16), jax.ShapeDtypeStruct((d_ff, d_in), BF16)],
        grid=(nblk + 1,),
        in_specs=[tile] * 8 + [rowp(cur), rowp(prev), rowp(prev),
                               pl.BlockSpec((1, rows1, d_ff), lambda n: (layer, slab(n), 0)),
                               pl.BlockSpec((1, rows2, d_in), lambda n: (layer, slab(n), 0))],
        out_specs=[pl.BlockSpec((1, ts, w), prev), pl.BlockSpec((rows1, d_ff), lambda n: (slab(n), 0)),
                   pl.BlockSpec((rows2, d_in), lambda n: (slab(n), 0))],
        scratch_shapes=[pltpu.VMEM((hpb, HEAD_DIM, HEAD_DIM), F32), pltpu.VMEM((nb, CHUNK, HEAD_DIM), BF16),
                        pltpu.VMEM((nb, CHUNK, HEAD_DIM), F32), sq(BF16), sq(F32),
                        pltpu.VMEM((nb, CHUNK, HEAD_DIM), F32), pltpu.VMEM((nb, CHUNK, HEAD_DIM), F32)],
        compiler_params=_cparams(("arbitrary",)),
        name="rwkv_scan",
    )(r, ld, cs, k, v, kk, a, g, p["r_k"], p["lnx_g"], p["lnx_b"], w1, w2)


def _nsa_prep_kernel(u_ref, cos_ref, sin_ref, q_o, kc_o, vc_o, ks_o, vs_o, kw_o, vw_o, gate_o):
    cos = cos_ref[...]
    sin = sin_ref[...]
    lane = lax.broadcasted_iota(jnp.int32, cos.shape, 1)
    first_half = (lane % HEAD_DIM) < (HEAD_DIM // 2)

    def rope(x):
        other = jnp.where(first_half, pltpu.roll(x, LANES - HEAD_DIM // 2, 1), pltpu.roll(x, HEAD_DIM // 2, 1))
        return x * cos + other * sin

    def put(out_ref, col0, nheads, roped, scale=None):
        for j in range(nheads // 2):
            x = u_ref[0, :, col0 + j * LANES:col0 + (j + 1) * LANES].astype(F32)
            if roped:
                x = rope(x)
            if scale is not None:
                x = x * scale
            out_ref[0, 2 * j] = x[:, :HEAD_DIM].astype(out_ref.dtype)
            out_ref[0, 2 * j + 1] = x[:, HEAD_DIM:].astype(out_ref.dtype)

    low = lane < HEAD_DIM

    def put_wide(out_ref, col0, nheads, scale, tail):
        for j in range(nheads // 2):
            x = rope(u_ref[0, :, col0 + j * LANES:col0 + (j + 1) * LANES].astype(F32))
            if scale is not None:
                x = x * scale
            out_ref[0, 2 * j] = jnp.where(low, x, tail).astype(out_ref.dtype)
            out_ref[0, 2 * j + 1] = jnp.where(low, pltpu.roll(x, HEAD_DIM, 1), tail).astype(out_ref.dtype)

    def put_t(out_ref, col0, nheads):
        extra = out_ref.shape[2] - HEAD_DIM
        ones_row = jnp.where(lax.broadcasted_iota(jnp.int32, (extra, cos.shape[0]), 0) == 0, 1.0, 0.0)
        for j in range(nheads // 2):
            xt = u_ref[0, :, col0 + j * LANES:col0 + (j + 1) * LANES].astype(F32).T
            for i, part in enumerate((xt[:HEAD_DIM, :], xt[HEAD_DIM:, :])):
                out_ref[0, 2 * j + i, :HEAD_DIM, :] = part.astype(out_ref.dtype)
                out_ref[0, 2 * j + i, HEAD_DIM:, :] = ones_row.astype(out_ref.dtype)

    pos = pl.program_id(1) * cos.shape[0] + lax.broadcasted_iota(jnp.int32, cos.shape, 0)
    blk_onehot = jnp.where(lane - HEAD_DIM == pos // SEL_BLOCK, 1.0, 0.0)
    put_wide(q_o, 0, NSA_HEADS, HEAD_DIM ** -0.5 * LOG2_E, 0.0)
    put(kc_o, NS_KC, NSA_KV_HEADS, True)
    put(vc_o, NS_KC + NSA_KV, NSA_KV_HEADS, False)
    put_wide(ks_o, NS_KC + 2 * NSA_KV, NSA_KV_HEADS, None, blk_onehot)
    put_t(vs_o, NS_KC + 3 * NSA_KV, NSA_KV_HEADS)
    put_wide(kw_o, NS_KC + 4 * NSA_KV, NSA_KV_HEADS, None, 0.0)
    put_t(vw_o, NS_KC + 5 * NSA_KV, NSA_KV_HEADS)
    gates_t = jax.nn.sigmoid(u_ref[0, :, NS_GATE:NS_GATE + LANES].astype(F32)).T
    for gi in range(NSA_KV_HEADS):
        gate_o[0, gi] = gates_t[gi * GATE_PAD:(gi + 1) * GATE_PAD, :]


def _nsa_prep_call(u_ns, cos_t, sin_t, tt):
    B, T, _ = u_ns.shape
    G, H, N = NSA_KV_HEADS, NSA_HEADS, HEAD_DIM
    assert T // SEL_BLOCK <= LANES - N
    kv = jax.ShapeDtypeStruct((B, G, T, N), F32)
    kv_wide = jax.ShapeDtypeStruct((B, G, T, LANES), BF16)
    kv_t = jax.ShapeDtypeStruct((B, G, VT_ROWS, T), BF16)
    kv_spec = pl.BlockSpec((1, G, tt, N), lambda b, t: (b, 0, t, 0))
    kvw_spec = pl.BlockSpec((1, G, tt, LANES), lambda b, t: (b, 0, t, 0))
    kvt_spec = pl.BlockSpec((1, G, VT_ROWS, tt), lambda b, t: (b, 0, 0, t))
    return pl.pallas_call(
        _nsa_prep_kernel,
        out_shape=[jax.ShapeDtypeStruct((B, H, T, LANES), BF16), kv, kv, kv_wide, kv_t,
                   kv_wide, kv_t, jax.ShapeDtypeStruct((B, G, GATE_PAD, T), F32)],
        grid=(B, T // tt),
        in_specs=[
            pl.BlockSpec((1, tt, NS_COLS), lambda b, t: (b, t, 0)),
            pl.BlockSpec((tt, LANES), lambda b, t: (t, 0)),
            pl.BlockSpec((tt, LANES), lambda b, t: (t, 0)),
        ],
        out_specs=[pl.BlockSpec((1, H, tt, LANES), lambda b, t: (b, 0, t, 0)), kv_spec, kv_spec, kvw_spec, kvt_spec,
                   kvw_spec, kvt_spec, pl.BlockSpec((1, G, GATE_PAD, tt), lambda b, t: (b, 0, 0, t))],
        compiler_params=_cparams(("parallel", "parallel")),
        name="nsa_prep",
    )(u_ns, cos_t, sin_t)


def _compress_kernel(x_ref, pos_ref, w1_ref, w2_ref, o_ref):
    x = x_ref[0, 0]
    half = x.shape[1]
    y_top = _dot(x + pos_ref[0, 0:1, :], w1_ref[0, :half, :], "x3")
    y_bot = _dot(x + pos_ref[0, 1:2, :], w1_ref[0, half:, :], "x3")
    pre = y_top + pltpu.roll(y_bot, x.shape[0] - 1, 0)
    o_ref[0, 0] = _dot(jax.nn.gelu(pre), w2_ref[0], "x3")


def _compress_call(kv, pos, w1, w2, which):
    B, G, T, N = kv.shape
    n_half = T // CMP_STRIDE
    width = CMP_STRIDE * N
    hidden = w1.shape[-1]
    x = kv.reshape(B, G, n_half, width)
    return pl.pallas_call(
        _compress_kernel,
        out_shape=jax.ShapeDtypeStruct((B, G, n_half, w2.shape[-1]), F32),
        grid=(B, G),
        in_specs=[
            pl.BlockSpec((1, 1, n_half, width), lambda b, g: (b, g, 0, 0)),
            pl.BlockSpec((1, 2, width), lambda b, g: (which, 0, 0)),
            pl.BlockSpec((1, 2 * width, hidden), lambda b, g: (which, 0, 0)),
            pl.BlockSpec((1, hidden, w2.shape[-1]), lambda b, g: (which, 0, 0)),
        ],
        out_specs=pl.BlockSpec((1, 1, n_half, w2.shape[-1]), lambda b, g: (b, g, 0, 0)),
        compiler_params=_cparams(("parallel", "parallel")),
        name="nsa_compress",
    )(x, pos, w1, w2)


def _nsa_attn_kernel(q_ref, kc_ref, vc_ref, ks_ref, vst_ref, kw_ref, vwt_ref, gate_ref, ov_ref, put_ref,
                     o_ref, s0_scr, s1_scr, *, tq, tk):
    HPG, N = HEADS_PER_GROUP, HEAD_DIM
    qi = pl.program_id(2)
    q0 = qi * tq

    wq = HPG * tq
    q_win = q_ref[0].reshape(wq, LANES)
    rel = lax.broadcasted_iota(jnp.int32, (tk, tq), 1) - lax.broadcasted_iota(jnp.int32, (tk, tq), 0)
    init = (jnp.full((1, wq), NEG, F32), jnp.zeros((VT_ROWS, wq), F32))

    def scores(q_all, k_ref, k0, dst):
        dst[...] = _dot_nt(k_ref[0, 0, pl.ds(k0, tk), :], q_all)

    def update(carry, src, vt_ref, k0, bias):
        m_i, acc = carry
        s = src[...]
        if bias is not None:
            s = s + (bias if bias.ndim == 0 else jnp.concatenate([bias] * HPG, axis=1))
        m_n = jnp.maximum(m_i, jnp.max(s, axis=0, keepdims=True))
        p = jnp.exp2(s - m_n)
        return m_n, jnp.exp2(m_i - m_n) * acc + _dot(vt_ref[0, 0, :, pl.ds(k0, tk)], p.astype(BF16))

    def split(o):
        return [o[:, h * tq:(h + 1) * tq] for h in range(HPG)]

    def finish(carry):
        return split(carry[1][:N] / carry[1][N:N + 1])

    def window_tile(e):
        koff = e * tk - WINDOW
        lo, hi = -(tk - 1) - koff, (tq - 1) - koff
        bias = None
        if lo < 0 or hi >= WINDOW:
            dist = rel - koff
            bias = jnp.where((dist >= 0) & (dist < WINDOW), 0.0, NEG)
        if koff < 0:
            bias = jnp.where(q0 + koff >= 0, 0.0 if bias is None else bias, NEG)
        return pl.multiple_of(jnp.maximum(q0 + koff, 0), tk), bias

    win = [window_tile(e) for e in range((WINDOW + tq) // tk)]
    assert len(win) in (2, 3)
    bufs = (s0_scr, s1_scr)
    scores(q_win, kw_ref, win[0][0], bufs[0])

    n_cmp = kc_ref.shape[2]
    c_c = lax.broadcasted_iota(jnp.int32, (n_cmp, tq), 0)
    t_c = q0 + lax.broadcasted_iota(jnp.int32, (n_cmp, tq), 1)
    cmask = jnp.concatenate([c_c * CMP_STRIDE + (CMP_BLOCK - 1) <= t_c] * HPG, axis=1)
    s = jnp.where(cmask, _dot_nt(kc_ref[0, 0].astype(BF16), q_win), NEG)
    scores(q_win, kw_ref, win[1][0], bufs[1])
    m = jnp.max(s, axis=0, keepdims=True)
    e = jnp.where(cmask, jnp.exp2(s - m), 0.0)
    l = jnp.sum(e, axis=0, keepdims=True)
    p = e / jnp.maximum(l, 1e-30)
    o_cmp = split(_dot(vc_ref[0, 0].T[:N].astype(BF16), p.astype(BF16)))
    psum = p[:, :tq]
    for h in range(1, HPG):
        psum = psum + p[:, h * tq:(h + 1) * tq]

    n_sel = ov_ref.shape[0]
    imp = _dot(ov_ref[...], psum, "r3")
    carry = update(init, bufs[0], vwt_ref, *win[0])
    if len(win) == 3:
        scores(q_win, kw_ref, win[2][0], bufs[0])
    blk = lax.broadcasted_iota(jnp.int32, (n_sel, tq), 0)
    t_s = q0 + lax.broadcasted_iota(jnp.int32, (n_sel, tq), 1)
    cur = t_s // SEL_BLOCK
    forced = (blk == 0) | (blk == cur) | (blk == cur - 1)
    val = jnp.where(forced, FORCE, jnp.where(blk * SEL_BLOCK <= t_s, imp, -1.0))
    rank = jnp.zeros((n_sel, tq), F32)
    for i in range(n_sel):
        vi = val[i:i + 1, :]
        ahead = (vi > val) | ((vi == val) & (blk > i))
        rank = rank + jnp.where(ahead, 1.0, 0.0)
    unsel_t = jnp.where(rank < float(min(N_SELECT, n_sel)), 0.0, NEG)
    q_bias = lax.dot_general(unsel_t.astype(BF16), put_ref[...], (((0,), (0,)), ((), ())),
                             preferred_element_type=F32)
    carry = update(carry, bufs[1], vwt_ref, *win[1])
    q_sel = jnp.concatenate([(q_ref[0, h].astype(F32) + q_bias).astype(BF16) for h in range(HPG)], axis=0)
    o_win = finish(update(carry, bufs[0], vwt_ref, *win[2]) if len(win) == 3 else carry)

    def key0(j):
        return pl.multiple_of(j * tk, tk)

    scores(q_sel, ks_ref, key0(0), s0_scr)

    def pair(jj, carry):
        scores(q_sel, ks_ref, key0(2 * jj + 1), s1_scr)
        carry = update(carry, s0_scr, vst_ref, key0(2 * jj), None)
        scores(q_sel, ks_ref, key0(2 * jj + 2), s0_scr)
        return update(carry, s1_scr, vst_ref, key0(2 * jj + 1), None)

    carry = lax.fori_loop(0, qi // 2, pair, init)
    causal = jnp.where(rel >= 0, 0.0, NEG)

    def tail_even(c):
        return update(c, s0_scr, vst_ref, key0(qi), causal)

    def tail_odd(c):
        scores(q_sel, ks_ref, key0(qi), s1_scr)
        c = update(c, s0_scr, vst_ref, key0(qi - 1), None)
        return update(c, s1_scr, vst_ref, key0(qi), causal)

    o_slc = finish(lax.cond(qi % 2 == 1, tail_odd, tail_even, carry))

    gate = gate_ref[0, 0]
    outs = [gate[3 * h:3 * h + 1, :] * o_cmp[h] + gate[3 * h + 1:3 * h + 2, :] * o_slc[h]
            + gate[3 * h + 2:3 * h + 3, :] * o_win[h] for h in range(HPG)]
    o_ref[0] = jnp.concatenate(outs, axis=0).T.astype(o_ref.dtype)


def _nsa_attn_call(q, kc, vc, ks, vs_t, kw, vw_t, gates_t, ov_t, put, tq, tk):
    B, H, T, _ = q.shape
    N = HEAD_DIM
    G, HPG = NSA_KV_HEADS, HEADS_PER_GROUP
    assert WINDOW % tk == 0 and tq == tk
    n_half = kc.shape[2]
    cmp_spec = pl.BlockSpec((1, 1, n_half, LANES), lambda b, g, t: (b, g, 0, 0))
    kv_spec = pl.BlockSpec((1, 1, T, LANES), lambda b, g, t: (b, g, 0, 0))
    kvt_spec = pl.BlockSpec((1, 1, VT_ROWS, T), lambda b, g, t: (b, g, 0, 0))
    return pl.pallas_call(
        functools.partial(_nsa_attn_kernel, tq=tq, tk=tk),
        out_shape=jax.ShapeDtypeStruct((B, T, D_NSA), BF16),
        grid=(B, G, T // tq),
        in_specs=[
            pl.BlockSpec((1, HPG, tq, LANES), lambda b, g, t: (b, g, t, 0)),
            cmp_spec, cmp_spec, kv_spec, kvt_spec, kv_spec, kvt_spec,
            pl.BlockSpec((1, 1, GATE_PAD, tq), lambda b, g, t: (b, g, 0, t)),
            pl.BlockSpec(ov_t.shape, lambda b, g, t: (0, 0)),
            pl.BlockSpec(put.shape, lambda b, g, t: (0, 0)),
        ],
        out_specs=pl.BlockSpec((1, tq, HPG * N), lambda b, g, t: (b, t, g)),
        scratch_shapes=[pltpu.VMEM((tk, HPG * tq), F32), pltpu.VMEM((tk, HPG * tq), F32)],
        compiler_params=_cparams(("parallel", "parallel", "arbitrary")),
        name="nsa_attn",
    )(q, kc, vc, ks, vs_t, kw, vw_t, gates_t, ov_t, put)


def _outproj_kernel(orw_ref, ons_ref, x_ref, gt_ref, w_ref, g_ref, b_ref, o_ref, *, alpha):
    half = orw_ref.shape[-1]
    rows = orw_ref.shape[1] // 2
    ys = [_dot(orw_ref[0, sl, :], w_ref[0, :half, :]) + _dot(ons_ref[0, sl, :], w_ref[0, half:, :])
          for sl in (slice(0, rows), slice(rows, 2 * rows))]
    for sl, y in zip((slice(0, rows), slice(rows, 2 * rows)), ys):
        z = alpha * x_ref[0, sl, :] + (1.0 + gt_ref[0]) * y
        o_ref[0, sl, :] = _layer_norm_rows(z, g_ref[0], b_ref[0])


def _outproj_call(o_rw, o_ns, x, mod3, row0, w_bf16, ln_g, ln_b, layer, alpha, tm):
    B, T, D = x.shape
    half = o_rw.shape[-1]
    return pl.pallas_call(
        functools.partial(_outproj_kernel, alpha=alpha),
        out_shape=jax.ShapeDtypeStruct((B, T, D), F32),
        grid=(B, T // tm),
        in_specs=[
            pl.BlockSpec((1, tm, half), lambda b, m: (b, m, 0)),
            pl.BlockSpec((1, tm, half), lambda b, m: (b, m, 0)),
            pl.BlockSpec((1, tm, D), lambda b, m: (b, m, 0)),
            pl.BlockSpec((1, 1, D), lambda b, m: (row0 + 6 * b + 2, 0, 0)),
            _layer_spec(w_bf16, layer), _layer_spec(ln_g, layer), _layer_spec(ln_b, layer),
        ],
        out_specs=pl.BlockSpec((1, tm, D), lambda b, m: (b, m, 0)),
        compiler_params=_cparams(("parallel", "parallel")),
        name="outproj_ln",
    )(o_rw, o_ns, x, mod3, w_bf16, ln_g, ln_b)


def _mlp_kernel(x_ref, sc_ref, sh_ref, gt_ref, w1_ref, w2_ref, g_ref, b_ref, o_ref, h_scr, acc_scr, *, alpha):
    f = pl.program_id(2)

    @pl.when(f == 0)
    def _():
        h = x_ref[0] * (1.0 + sc_ref[0]) + sh_ref[0]
        h_scr[...] = h.astype(BF16)
        acc_scr[...] = jnp.zeros_like(acc_scr)

    a = jnp.maximum(_dot(h_scr[...], w1_ref[...]), 0.0)
    acc_scr[...] += _dot((a * a).astype(BF16), w2_ref[...])

    @pl.when(f == pl.num_programs(2) - 1)
    def _():
        z = alpha * x_ref[0] + (1.0 + gt_ref[0]) * acc_scr[...]
        o_ref[0] = _layer_norm_rows(z, g_ref[0], b_ref[0])


def _mlp_call(x, mod3, row0, w1_bf16, w2_bf16, ln_g, ln_b, layer, alpha, tm, tf):
    B, T, D = x.shape
    FF = w1_bf16.shape[1]
    modspec = lambda j: pl.BlockSpec((1, 1, D), lambda b, m, f: (row0 + 6 * b + j, 0, 0))
    return pl.pallas_call(
        functools.partial(_mlp_kernel, alpha=alpha),
        out_shape=jax.ShapeDtypeStruct((B, T, D), F32),
        grid=(B, T // tm, FF // tf),
        in_specs=[
            pl.BlockSpec((1, tm, D), lambda b, m, f: (b, m, 0)),
            modspec(4), modspec(3), modspec(5),
            pl.BlockSpec((D, tf), lambda b, m, f: (0, f)),
            pl.BlockSpec((tf, D), lambda b, m, f: (f, 0)),
            _layer_spec(ln_g, layer), _layer_spec(ln_b, layer),
        ],
        out_specs=pl.BlockSpec((1, tm, D), lambda b, m, f: (b, m, 0)),
        scratch_shapes=[pltpu.VMEM((tm, D), BF16), pltpu.VMEM((tm, D), F32)],
        compiler_params=_cparams(("parallel", "parallel", "arbitrary")),
        name="mlp_ln",
    )(x, mod3, mod3, mod3, w1_bf16, w2_bf16, ln_g, ln_b)


def _pad_last(w, n):
    return jnp.pad(w, [(0, 0)] * (w.ndim - 1) + [(0, n - w.shape[-1])])


def _pad_rows(w, n):
    return jnp.pad(w, [(0, 0)] * (w.ndim - 2) + [(0, n - w.shape[-2]), (0, 0)])


def _split_w_in(w_in):
    c = np.cumsum([0, D_RWKV, D_RWKV, D_RWKV, DECAY_LORA, AAA_LORA, GATE_LORA]).tolist()
    w_in = lax.optimization_barrier(w_in.astype(BF16))
    rw = jnp.concatenate([w_in[..., c[0]:c[3]], _pad_last(w_in[..., c[3]:c[4]], LORA_PAD),
                          _pad_last(w_in[..., c[4]:c[5]], LORA_PAD), w_in[..., c[5]:c[6]]], axis=-1)
    ng = 3 * HEADS_PER_GROUP
    lead = w_in.shape[:-1]
    gates = _pad_last(w_in[..., c[6] + NS_GATE:].reshape(lead + (NSA_KV_HEADS, ng)), GATE_PAD)
    gates = _pad_last(gates.reshape(lead + (NSA_KV_HEADS * GATE_PAD,)), LANES)
    ns = jnp.concatenate([w_in[..., c[6]:c[6] + NS_GATE], gates], axis=-1)
    return jnp.concatenate([rw, ns], axis=-1)


def _pad_mu(mu):
    c = np.cumsum([0, 3 * D_RWKV, DECAY_LORA, AAA_LORA, GATE_LORA]).tolist()
    parts = [mu[:, c[0]:c[1]], _pad_last(mu[:, c[1]:c[2]], LORA_PAD), _pad_last(mu[:, c[2]:c[3]], LORA_PAD),
             mu[:, c[3]:c[4]]]
    return jnp.concatenate(parts, axis=-1)[:, None, :]


def _rope_tables(T):
    half = HEAD_DIM // 2
    inv = ROPE_THETA ** (-jnp.arange(half, dtype=F32) / half)
    ang = jnp.arange(T, dtype=F32)[:, None] * inv[None]
    cos, sin = jnp.cos(ang), jnp.sin(ang)
    cos_t = jnp.tile(cos, (1, LANES // half))
    sin_t = jnp.tile(jnp.concatenate([-sin, sin], axis=1), (1, LANES // HEAD_DIM))
    return cos_t, sin_t


def _selection_constants(T):
    n_half = T // CMP_STRIDE
    n_cmp = (T - CMP_BLOCK) // CMP_STRIDE + 1
    n_sel = T // SEL_BLOCK
    pos = np.arange(n_cmp)[:, None] * CMP_STRIDE + np.arange(CMP_BLOCK)[None]
    ov = ((pos // SEL_BLOCK)[..., None] == np.arange(n_sel)).sum(1) / CMP_BLOCK
    ov_t = np.zeros((n_sel, n_half), np.float32)
    ov_t[:, :n_cmp] = ov.T
    put = (np.arange(LANES)[None, :] == HEAD_DIM + np.arange(n_sel)[:, None]).astype(np.float32)
    return jnp.asarray(ov_t), jnp.asarray(put, BF16)


def kernel(x, c, w_ada, b_ada, w_in, rwkv_mu, rwkv_w0, rwkv_w2, rwkv_a0, rwkv_a2, rwkv_g2, rwkv_k_k, rwkv_k_a, rwkv_r_k, rwkv_lnx_g, rwkv_lnx_b, rwkv_v0, rwkv_v1, rwkv_v2, nsa_cmp_pos, nsa_cmp_w1, nsa_cmp_w2, w_out, ln1_g, ln1_b, mlp_w1, mlp_w2, ln2_g, ln2_b):
    B, T, D = x.shape
    L = w_ada.shape[0]
    alpha = (2 * L) ** 0.25

    c_pad = jnp.pad(c, ((0, -B % 8), (0, 0)))
    mod = _ada_call(c_pad, w_ada, b_ada)[:, :B]
    mod3 = mod.reshape(L * B * 6, 1, D)
    cos_t, sin_t = _rope_tables(T)
    ov_t, sel_put = _selection_constants(T)

    rows = lambda z: z.reshape(z.shape[0], 1, -1)
    w_in_b = _split_w_in(w_in)
    p = {
        "mu": _pad_mu(rwkv_mu), "w0": rows(rwkv_w0), "a0": rows(rwkv_a0),
        "w2": _pad_rows(rwkv_w2, LORA_PAD).astype(BF16), "a2": _pad_rows(rwkv_a2, LORA_PAD).astype(BF16),
        "g2": rwkv_g2.astype(BF16), "k_k": rows(rwkv_k_k), "k_a": rows(rwkv_k_a),
        "r_k": rows(rwkv_r_k), "lnx_g": rows(rwkv_lnx_g), "lnx_b": rows(rwkv_lnx_b),
        "v0": rows(rwkv_v0), "v1": _pad_last(rwkv_v1, LORA_PAD).astype(BF16),
        "v2": _pad_rows(rwkv_v2, LORA_PAD).astype(BF16),
    }
    half_block = CMP_STRIDE * HEAD_DIM
    cmp_pos = nsa_cmp_pos.reshape(2 * L, 2, half_block)
    cmp_w1 = nsa_cmp_w1.reshape((2 * L,) + nsa_cmp_w1.shape[2:])
    cmp_w2 = _pad_last(nsa_cmp_w2.reshape((2 * L,) + nsa_cmp_w2.shape[2:]), LANES)
    w_out_b = w_out.astype(BF16)
    ln1 = (rows(ln1_g), rows(ln1_b))
    ln2 = (rows(ln2_g), rows(ln2_b))

    v_first = None
    for i in range(L):
        row0 = i * B * 6
        u_rw, u_ns = _modmm_call(x, mod3, row0, w_in_b, i, RW_COLS, TILES["proj_m"], PROJ_TN)

        r, ld, cs, k, v, kk, a, g = _rwkv_prep_call(u_rw, p, i, v_first, TILES["prep"])
        if i == 0:
            v_first = v
        o_rw, w1_b, w2_b = _rwkv_scan_call(r, ld, cs, k, v, kk, a, g, p, mlp_w1, mlp_w2, i, TILES["scan_rows"],
                                           TILES["scan_heads"])

        q, kc_in, vc_in, ks, vs_t, kw, vw_t, gates_t = _nsa_prep_call(u_ns, cos_t, sin_t, TILES["prep"])
        kc = _compress_call(kc_in, cmp_pos, cmp_w1, cmp_w2, 2 * i)
        vc = _compress_call(vc_in, cmp_pos, cmp_w1, cmp_w2, 2 * i + 1)
        o_ns = _nsa_attn_call(q, kc, vc, ks, vs_t, kw, vw_t, gates_t, ov_t, sel_put, TILES["attn_q"], TILES["attn_k"])

        x = _outproj_call(o_rw, o_ns, x, mod3, row0, w_out_b, *ln1, i, alpha, TILES["out_m"])
        x = _mlp_call(x, mod3, row0, w1_b, w2_b, *ln2, i, alpha, TILES["mlp_m"], TILES["mlp_f"])
    return x
```

```python
import functools

import numpy as np
import jax
import jax.numpy as jnp
from jax import lax
from jax.experimental import pallas as pl
from jax.experimental.pallas import tpu as pltpu

F32 = jnp.float32
BF16 = jnp.bfloat16

HEAD_DIM = 64
RWKV_HEADS = 16
NSA_HEADS = 16
NSA_KV_HEADS = 4
HEADS_PER_GROUP = NSA_HEADS // NSA_KV_HEADS
D_RWKV = RWKV_HEADS * HEAD_DIM
D_NSA = NSA_HEADS * HEAD_DIM
NSA_KV = NSA_KV_HEADS * HEAD_DIM
DECAY_LORA = 96
AAA_LORA = 96
GATE_LORA = 256
GN_EPS = 64e-5
CMP_BLOCK = 32
CMP_STRIDE = 16
SEL_BLOCK = 64
N_SELECT = 8
WINDOW = 512
ROPE_THETA = 10000.0
NEG = -1e30
FORCE = 1e4
LN_EPS = 1e-5
LOG2_E = 1.4426950408889634
EXP_NEG_HALF = 0.6065306597126334

LANES = 128
LORA_PAD = LANES
RW_XW = 3 * D_RWKV
RW_XA = RW_XW + LORA_PAD
RW_XG = RW_XA + LORA_PAD
RW_COLS = RW_XG + GATE_LORA
NS_KC = D_NSA
NS_GATE = D_NSA + 6 * NSA_KV
NS_COLS = NS_GATE + LANES
GATE_PAD = 16
VT_ROWS = HEAD_DIM + 16
PROJ_TN = 896

CHUNK = 64
V7X_VMEM_BYTES = 64 * 1024 * 1024
VMEM_LIMIT = V7X_VMEM_BYTES - 8 * 1024 * 1024
TILES = dict(proj_m=1024, prep=256, scan_rows=128, scan_heads=8, attn_q=256, attn_k=256, out_m=512, mlp_m=512,
             mlp_f=1024)


def _layer_spec(arr, layer):
    tail = (0,) * (arr.ndim - 1)
    return pl.BlockSpec((1,) + arr.shape[1:], lambda *_: (layer,) + tail)


def _cparams(sem):
    return pltpu.CompilerParams(dimension_semantics=sem, vmem_limit_bytes=VMEM_LIMIT)


def _mm(fn, a, b, mode):
    if mode is None:
        return fn(a, b)
    ah, bh = a.astype(BF16), b.astype(BF16)
    if mode == "r3":
        r1 = b - bh.astype(F32)
        bm = r1.astype(BF16)
        bl = (r1 - bm.astype(F32)).astype(BF16)
        return fn(ah, bh) + (fn(ah, bm) + fn(ah, bl))
    assert mode == "x3"
    al = (a - ah.astype(F32)).astype(BF16)
    bl = (b - bh.astype(F32)).astype(BF16)
    return fn(ah, bh) + (fn(ah, bl) + fn(al, bh))


def _dot(a, b, mode=None):
    return _mm(lambda p, q: jnp.dot(p, q, preferred_element_type=F32), a, b, mode)


def _dot_nt(a, b):
    return lax.dot_general(a, b, (((1,), (1,)), ((), ())), preferred_element_type=F32)


def _bmm(a, b):
    return jnp.einsum("bij,bjk->bik", a, b, preferred_element_type=F32)


def _bmm_nt(a, b):
    return jnp.einsum("bik,bjk->bij", a, b, preferred_element_type=F32)


def _bmm_tn(a, b):
    return jnp.einsum("bci,bcj->bij", a, b, preferred_element_type=F32)


def _layer_norm_rows(z, g, b):
    mu = jnp.mean(z, axis=-1, keepdims=True)
    zc = z - mu
    var = jnp.mean(zc * zc, axis=-1, keepdims=True)
    return zc * lax.rsqrt(var + LN_EPS) * g + b


def _ada_kernel(c_ref, w_ref, b_ref, o_ref):
    c = c_ref[...]
    cond = c * jax.nn.sigmoid(c)
    o_ref[0] = _dot(cond.astype(BF16), w_ref[0].astype(BF16)) + b_ref[0]


def _ada_call(c_pad, w_ada, b_ada):
    L, D, N = w_ada.shape
    tn = 1024
    return pl.pallas_call(
        _ada_kernel,
        out_shape=jax.ShapeDtypeStruct((L, c_pad.shape[0], N), F32),
        grid=(L, N // tn),
        in_specs=[
            pl.BlockSpec(c_pad.shape, lambda l, n: (0, 0)),
            pl.BlockSpec((1, D, tn), lambda l, n: (l, 0, n)),
            pl.BlockSpec((1, 1, tn), lambda l, n: (l, 0, n)),
        ],
        out_specs=pl.BlockSpec((1, c_pad.shape[0], tn), lambda l, n: (l, 0, n)),
        compiler_params=_cparams(("parallel", "parallel")),
        name="adaln_mod",
    )(c_pad, w_ada, b_ada.reshape(L, 1, N))


def _modmm_kernel(x_ref, sc_ref, sh_ref, w_ref, o1_ref, o2_ref, h_scr, *, n1):
    n = pl.program_id(2)

    @pl.when(n == 0)
    def _():
        h = x_ref[0] * (1.0 + sc_ref[0]) + sh_ref[0]
        h_scr[...] = h.astype(BF16)

    y = _dot(h_scr[...], w_ref[0]).astype(o1_ref.dtype)

    @pl.when(n < n1)
    def _():
        o1_ref[0] = y

    @pl.when(n >= n1)
    def _():
        o2_ref[0] = y


def _modmm_call(x, mod3, row0, w_bf16, layer, n_first, tm, tn):
    B, T, D = x.shape
    N = w_bf16.shape[2]
    n1 = n_first // tn
    return pl.pallas_call(
        functools.partial(_modmm_kernel, n1=n1),
        out_shape=[jax.ShapeDtypeStruct((B, T, n_first), BF16), jax.ShapeDtypeStruct((B, T, N - n_first), BF16)],
        grid=(B, T // tm, N // tn),
        in_specs=[
            pl.BlockSpec((1, tm, D), lambda b, m, n: (b, m, 0)),
            pl.BlockSpec((1, 1, D), lambda b, m, n: (row0 + 6 * b + 1, 0, 0)),
            pl.BlockSpec((1, 1, D), lambda b, m, n: (row0 + 6 * b, 0, 0)),
            pl.BlockSpec((1, D, tn), lambda b, m, n: (layer, 0, n)),
        ],
        out_specs=[pl.BlockSpec((1, tm, tn), lambda b, m, n: (b, m, jnp.minimum(n, n1 - 1))),
                   pl.BlockSpec((1, tm, tn), lambda b, m, n: (b, m, jnp.maximum(n - n1, 0)))],
        scratch_shapes=[pltpu.VMEM((tm, D), BF16)],
        compiler_params=_cparams(("parallel", "parallel", "arbitrary")),
        name="inproj",
    )(x, mod3, mod3, w_bf16)


def _rwkv_prep_kernel(*refs, tt, first_layer):
    if first_layer:
        (u_ref, up_ref, mu_ref, w0_ref, w2_ref, a0_ref, a2_ref, g2_ref, kk_ref, ka_ref,
         r_o, ld_o, cs_o, k_o, v_o, kk_o, a_o, g_o) = refs
    else:
        (u_ref, up_ref, mu_ref, w0_ref, w2_ref, a0_ref, a2_ref, g2_ref, kk_ref, ka_ref,
         vf_ref, v0_ref, v1_ref, v2_ref,
         r_o, ld_o, cs_o, k_o, v_o, kk_o, a_o, g_o) = refs
    ti = pl.program_id(1)
    ub = u_ref[0]
    u = ub.astype(F32)
    row = lax.broadcasted_iota(jnp.int32, (tt, tt), 0)
    col = lax.broadcasted_iota(jnp.int32, (tt, tt), 1)
    us = _dot(jnp.where(col + 1 == row, 1.0, 0.0).astype(BF16), ub)
    prev_row = jnp.where(ti > 0, up_ref[0, 15:16, :].astype(F32), 0.0)
    us = jnp.where(lax.broadcasted_iota(jnp.int32, (tt, 1), 0) == 0, prev_row, us)
    x = u + (us - u) * mu_ref[0]
    r = x[:, 0:D_RWKV]
    k = x[:, D_RWKV:2 * D_RWKV]
    v = x[:, 2 * D_RWKV:3 * D_RWKV]
    xw = x[:, RW_XW:RW_XA]
    xa = x[:, RW_XA:RW_XG]
    xg = x[:, RW_XG:RW_COLS]
    z = w0_ref[0] + _dot(jnp.tanh(xw).astype(BF16), w2_ref[0])
    ld = -EXP_NEG_HALF * jax.nn.sigmoid(z)
    ld_o[0] = ld
    chunk_ltri = jnp.where((row // CHUNK == col // CHUNK) & (col <= row), 1.0, 0.0)
    cs_o[0] = _dot(chunk_ltri, ld, "r3")
    a = jax.nn.sigmoid(a0_ref[0] + _dot(xa.astype(BF16), a2_ref[0]))
    g_o[0] = _dot(jax.nn.sigmoid(xg).astype(BF16), g2_ref[0]).astype(g_o.dtype)
    if not first_layer:
        lo = _dot(v.astype(BF16), v1_ref[0])
        gate = jax.nn.sigmoid(v0_ref[0] + _dot(lo.astype(BF16), v2_ref[0]))
        v = v + (vf_ref[0].astype(F32) - v) * gate
    r_o[0] = r.astype(r_o.dtype)
    v_o[0] = v.astype(v_o.dtype)
    a_o[0] = a.astype(a_o.dtype)
    kk_o[0] = (k * kk_ref[0]).astype(kk_o.dtype)
    k_o[0] = (k * (1.0 + (a - 1.0) * ka_ref[0])).astype(k_o.dtype)


def _rwkv_prep_call(u_rw, p, layer, v_first, tt):
    B, T, _ = u_rw.shape
    first_layer = v_first is None
    tile = pl.BlockSpec((1, tt, D_RWKV), lambda b, t: (b, t, 0))
    names = ["mu", "w0", "w2", "a0", "a2", "g2", "k_k", "k_a"]
    in_specs = [
        pl.BlockSpec((1, tt, RW_COLS), lambda b, t: (b, t, 0)),
        pl.BlockSpec((1, 16, RW_COLS), lambda b, t: (b, jnp.maximum(t * (tt // 16) - 1, 0), 0)),
    ] + [_layer_spec(p[nm], layer) for nm in names]
    args = [u_rw, u_rw] + [p[nm] for nm in names]
    if not first_layer:
        in_specs += [tile] + [_layer_spec(p[nm], layer - 1) for nm in ("v0", "v1", "v2")]
        args += [v_first, p["v0"], p["v1"], p["v2"]]
    out = lambda dt: jax.ShapeDtypeStruct((B, T, D_RWKV), dt)
    return pl.pallas_call(
        functools.partial(_rwkv_prep_kernel, tt=tt, first_layer=first_layer),
        out_shape=[out(BF16), out(F32), out(F32)] + [out(BF16)] * 5,
        grid=(B, T // tt),
        in_specs=in_specs,
        out_specs=[tile] * 8,
        compiler_params=_cparams(("parallel", "parallel")),
        name="rwkv_prep",
    )(*args)


def _rwkv_scan_kernel(r_ref, ld_ref, cs_ref, k_ref, v_ref, kk_ref, a_ref, g_ref, rk_ref, lg_ref, lb_ref,
                      w1_ref, w2_ref, o_ref, w1b_ref, w2b_ref, s_scr, wr_scr, o0_scr, pm_scr, qm_scr, bn_scr, g_scr,
                      *, ts, nt, hpb):
    C = CHUNK
    nc = ts // C
    nb = hpb * nc
    N = HEAD_DIM
    n = pl.program_id(0)

    w1b_ref[...] = w1_ref[0].astype(BF16)
    w2b_ref[...] = w2_ref[0].astype(BF16)

    @pl.when(n == 0)
    def _():
        for scr in (s_scr, wr_scr, o0_scr, pm_scr, qm_scr, bn_scr, g_scr):
            scr[...] = jnp.zeros_like(scr)

    first = lax.rem(jnp.maximum(n - 1, 0), nt) == 0
    outs = []
    states = [jnp.where(first, 0.0, s_scr[hh]) for hh in range(hpb)]
    outs = [None] * nb
    for c in range(nc):
        for hh in range(hpb):
            i = hh * nc + c
            sb = states[hh].astype(BF16)
            outs[i] = _dot_nt(wr_scr[i], sb) + o0_scr[i]
            states[hh] = _dot(sb, pm_scr[i]) + qm_scr[i]
    for hh in range(hpb):
        s_scr[hh] = states[hh]
    o = jnp.stack(outs, axis=0)

    def split_row(ref):
        x = ref[0]
        return jnp.concatenate([jnp.broadcast_to(x[None, :, h * N:(h + 1) * N], (nc, 1, N)) for h in range(hpb)],
                               axis=0)

    mu = jnp.mean(o, axis=-1, keepdims=True)
    oc = o - mu
    var = jnp.mean(oc * oc, axis=-1, keepdims=True)
    on = oc * lax.rsqrt(var + GN_EPS) * split_row(lg_ref) + split_row(lb_ref)
    res = (on + bn_scr[...]) * g_scr[...]
    o_ref[0] = jnp.concatenate([res[h * nc:(h + 1) * nc].reshape(ts, N) for h in range(hpb)],
                               axis=-1).astype(o_ref.dtype)

    def split(ref):
        x = ref[0].astype(F32).reshape(nc, C, hpb * N)
        return jnp.concatenate([x[:, :, h * N:(h + 1) * N] for h in range(hpb)], axis=0)

    r, ld, cs, k, v, kkr, a, g = (split(z) for z in (r_ref, ld_ref, cs_ref, k_ref, v_ref, kk_ref, a_ref, g_ref))
    kkn = kkr / jnp.maximum(jnp.sqrt(jnp.sum(kkr * kkr, axis=-1, keepdims=True)), 1e-12)
    row = lax.broadcasted_iota(jnp.int32, (C, C), 0)
    col = lax.broadcasted_iota(jnp.int32, (C, C), 1)
    incl = col <= row
    strict = col < row
    eye = col == row
    cs_last = cs[:, C - 1:C, :]
    e_in = jnp.exp(cs)
    e_ex = jnp.exp(cs - ld)
    e_neg = jnp.exp(-cs)
    e_hat = jnp.exp(cs_last - cs)
    at = -kkn * e_ex
    b = kkn * a
    bt = b * e_neg
    kt = k * e_neg
    rt = r * e_in
    bh = (b * e_hat).astype(BF16)
    kh = (k * e_hat).astype(BF16)
    vb = v.astype(BF16)
    A = _bmm_nt(jnp.concatenate([at, rt], axis=1).astype(BF16), jnp.concatenate([bt, kt], axis=1).astype(BF16))
    a_ab = jnp.where(strict, A[:, :C, :C], 0.0)
    a_ak = jnp.where(strict, A[:, :C, C:], 0.0).astype(BF16)
    a_rb = jnp.where(incl, A[:, C:, :C], 0.0).astype(BF16)
    a_rk = jnp.where(incl, A[:, C:, C:], 0.0).astype(BF16)
    npow = a_ab.astype(BF16)
    tinv = jnp.where(eye, 1.0, 0.0).astype(F32) + a_ab
    p2 = 2
    while p2 < C:
        npow_f = _bmm(npow, npow)
        npow = npow_f.astype(BF16)
        tinv = tinv + _bmm(npow, tinv.astype(BF16))
        p2 *= 2
    akv = _bmm(a_ak, vb)
    x = _bmm(tinv.astype(BF16), jnp.concatenate([at, akv], axis=-1).astype(BF16))
    xb = x.astype(BF16)
    y = _bmm(a_rb, xb)
    pq = _bmm_tn(xb, bh)
    wr_scr[...] = (rt + y[:, :, :N]).astype(BF16)
    o0_scr[...] = y[:, :, N:] + _bmm(a_rk, vb)
    pm_scr[...] = (pq[:, :N, :] + jnp.where(eye, jnp.exp(cs_last), 0.0)).astype(BF16)
    qm_scr[...] = pq[:, N:, :] + _bmm_tn(vb, kh)
    bn_scr[...] = jnp.sum(r * k * split_row(rk_ref), axis=-1, keepdims=True) * v
    g_scr[...] = g


def _rwkv_scan_call(r, ld, cs, k, v, kk, a, g, p, w1, w2, layer, ts, hpb):
    B, T, _ = r.shape
    nt = T // ts
    hg = RWKV_HEADS // hpb
    nblk = B * hg * nt
    nb = hpb * (ts // CHUNK)
    w = hpb * HEAD_DIM

    def blk(n):
        return n // (hg * nt), lax.rem(n, nt), lax.rem(n // nt, hg)

    cur = lambda n: blk(jnp.minimum(n, nblk - 1))
    prev = lambda n: blk(jnp.maximum(n - 1, 0))
    tile = pl.BlockSpec((1, ts, w), cur)
    rowp = lambda f: pl.BlockSpec((1, 1, w), lambda n: (layer, 0, f(n)[2]))
    sq = lambda dt: pltpu.VMEM((nb, HEAD_DIM, HEAD_DIM), dt)
    _, d_in, d_ff = w1.shape
    rows1, rows2 = d_in // nblk, d_ff // nblk
    assert rows1 * nblk == d_in and rows2 * nblk == d_ff and rows1 % 16 == 0 and rows2 % 16 == 0
    slab = lambda n: jnp.minimum(n, nblk - 1)
    return pl.pallas_call(
        functools.partial(_rwkv_scan_kernel, ts=ts, nt=nt, hpb=hpb),
        out_shape=[jax.ShapeDtypeStruct((B, T, D_RWKV), BF16),
                   jax.ShapeDtypeStruct((d_in, d_ff), BF16), jax.ShapeDtypeStruct((d_ff, d_in), BF16)],
        grid=(nblk + 1,),
        in_specs=[tile] * 8 + [rowp(cur), rowp(prev), rowp(prev),
                               pl.BlockSpec((1, rows1, d_ff), lambda n: (layer, slab(n), 0)),
                               pl.BlockSpec((1, rows2, d_in), lambda n: (layer, slab(n), 0))],
        out_specs=[pl.BlockSpec((1, ts, w), prev), pl.BlockSpec((rows1, d_ff), lambda n: (slab(n), 0)),
                   pl.BlockSpec((rows2, d_in), lambda n: (slab(n), 0))],
        scratch_shapes=[pltpu.VMEM((hpb, HEAD_DIM, HEAD_DIM), F32), pltpu.VMEM((nb, CHUNK, HEAD_DIM), BF16),
                        pltpu.VMEM((nb, CHUNK, HEAD_DIM), F32), sq(BF16), sq(F32),
                        pltpu.VMEM((nb, CHUNK, HEAD_DIM), F32), pltpu.VMEM((nb, CHUNK, HEAD_DIM), F32)],
        compiler_params=_cparams(("arbitrary",)),
        name="rwkv_scan",
    )(r, ld, cs, k, v, kk, a, g, p["r_k"], p["lnx_g"], p["lnx_b"], w1, w2)


def _nsa_prep_kernel(u_ref, cos_ref, sin_ref, q_o, kc_o, vc_o, ks_o, vs_o, kw_o, vw_o, gate_o):
    cos = cos_ref[...]
    sin = sin_ref[...]
    lane = lax.broadcasted_iota(jnp.int32, cos.shape, 1)
    first_half = (lane % HEAD_DIM) < (HEAD_DIM // 2)

    def rope(x):
        other = jnp.where(first_half, pltpu.roll(x, LANES - HEAD_DIM // 2, 1), pltpu.roll(x, HEAD_DIM // 2, 1))
        return x * cos + other * sin

    def put(out_ref, col0, nheads, roped, scale=None):
        for j in range(nheads // 2):
            x = u_ref[0, :, col0 + j * LANES:col0 + (j + 1) * LANES].astype(F32)
            if roped:
                x = rope(x)
            if scale is not None:
                x = x * scale
            out_ref[0, 2 * j] = x[:, :HEAD_DIM].astype(out_ref.dtype)
            out_ref[0, 2 * j + 1] = x[:, HEAD_DIM:].astype(out_ref.dtype)

    low = lane < HEAD_DIM

    def put_wide(out_ref, col0, nheads, scale, tail):
        for j in range(nheads // 2):
            x = rope(u_ref[0, :, col0 + j * LANES:col0 + (j + 1) * LANES].astype(F32))
            if scale is not None:
                x = x * scale
            out_ref[0, 2 * j] = jnp.where(low, x, tail).astype(out_ref.dtype)
            out_ref[0, 2 * j + 1] = jnp.where(low, pltpu.roll(x, HEAD_DIM, 1), tail).astype(out_ref.dtype)

    def put_t(out_ref, col0, nheads):
        extra = out_ref.shape[2] - HEAD_DIM
        ones_row = jnp.where(lax.broadcasted_iota(jnp.int32, (extra, cos.shape[0]), 0) == 0, 1.0, 0.0)
        for j in range(nheads // 2):
            xt = u_ref[0, :, col0 + j * LANES:col0 + (j + 1) * LANES].astype(F32).T
            for i, part in enumerate((xt[:HEAD_DIM, :], xt[HEAD_DIM:, :])):
                out_ref[0, 2 * j + i, :HEAD_DIM, :] = part.astype(out_ref.dtype)
                out_ref[0, 2 * j + i, HEAD_DIM:, :] = ones_row.astype(out_ref.dtype)

    pos = pl.program_id(1) * cos.shape[0] + lax.broadcasted_iota(jnp.int32, cos.shape, 0)
    blk_onehot = jnp.where(lane - HEAD_DIM == pos // SEL_BLOCK, 1.0, 0.0)
    put_wide(q_o, 0, NSA_HEADS, HEAD_DIM ** -0.5 * LOG2_E, 0.0)
    put(kc_o, NS_KC, NSA_KV_HEADS, True)
    put(vc_o, NS_KC + NSA_KV, NSA_KV_HEADS, False)
    put_wide(ks_o, NS_KC + 2 * NSA_KV, NSA_KV_HEADS, None, blk_onehot)
    put_t(vs_o, NS_KC + 3 * NSA_KV, NSA_KV_HEADS)
    put_wide(kw_o, NS_KC + 4 * NSA_KV, NSA_KV_HEADS, None, 0.0)
    put_t(vw_o, NS_KC + 5 * NSA_KV, NSA_KV_HEADS)
    gates_t = jax.nn.sigmoid(u_ref[0, :, NS_GATE:NS_GATE + LANES].astype(F32)).T
    for gi in range(NSA_KV_HEADS):
        gate_o[0, gi] = gates_t[gi * GATE_PAD:(gi + 1) * GATE_PAD, :]


def _nsa_prep_call(u_ns, cos_t, sin_t, tt):
    B, T, _ = u_ns.shape
    G, H, N = NSA_KV_HEADS, NSA_HEADS, HEAD_DIM
    assert T // SEL_BLOCK <= LANES - N
    kv = jax.ShapeDtypeStruct((B, G, T, N), F32)
    kv_wide = jax.ShapeDtypeStruct((B, G, T, LANES), BF16)
    kv_t = jax.ShapeDtypeStruct((B, G, VT_ROWS, T), BF16)
    kv_spec = pl.BlockSpec((1, G, tt, N), lambda b, t: (b, 0, t, 0))
    kvw_spec = pl.BlockSpec((1, G, tt, LANES), lambda b, t: (b, 0, t, 0))
    kvt_spec = pl.BlockSpec((1, G, VT_ROWS, tt), lambda b, t: (b, 0, 0, t))
    return pl.pallas_call(
        _nsa_prep_kernel,
        out_shape=[jax.ShapeDtypeStruct((B, H, T, LANES), BF16), kv, kv, kv_wide, kv_t,
                   kv_wide, kv_t, jax.ShapeDtypeStruct((B, G, GATE_PAD, T), F32)],
        grid=(B, T // tt),
        in_specs=[
            pl.BlockSpec((1, tt, NS_COLS), lambda b, t: (b, t, 0)),
            pl.BlockSpec((tt, LANES), lambda b, t: (t, 0)),
            pl.BlockSpec((tt, LANES), lambda b, t: (t, 0)),
        ],
        out_specs=[pl.BlockSpec((1, H, tt, LANES), lambda b, t: (b, 0, t, 0)), kv_spec, kv_spec, kvw_spec, kvt_spec,
                   kvw_spec, kvt_spec, pl.BlockSpec((1, G, GATE_PAD, tt), lambda b, t: (b, 0, 0, t))],
        compiler_params=_cparams(("parallel", "parallel")),
        name="nsa_prep",
    )(u_ns, cos_t, sin_t)


def _compress_kernel(x_ref, pos_ref, w1_ref, w2_ref, o_ref):
    x = x_ref[0, 0]
    half = x.shape[1]
    y_top = _dot(x + pos_ref[0, 0:1, :], w1_ref[0, :half, :], "x3")
    y_bot = _dot(x + pos_ref[0, 1:2, :], w1_ref[0, half:, :], "x3")
    pre = y_top + pltpu.roll(y_bot, x.shape[0] - 1, 0)
    o_ref[0, 0] = _dot(jax.nn.gelu(pre), w2_ref[0], "x3")


def _compress_call(kv, pos, w1, w2, which):
    B, G, T, N = kv.shape
    n_half = T // CMP_STRIDE
    width = CMP_STRIDE * N
    hidden = w1.shape[-1]
    x = kv.reshape(B, G, n_half, width)
    return pl.pallas_call(
        _compress_kernel,
        out_shape=jax.ShapeDtypeStruct((B, G, n_half, w2.shape[-1]), F32),
        grid=(B, G),
        in_specs=[
            pl.BlockSpec((1, 1, n_half, width), lambda b, g: (b, g, 0, 0)),
            pl.BlockSpec((1, 2, width), lambda b, g: (which, 0, 0)),
            pl.BlockSpec((1, 2 * width, hidden), lambda b, g: (which, 0, 0)),
            pl.BlockSpec((1, hidden, w2.shape[-1]), lambda b, g: (which, 0, 0)),
        ],
        out_specs=pl.BlockSpec((1, 1, n_half, w2.shape[-1]), lambda b, g: (b, g, 0, 0)),
        compiler_params=_cparams(("parallel", "parallel")),
        name="nsa_compress",
    )(x, pos, w1, w2)


def _nsa_attn_kernel(q_ref, kc_ref, vc_ref, ks_ref, vst_ref, kw_ref, vwt_ref, gate_ref, ov_ref, put_ref,
                     o_ref, s0_scr, s1_scr, *, tq, tk):
    HPG, N = HEADS_PER_GROUP, HEAD_DIM
    qi = pl.program_id(2)
    q0 = qi * tq

    wq = HPG * tq
    q_win = q_ref[0].reshape(wq, LANES)
    rel = lax.broadcasted_iota(jnp.int32, (tk, tq), 1) - lax.broadcasted_iota(jnp.int32, (tk, tq), 0)
    init = (jnp.full((1, wq), NEG, F32), jnp.zeros((VT_ROWS, wq), F32))

    def scores(q_all, k_ref, k0, dst):
        dst[...] = _dot_nt(k_ref[0, 0, pl.ds(k0, tk), :], q_all)

    def update(carry, src, vt_ref, k0, bias):
        m_i, acc = carry
        s = src[...]
        if bias is not None:
            s = s + (bias if bias.ndim == 0 else jnp.concatenate([bias] * HPG, axis=1))
        m_n = jnp.maximum(m_i, jnp.max(s, axis=0, keepdims=True))
        p = jnp.exp2(s - m_n)
        return m_n, jnp.exp2(m_i - m_n) * acc + _dot(vt_ref[0, 0, :, pl.ds(k0, tk)], p.astype(BF16))

    def split(o):
        return [o[:, h * tq:(h + 1) * tq] for h in range(HPG)]

    def finish(carry):
        return split(carry[1][:N] / carry[1][N:N + 1])

    def window_tile(e):
        koff = e * tk - WINDOW
        lo, hi = -(tk - 1) - koff, (tq - 1) - koff
        bias = None
        if lo < 0 or hi >= WINDOW:
            dist = rel - koff
            bias = jnp.where((dist >= 0) & (dist < WINDOW), 0.0, NEG)
        if koff < 0:
            bias = jnp.where(q0 + koff >= 0, 0.0 if bias is None else bias, NEG)
        return pl.multiple_of(jnp.maximum(q0 + koff, 0), tk), bias

    win = [window_tile(e) for e in range((WINDOW + tq) // tk)]
    assert len(win) in (2, 3)
    bufs = (s0_scr, s1_scr)
    scores(q_win, kw_ref, win[0][0], bufs[0])

    n_cmp = kc_ref.shape[2]
    c_c = lax.broadcasted_iota(jnp.int32, (n_cmp, tq), 0)
    t_c = q0 + lax.broadcasted_iota(jnp.int32, (n_cmp, tq), 1)
    cmask = jnp.concatenate([c_c * CMP_STRIDE + (CMP_BLOCK - 1) <= t_c] * HPG, axis=1)
    s = jnp.where(cmask, _dot_nt(kc_ref[0, 0].astype(BF16), q_win), NEG)
    scores(q_win, kw_ref, win[1][0], bufs[1])
    m = jnp.max(s, axis=0, keepdims=True)
    e = jnp.where(cmask, jnp.exp2(s - m), 0.0)
    l = jnp.sum(e, axis=0, keepdims=True)
    p = e / jnp.maximum(l, 1e-30)
    o_cmp = split(_dot(vc_ref[0, 0].T[:N].astype(BF16), p.astype(BF16)))
    psum = p[:, :tq]
    for h in range(1, HPG):
        psum = psum + p[:, h * tq:(h + 1) * tq]

    n_sel = ov_ref.shape[0]
    imp = _dot(ov_ref[...], psum, "r3")
    carry = update(init, bufs[0], vwt_ref, *win[0])
    if len(win) == 3:
        scores(q_win, kw_ref, win[2][0], bufs[0])
    blk = lax.broadcasted_iota(jnp.int32, (n_sel, tq), 0)
    t_s = q0 + lax.broadcasted_iota(jnp.int32, (n_sel, tq), 1)
    cur = t_s // SEL_BLOCK
    forced = (blk == 0) | (blk == cur) | (blk == cur - 1)
    val = jnp.where(forced, FORCE, jnp.where(blk * SEL_BLOCK <= t_s, imp, -1.0))
    rank = jnp.zeros((n_sel, tq), F32)
    for i in range(n_sel):
        vi = val[i:i + 1, :]
        ahead = (vi > val) | ((vi == val) & (blk > i))
        rank = rank + jnp.where(ahead, 1.0, 0.0)
    unsel_t = jnp.where(rank < float(min(N_SELECT, n_sel)), 0.0, NEG)
    q_bias = lax.dot_general(unsel_t.astype(BF16), put_ref[...], (((0,), (0,)), ((), ())),
                             preferred_element_type=F32)
    carry = update(carry, bufs[1], vwt_ref, *win[1])
    q_sel = jnp.concatenate([(q_ref[0, h].astype(F32) + q_bias).astype(BF16) for h in range(HPG)], axis=0)
    o_win = finish(update(carry, bufs[0], vwt_ref, *win[2]) if len(win) == 3 else carry)

    def key0(j):
        return pl.multiple_of(j * tk, tk)

    scores(q_sel, ks_ref, key0(0), s0_scr)

    def pair(jj, carry):
        scores(q_sel, ks_ref, key0(2 * jj + 1), s1_scr)
        carry = update(carry, s0_scr, vst_ref, key0(2 * jj), None)
        scores(q_sel, ks_ref, key0(2 * jj + 2), s0_scr)
        return update(carry, s1_scr, vst_ref, key0(2 * jj + 1), None)

    carry = lax.fori_loop(0, qi // 2, pair, init)
    causal = jnp.where(rel >= 0, 0.0, NEG)

    def tail_even(c):
        return update(c, s0_scr, vst_ref, key0(qi), causal)

    def tail_odd(c):
        scores(q_sel, ks_ref, key0(qi), s1_scr)
        c = update(c, s0_scr, vst_ref, key0(qi - 1), None)
        return update(c, s1_scr, vst_ref, key0(qi), causal)

    o_slc = finish(lax.cond(qi % 2 == 1, tail_odd, tail_even, carry))

    gate = gate_ref[0, 0]
    outs = [gate[3 * h:3 * h + 1, :] * o_cmp[h] + gate[3 * h + 1:3 * h + 2, :] * o_slc[h]
            + gate[3 * h + 2:3 * h + 3, :] * o_win[h] for h in range(HPG)]
    o_ref[0] = jnp.concatenate(outs, axis=0).T.astype(o_ref.dtype)


def _nsa_attn_call(q, kc, vc, ks, vs_t, kw, vw_t, gates_t, ov_t, put, tq, tk):
    B, H, T, _ = q.shape
    N = HEAD_DIM
    G, HPG = NSA_KV_HEADS, HEADS_PER_GROUP
    assert WINDOW % tk == 0 and tq == tk
    n_half = kc.shape[2]
    cmp_spec = pl.BlockSpec((1, 1, n_half, LANES), lambda b, g, t: (b, g, 0, 0))
    kv_spec = pl.BlockSpec((1, 1, T, LANES), lambda b, g, t: (b, g, 0, 0))
    kvt_spec = pl.BlockSpec((1, 1, VT_ROWS, T), lambda b, g, t: (b, g, 0, 0))
    return pl.pallas_call(
        functools.partial(_nsa_attn_kernel, tq=tq, tk=tk),
        out_shape=jax.ShapeDtypeStruct((B, T, D_NSA), BF16),
        grid=(B, G, T // tq),
        in_specs=[
            pl.BlockSpec((1, HPG, tq, LANES), lambda b, g, t: (b, g, t, 0)),
            cmp_spec, cmp_spec, kv_spec, kvt_spec, kv_spec, kvt_spec,
            pl.BlockSpec((1, 1, GATE_PAD, tq), lambda b, g, t: (b, g, 0, t)),
            pl.BlockSpec(ov_t.shape, lambda b, g, t: (0, 0)),
            pl.BlockSpec(put.shape, lambda b, g, t: (0, 0)),
        ],
        out_specs=pl.BlockSpec((1, tq, HPG * N), lambda b, g, t: (b, t, g)),
        scratch_shapes=[pltpu.VMEM((tk, HPG * tq), F32), pltpu.VMEM((tk, HPG * tq), F32)],
        compiler_params=_cparams(("parallel", "parallel", "arbitrary")),
        name="nsa_attn",
    )(q, kc, vc, ks, vs_t, kw, vw_t, gates_t, ov_t, put)


def _outproj_kernel(orw_ref, ons_ref, x_ref, gt_ref, w_ref, g_ref, b_ref, o_ref, *, alpha):
    half = orw_ref.shape[-1]
    rows = orw_ref.shape[1] // 2
    ys = [_dot(orw_ref[0, sl, :], w_ref[0, :half, :]) + _dot(ons_ref[0, sl, :], w_ref[0, half:, :])
          for sl in (slice(0, rows), slice(rows, 2 * rows))]
    for sl, y in zip((slice(0, rows), slice(rows, 2 * rows)), ys):
        z = alpha * x_ref[0, sl, :] + (1.0 + gt_ref[0]) * y
        o_ref[0, sl, :] = _layer_norm_rows(z, g_ref[0], b_ref[0])


def _outproj_call(o_rw, o_ns, x, mod3, row0, w_bf16, ln_g, ln_b, layer, alpha, tm):
    B, T, D = x.shape
    half = o_rw.shape[-1]
    return pl.pallas_call(
        functools.partial(_outproj_kernel, alpha=alpha),
        out_shape=jax.ShapeDtypeStruct((B, T, D), F32),
        grid=(B, T // tm),
        in_specs=[
            pl.BlockSpec((1, tm, half), lambda b, m: (b, m, 0)),
            pl.BlockSpec((1, tm, half), lambda b, m: (b, m, 0)),
            pl.BlockSpec((1, tm, D), lambda b, m: (b, m, 0)),
            pl.BlockSpec((1, 1, D), lambda b, m: (row0 + 6 * b + 2, 0, 0)),
            _layer_spec(w_bf16, layer), _layer_spec(ln_g, layer), _layer_spec(ln_b, layer),
        ],
        out_specs=pl.BlockSpec((1, tm, D), lambda b, m: (b, m, 0)),
        compiler_params=_cparams(("parallel", "parallel")),
        name="outproj_ln",
    )(o_rw, o_ns, x, mod3, w_bf16, ln_g, ln_b)


def _mlp_kernel(x_ref, sc_ref, sh_ref, gt_ref, w1_ref, w2_ref, g_ref, b_ref, o_ref, h_scr, acc_scr, *, alpha):
    f = pl.program_id(2)

    @pl.when(f == 0)
    def _():
        h = x_ref[0] * (1.0 + sc_ref[0]) + sh_ref[0]
        h_scr[...] = h.astype(BF16)
        acc_scr[...] = jnp.zeros_like(acc_scr)

    a = jnp.maximum(_dot(h_scr[...], w1_ref[...]), 0.0)
    acc_scr[...] += _dot((a * a).astype(BF16), w2_ref[...])

    @pl.when(f == pl.num_programs(2) - 1)
    def _():
        z = alpha * x_ref[0] + (1.0 + gt_ref[0]) * acc_scr[...]
        o_ref[0] = _layer_norm_rows(z, g_ref[0], b_ref[0])


def _mlp_call(x, mod3, row0, w1_bf16, w2_bf16, ln_g, ln_b, layer, alpha, tm, tf):
    B, T, D = x.shape
    FF = w1_bf16.shape[1]
    modspec = lambda j: pl.BlockSpec((1, 1, D), lambda b, m, f: (row0 + 6 * b + j, 0, 0))
    return pl.pallas_call(
        functools.partial(_mlp_kernel, alpha=alpha),
        out_shape=jax.ShapeDtypeStruct((B, T, D), F32),
        grid=(B, T // tm, FF // tf),
        in_specs=[
            pl.BlockSpec((1, tm, D), lambda b, m, f: (b, m, 0)),
            modspec(4), modspec(3), modspec(5),
            pl.BlockSpec((D, tf), lambda b, m, f: (0, f)),
            pl.BlockSpec((tf, D), lambda b, m, f: (f, 0)),
            _layer_spec(ln_g, layer), _layer_spec(ln_b, layer),
        ],
        out_specs=pl.BlockSpec((1, tm, D), lambda b, m, f: (b, m, 0)),
        scratch_shapes=[pltpu.VMEM((tm, D), BF16), pltpu.VMEM((tm, D), F32)],
        compiler_params=_cparams(("parallel", "parallel", "arbitrary")),
        name="mlp_ln",
    )(x, mod3, mod3, mod3, w1_bf16, w2_bf16, ln_g, ln_b)


def _pad_last(w, n):
    return jnp.pad(w, [(0, 0)] * (w.ndim - 1) + [(0, n - w.shape[-1])])


def _pad_rows(w, n):
    return jnp.pad(w, [(0, 0)] * (w.ndim - 2) + [(0, n - w.shape[-2]), (0, 0)])


def _split_w_in(w_in):
    c = np.cumsum([0, D_RWKV, D_RWKV, D_RWKV, DECAY_LORA, AAA_LORA, GATE_LORA]).tolist()
    w_in = w_in.astype(BF16)
    rw = jnp.concatenate([w_in[..., c[0]:c[3]], _pad_last(w_in[..., c[3]:c[4]], LORA_PAD),
                          _pad_last(w_in[..., c[4]:c[5]], LORA_PAD), w_in[..., c[5]:c[6]]], axis=-1)
    ng = 3 * HEADS_PER_GROUP
    lead = w_in.shape[:-1]
    gates = _pad_last(w_in[..., c[6] + NS_GATE:].reshape(lead + (NSA_KV_HEADS, ng)), GATE_PAD)
    gates = _pad_last(gates.reshape(lead + (NSA_KV_HEADS * GATE_PAD,)), LANES)
    ns = jnp.concatenate([w_in[..., c[6]:c[6] + NS_GATE], gates], axis=-1)
    return jnp.concatenate([rw, ns], axis=-1)


def _pad_mu(mu):
    c = np.cumsum([0, 3 * D_RWKV, DECAY_LORA, AAA_LORA, GATE_LORA]).tolist()
    parts = [mu[:, c[0]:c[1]], _pad_last(mu[:, c[1]:c[2]], LORA_PAD), _pad_last(mu[:, c[2]:c[3]], LORA_PAD),
             mu[:, c[3]:c[4]]]
    return jnp.concatenate(parts, axis=-1)[:, None, :]


def _rope_tables(T):
    half = HEAD_DIM // 2
    inv = ROPE_THETA ** (-jnp.arange(half, dtype=F32) / half)
    ang = jnp.arange(T, dtype=F32)[:, None] * inv[None]
    cos, sin = jnp.cos(ang), jnp.sin(ang)
    cos_t = jnp.tile(cos, (1, LANES // half))
    sin_t = jnp.tile(jnp.concatenate([-sin, sin], axis=1), (1, LANES // HEAD_DIM))
    return cos_t, sin_t


def _selection_constants(T):
    n_half = T // CMP_STRIDE
    n_cmp = (T - CMP_BLOCK) // CMP_STRIDE + 1
    n_sel = T // SEL_BLOCK
    pos = np.arange(n_cmp)[:, None] * CMP_STRIDE + np.arange(CMP_BLOCK)[None]
    ov = ((pos // SEL_BLOCK)[..., None] == np.arange(n_sel)).sum(1) / CMP_BLOCK
    ov_t = np.zeros((n_sel, n_half), np.float32)
    ov_t[:, :n_cmp] = ov.T
    put = (np.arange(LANES)[None, :] == HEAD_DIM + np.arange(n_sel)[:, None]).astype(np.float32)
    return jnp.asarray(ov_t), jnp.asarray(put, BF16)


def kernel(x, c, w_ada, b_ada, w_in, rwkv_mu, rwkv_w0, rwkv_w2, rwkv_a0, rwkv_a2, rwkv_g2, rwkv_k_k, rwkv_k_a, rwkv_r_k, rwkv_lnx_g, rwkv_lnx_b, rwkv_v0, rwkv_v1, rwkv_v2, nsa_cmp_pos, nsa_cmp_w1, nsa_cmp_w2, w_out, ln1_g, ln1_b, mlp_w1, mlp_w2, ln2_g, ln2_b):
    B, T, D = x.shape
    L = w_ada.shape[0]
    alpha = (2 * L) ** 0.25

    c_pad = jnp.pad(c, ((0, -B % 8), (0, 0)))
    mod = _ada_call(c_pad, w_ada, b_ada)[:, :B]
    mod3 = mod.reshape(L * B * 6, 1, D)
    cos_t, sin_t = _rope_tables(T)
    ov_t, sel_put = _selection_constants(T)

    rows = lambda z: z.reshape(z.shape[0], 1, -1)
    w_in_b = _split_w_in(w_in)
    p = {
        "mu": _pad_mu(rwkv_mu), "w0": rows(rwkv_w0), "a0": rows(rwkv_a0),
        "w2": _pad_rows(rwkv_w2, LORA_PAD).astype(BF16), "a2": _pad_rows(rwkv_a2, LORA_PAD).astype(BF16),
        "g2": rwkv_g2.astype(BF16), "k_k": rows(rwkv_k_k), "k_a": rows(rwkv_k_a),
        "r_k": rows(rwkv_r_k), "lnx_g": rows(rwkv_lnx_g), "lnx_b": rows(rwkv_lnx_b),
        "v0": rows(rwkv_v0), "v1": _pad_last(rwkv_v1, LORA_PAD).astype(BF16),
        "v2": _pad_rows(rwkv_v2, LORA_PAD).astype(BF16),
    }
    half_block = CMP_STRIDE * HEAD_DIM
    cmp_pos = nsa_cmp_pos.reshape(2 * L, 2, half_block)
    cmp_w1 = nsa_cmp_w1.reshape((2 * L,) + nsa_cmp_w1.shape[2:])
    cmp_w2 = _pad_last(nsa_cmp_w2.reshape((2 * L,) + nsa_cmp_w2.shape[2:]), LANES)
    w_out_b = w_out.astype(BF16)
    ln1 = (rows(ln1_g), rows(ln1_b))
    ln2 = (rows(ln2_g), rows(ln2_b))

    v_first = None
    for i in range(L):
        row0 = i * B * 6
        u_rw, u_ns = _modmm_call(x, mod3, row0, w_in_b, i, RW_COLS, TILES["proj_m"], PROJ_TN)

        r, ld, cs, k, v, kk, a, g = _rwkv_prep_call(u_rw, p, i, v_first, TILES["prep"])
        if i == 0:
            v_first = v
        o_rw, w1_b, w2_b = _rwkv_scan_call(r, ld, cs, k, v, kk, a, g, p, mlp_w1, mlp_w2, i, TILES["scan_rows"],
                                           TILES["scan_heads"])

        q, kc_in, vc_in, ks, vs_t, kw, vw_t, gates_t = _nsa_prep_call(u_ns, cos_t, sin_t, TILES["prep"])
        kc = _compress_call(kc_in, cmp_pos, cmp_w1, cmp_w2, 2 * i)
        vc = _compress_call(vc_in, cmp_pos, cmp_w1, cmp_w2, 2 * i + 1)
        o_ns = _nsa_attn_call(q, kc, vc, ks, vs_t, kw, vw_t, gates_t, ov_t, sel_put, TILES["attn_q"], TILES["attn_k"])

        x = _outproj_call(o_rw, o_ns, x, mod3, row0, w_out_b, *ln1, i, alpha, TILES["out_m"])
        x = _mlp_call(x, mod3, row0, w1_b, w2_b, *ln2, i, alpha, TILES["mlp_m"], TILES["mlp_f"])
    return x
```

```python
import functools

import numpy as np
import jax
import jax.numpy as jnp
from jax import lax
from jax.experimental import pallas as pl
from jax.experimental.pallas import tpu as pltpu

F32 = jnp.float32
BF16 = jnp.bfloat16

HEAD_DIM = 64
RWKV_HEADS = 16
NSA_HEADS = 16
NSA_KV_HEADS = 4
HEADS_PER_GROUP = NSA_HEADS // NSA_KV_HEADS
D_RWKV = RWKV_HEADS * HEAD_DIM
D_NSA = NSA_HEADS * HEAD_DIM
NSA_KV = NSA_KV_HEADS * HEAD_DIM
DECAY_LORA = 96
AAA_LORA = 96
GATE_LORA = 256
GN_EPS = 64e-5
CMP_BLOCK = 32
CMP_STRIDE = 16
SEL_BLOCK = 64
N_SELECT = 8
WINDOW = 512
ROPE_THETA = 10000.0
NEG = -1e30
FORCE = 1e4
LN_EPS = 1e-5
LOG2_E = 1.4426950408889634
EXP_NEG_HALF = 0.6065306597126334

LANES = 128
LORA_PAD = LANES
RW_XW = 3 * D_RWKV
RW_XA = RW_XW + LORA_PAD
RW_XG = RW_XA + LORA_PAD
RW_COLS = RW_XG + GATE_LORA
NS_KC = D_NSA
NS_GATE = D_NSA + 6 * NSA_KV
NS_COLS = NS_GATE + LANES
GATE_PAD = 16
VT_ROWS = HEAD_DIM + 16
PROJ_TN = 896

CHUNK = 64
V7X_VMEM_BYTES = 64 * 1024 * 1024
VMEM_LIMIT = V7X_VMEM_BYTES - 8 * 1024 * 1024
TILES = dict(proj_m=1024, prep=256, scan_rows=128, scan_heads=8, attn_q=256, attn_k=256, out_m=512, mlp_m=512,
             mlp_f=1024)


def _layer_spec(arr, layer):
    tail = (0,) * (arr.ndim - 1)
    return pl.BlockSpec((1,) + arr.shape[1:], lambda *_: (layer,) + tail)


def _cparams(sem):
    return pltpu.CompilerParams(dimension_semantics=sem, vmem_limit_bytes=VMEM_LIMIT)


def _mm(fn, a, b, mode):
    if mode is None:
        return fn(a, b)
    ah, bh = a.astype(BF16), b.astype(BF16)
    if mode == "r3":
        r1 = b - bh.astype(F32)
        bm = r1.astype(BF16)
        bl = (r1 - bm.astype(F32)).astype(BF16)
        return fn(ah, bh) + (fn(ah, bm) + fn(ah, bl))
    assert mode == "x3"
    al = (a - ah.astype(F32)).astype(BF16)
    bl = (b - bh.astype(F32)).astype(BF16)
    return fn(ah, bh) + (fn(ah, bl) + fn(al, bh))


def _dot(a, b, mode=None):
    return _mm(lambda p, q: jnp.dot(p, q, preferred_element_type=F32), a, b, mode)


def _dot_nt(a, b):
    return lax.dot_general(a, b, (((1,), (1,)), ((), ())), preferred_element_type=F32)


def _bmm(a, b):
    return jnp.einsum("bij,bjk->bik", a, b, preferred_element_type=F32)


def _bmm_nt(a, b):
    return jnp.einsum("bik,bjk->bij", a, b, preferred_element_type=F32)


def _bmm_tn(a, b):
    return jnp.einsum("bci,bcj->bij", a, b, preferred_element_type=F32)


def _layer_norm_rows(z, g, b):
    mu = jnp.mean(z, axis=-1, keepdims=True)
    zc = z - mu
    var = jnp.mean(zc * zc, axis=-1, keepdims=True)
    return zc * lax.rsqrt(var + LN_EPS) * g + b


def _ada_kernel(c_ref, w_ref, b_ref, o_ref):
    c = c_ref[...]
    cond = c * jax.nn.sigmoid(c)
    o_ref[0] = _dot(cond.astype(BF16), w_ref[0].astype(BF16)) + b_ref[0]


def _ada_call(c_pad, w_ada, b_ada):
    L, D, N = w_ada.shape
    tn = 1024
    return pl.pallas_call(
        _ada_kernel,
        out_shape=jax.ShapeDtypeStruct((L, c_pad.shape[0], N), F32),
        grid=(L, N // tn),
        in_specs=[
            pl.BlockSpec(c_pad.shape, lambda l, n: (0, 0)),
            pl.BlockSpec((1, D, tn), lambda l, n: (l, 0, n)),
            pl.BlockSpec((1, 1, tn), lambda l, n: (l, 0, n)),
        ],
        out_specs=pl.BlockSpec((1, c_pad.shape[0], tn), lambda l, n: (l, 0, n)),
        compiler_params=_cparams(("parallel", "parallel")),
        name="adaln_mod",
    )(c_pad, w_ada, b_ada.reshape(L, 1, N))


def _modmm_kernel(x_ref, sc_ref, sh_ref, w_ref, o1_ref, o2_ref, h_scr, *, n1):
    n = pl.program_id(2)

    @pl.when(n == 0)
    def _():
        h = x_ref[0] * (1.0 + sc_ref[0]) + sh_ref[0]
        h_scr[...] = h.astype(BF16)

    y = _dot(h_scr[...], w_ref[0]).astype(o1_ref.dtype)

    @pl.when(n < n1)
    def _():
        o1_ref[0] = y

    @pl.when(n >= n1)
    def _():
        o2_ref[0] = y


def _modmm_call(x, mod3, row0, w_bf16, layer, n_first, tm, tn):
    B, T, D = x.shape
    N = w_bf16.shape[2]
    n1 = n_first // tn
    return pl.pallas_call(
        functools.partial(_modmm_kernel, n1=n1),
        out_shape=[jax.ShapeDtypeStruct((B, T, n_first), BF16), jax.ShapeDtypeStruct((B, T, N - n_first), BF16)],
        grid=(B, T // tm, N // tn),
        in_specs=[
            pl.BlockSpec((1, tm, D), lambda b, m, n: (b, m, 0)),
            pl.BlockSpec((1, 1, D), lambda b, m, n: (row0 + 6 * b + 1, 0, 0)),
            pl.BlockSpec((1, 1, D), lambda b, m, n: (row0 + 6 * b, 0, 0)),
            pl.BlockSpec((1, D, tn), lambda b, m, n: (layer, 0, n)),
        ],
        out_specs=[pl.BlockSpec((1, tm, tn), lambda b, m, n: (b, m, jnp.minimum(n, n1 - 1))),
                   pl.BlockSpec((1, tm, tn), lambda b, m, n: (b, m, jnp.maximum(n - n1, 0)))],
        scratch_shapes=[pltpu.VMEM((tm, D), BF16)],
        compiler_params=_cparams(("parallel", "parallel", "arbitrary")),
        name="inproj",
    )(x, mod3, mod3, w_bf16)


def _rwkv_prep_kernel(*refs, tt, first_layer):
    if first_layer:
        (u_ref, up_ref, mu_ref, w0_ref, w2_ref, a0_ref, a2_ref, g2_ref, kk_ref, ka_ref,
         r_o, ld_o, cs_o, k_o, v_o, kk_o, a_o, g_o) = refs
    else:
        (u_ref, up_ref, mu_ref, w0_ref, w2_ref, a0_ref, a2_ref, g2_ref, kk_ref, ka_ref,
         vf_ref, v0_ref, v1_ref, v2_ref,
         r_o, ld_o, cs_o, k_o, v_o, kk_o, a_o, g_o) = refs
    ti = pl.program_id(1)
    ub = u_ref[0]
    u = ub.astype(F32)
    row = lax.broadcasted_iota(jnp.int32, (tt, tt), 0)
    col = lax.broadcasted_iota(jnp.int32, (tt, tt), 1)
    us = _dot(jnp.where(col + 1 == row, 1.0, 0.0).astype(BF16), ub)
    prev_row = jnp.where(ti > 0, up_ref[0, 15:16, :].astype(F32), 0.0)
    us = jnp.where(lax.broadcasted_iota(jnp.int32, (tt, 1), 0) == 0, prev_row, us)
    x = u + (us - u) * mu_ref[0]
    r = x[:, 0:D_RWKV]
    k = x[:, D_RWKV:2 * D_RWKV]
    v = x[:, 2 * D_RWKV:3 * D_RWKV]
    xw = x[:, RW_XW:RW_XA]
    xa = x[:, RW_XA:RW_XG]
    xg = x[:, RW_XG:RW_COLS]
    z = w0_ref[0] + _dot(jnp.tanh(xw).astype(BF16), w2_ref[0])
    ld = -EXP_NEG_HALF * jax.nn.sigmoid(z)
    ld_o[0] = ld
    chunk_ltri = jnp.where((row // CHUNK == col // CHUNK) & (col <= row), 1.0, 0.0)
    cs_o[0] = _dot(chunk_ltri, ld, "r3")
    a = jax.nn.sigmoid(a0_ref[0] + _dot(xa.astype(BF16), a2_ref[0]))
    g_o[0] = _dot(jax.nn.sigmoid(xg).astype(BF16), g2_ref[0]).astype(g_o.dtype)
    if not first_layer:
        lo = _dot(v.astype(BF16), v1_ref[0])
        gate = jax.nn.sigmoid(v0_ref[0] + _dot(lo.astype(BF16), v2_ref[0]))
        v = v + (vf_ref[0].astype(F32) - v) * gate
    r_o[0] = r.astype(r_o.dtype)
    v_o[0] = v.astype(v_o.dtype)
    a_o[0] = a.astype(a_o.dtype)
    kk_o[0] = (k * kk_ref[0]).astype(kk_o.dtype)
    k_o[0] = (k * (1.0 + (a - 1.0) * ka_ref[0])).astype(k_o.dtype)


def _rwkv_prep_call(u_rw, p, layer, v_first, tt):
    B, T, _ = u_rw.shape
    first_layer = v_first is None
    tile = pl.BlockSpec((1, tt, D_RWKV), lambda b, t: (b, t, 0))
    names = ["mu", "w0", "w2", "a0", "a2", "g2", "k_k", "k_a"]
    in_specs = [
        pl.BlockSpec((1, tt, RW_COLS), lambda b, t: (b, t, 0)),
        pl.BlockSpec((1, 16, RW_COLS), lambda b, t: (b, jnp.maximum(t * (tt // 16) - 1, 0), 0)),
    ] + [_layer_spec(p[nm], layer) for nm in names]
    args = [u_rw, u_rw] + [p[nm] for nm in names]
    if not first_layer:
        in_specs += [tile] + [_layer_spec(p[nm], layer - 1) for nm in ("v0", "v1", "v2")]
        args += [v_first, p["v0"], p["v1"], p["v2"]]
    out = lambda dt: jax.ShapeDtypeStruct((B, T, D_RWKV), dt)
    return pl.pallas_call(
        functools.partial(_rwkv_prep_kernel, tt=tt, first_layer=first_layer),
        out_shape=[out(BF16), out(F32), out(F32)] + [out(BF16)] * 5,
        grid=(B, T // tt),
        in_specs=in_specs,
        out_specs=[tile] * 8,
        compiler_params=_cparams(("parallel", "parallel")),
        name="rwkv_prep",
    )(*args)


def _rwkv_scan_kernel(r_ref, ld_ref, cs_ref, k_ref, v_ref, kk_ref, a_ref, g_ref, rk_ref, lg_ref, lb_ref,
                      w1_ref, w2_ref, o_ref, w1b_ref, w2b_ref, s_scr, wr_scr, o0_scr, pm_scr, qm_scr, bn_scr, g_scr,
                      *, ts, nt, hpb):
    C = CHUNK
    nc = ts // C
    nb = hpb * nc
    N = HEAD_DIM
    n = pl.program_id(0)

    w1b_ref[...] = w1_ref[0].astype(BF16)
    w2b_ref[...] = w2_ref[0].astype(BF16)

    @pl.when(n == 0)
    def _():
        for scr in (s_scr, wr_scr, o0_scr, pm_scr, qm_scr, bn_scr, g_scr):
            scr[...] = jnp.zeros_like(scr)

    first = lax.rem(jnp.maximum(n - 1, 0), nt) == 0
    states = [jnp.where(first, 0.0, s_scr[hh]) for hh in range(hpb)]
    outs = [None] * nb
    for c in range(nc):
        for hh in range(hpb):
            i = hh * nc + c
            sb = states[hh].astype(BF16)
            outs[i] = _dot_nt(wr_scr[i], sb) + o0_scr[i]
            states[hh] = _dot(sb, pm_scr[i]) + qm_scr[i]
    for hh in range(hpb):
        s_scr[hh] = states[hh]
    o = jnp.stack(outs, axis=0)

    def split_row(ref):
        x = ref[0]
        return jnp.concatenate([jnp.broadcast_to(x[None, :, h * N:(h + 1) * N], (nc, 1, N)) for h in range(hpb)],
                               axis=0)

    mu = jnp.mean(o, axis=-1, keepdims=True)
    oc = o - mu
    var = jnp.mean(oc * oc, axis=-1, keepdims=True)
    on = oc * lax.rsqrt(var + GN_EPS) * split_row(lg_ref) + split_row(lb_ref)
    res = (on + bn_scr[...]) * g_scr[...]
    o_ref[0] = jnp.concatenate([res[h * nc:(h + 1) * nc].reshape(ts, N) for h in range(hpb)],
                               axis=-1).astype(o_ref.dtype)

    def split(ref):
        x = ref[0].astype(F32).reshape(nc, C, hpb * N)
        return jnp.concatenate([x[:, :, h * N:(h + 1) * N] for h in range(hpb)], axis=0)

    r, ld, cs, k, v, kkr, a, g = (split(z) for z in (r_ref, ld_ref, cs_ref, k_ref, v_ref, kk_ref, a_ref, g_ref))
    kkn = kkr / jnp.maximum(jnp.sqrt(jnp.sum(kkr * kkr, axis=-1, keepdims=True)), 1e-12)
    row = lax.broadcasted_iota(jnp.int32, (C, C), 0)
    col = lax.broadcasted_iota(jnp.int32, (C, C), 1)
    incl = col <= row
    strict = col < row
    eye = col == row
    cs_last = cs[:, C - 1:C, :]
    e_in = jnp.exp(cs)
    e_ex = jnp.exp(cs - ld)
    e_neg = jnp.exp(-cs)
    e_hat = jnp.exp(cs_last - cs)
    at = -kkn * e_ex
    b = kkn * a
    bt = b * e_neg
    kt = k * e_neg
    rt = r * e_in
    bh = (b * e_hat).astype(BF16)
    kh = (k * e_hat).astype(BF16)
    vb = v.astype(BF16)
    A = _bmm_nt(jnp.concatenate([at, rt], axis=1).astype(BF16), jnp.concatenate([bt, kt], axis=1).astype(BF16))
    a_ab = jnp.where(strict, A[:, :C, :C], 0.0)
    a_ak = jnp.where(strict, A[:, :C, C:], 0.0).astype(BF16)
    a_rb = jnp.where(incl, A[:, C:, :C], 0.0).astype(BF16)
    a_rk = jnp.where(incl, A[:, C:, C:], 0.0).astype(BF16)
    npow = a_ab.astype(BF16)
    tinv = jnp.where(eye, 1.0, 0.0).astype(F32) + a_ab
    p2 = 2
    while p2 < C:
        npow_f = _bmm(npow, npow)
        npow = npow_f.astype(BF16)
        tinv = tinv + _bmm(npow, tinv.astype(BF16))
        p2 *= 2
    akv = _bmm(a_ak, vb)
    x = _bmm(tinv.astype(BF16), jnp.concatenate([at, akv], axis=-1).astype(BF16))
    xb = x.astype(BF16)
    y = _bmm(a_rb, xb)
    pq = _bmm_tn(xb, bh)
    wr_scr[...] = (rt + y[:, :, :N]).astype(BF16)
    o0_scr[...] = y[:, :, N:] + _bmm(a_rk, vb)
    pm_scr[...] = (pq[:, :N, :] + jnp.where(eye, jnp.exp(cs_last), 0.0)).astype(BF16)
    qm_scr[...] = pq[:, N:, :] + _bmm_tn(vb, kh)
    bn_scr[...] = jnp.sum(r * k * split_row(rk_ref), axis=-1, keepdims=True) * v
    g_scr[...] = g


def _rwkv_scan_call(r, ld, cs, k, v, kk, a, g, p, w1, w2, layer, ts, hpb):
    B, T, _ = r.shape
    nt = T // ts
    hg = RWKV_HEADS // hpb
    nblk = B * hg * nt
    nb = hpb * (ts // CHUNK)
    w = hpb * HEAD_DIM

    def blk(n):
        return n // (hg * nt), lax.rem(n, nt), lax.rem(n // nt, hg)

    cur = lambda n: blk(jnp.minimum(n, nblk - 1))
    prev = lambda n: blk(jnp.maximum(n - 1, 0))
    tile = pl.BlockSpec((1, ts, w), cur)
    rowp = lambda f: pl.BlockSpec((1, 1, w), lambda n: (layer, 0, f(n)[2]))
    sq = lambda dt: pltpu.VMEM((nb, HEAD_DIM, HEAD_DIM), dt)
    _, d_in, d_ff = w1.shape
    rows1, rows2 = d_in // nblk, d_ff // nblk
    assert rows1 * nblk == d_in and rows2 * nblk == d_ff and rows1 % 16 == 0 and rows2 % 16 == 0
    slab = lambda n: jnp.minimum(n, nblk - 1)
    return pl.pallas_call(
        functools.partial(_rwkv_scan_kernel, ts=ts, nt=nt, hpb=hpb),
        out_shape=[jax.ShapeDtypeStruct((B, T, D_RWKV), BF16),
                   jax.ShapeDtypeStruct((d_in, d_ff), BF16), jax.ShapeDtypeStruct((d_ff, d_in), BF16)],
        grid=(nblk + 1,),
        in_specs=[tile] * 8 + [rowp(cur), rowp(prev), rowp(prev),
                               pl.BlockSpec((1, rows1, d_ff), lambda n: (layer, slab(n), 0)),
                               pl.BlockSpec((1, rows2, d_in), lambda n: (layer, slab(n), 0))],
        out_specs=[pl.BlockSpec((1, ts, w), prev), pl.BlockSpec((rows1, d_ff), lambda n: (slab(n), 0)),
                   pl.BlockSpec((rows2, d_in), lambda n: (slab(n), 0))],
        scratch_shapes=[pltpu.VMEM((hpb, HEAD_DIM, HEAD_DIM), F32), pltpu.VMEM((nb, CHUNK, HEAD_DIM), BF16),
                        pltpu.VMEM((nb, CHUNK, HEAD_DIM), F32), sq(BF16), sq(F32),
                        pltpu.VMEM((nb, CHUNK, HEAD_DIM), F32), pltpu.VMEM((nb, CHUNK, HEAD_DIM), F32)],
        compiler_params=_cparams(("arbitrary",)),
        name="rwkv_scan",
    )(r, ld, cs, k, v, kk, a, g, p["r_k"], p["lnx_g"], p["lnx_b"], w1, w2)


def _nsa_prep_kernel(u_ref, cos_ref, sin_ref, q_o, kc_o, vc_o, ks_o, vs_o, kw_o, vw_o, gate_o):
    cos = cos_ref[...]
    sin = sin_ref[...]
    lane = lax.broadcasted_iota(jnp.int32, cos.shape, 1)
    first_half = (lane % HEAD_DIM) < (HEAD_DIM // 2)

    def rope(x):
        other = jnp.where(first_half, pltpu.roll(x, LANES - HEAD_DIM // 2, 1), pltpu.roll(x, HEAD_DIM // 2, 1))
        return x * cos + other * sin

    def put(out_ref, col0, nheads, roped, scale=None):
        for j in range(nheads // 2):
            x = u_ref[0, :, col0 + j * LANES:col0 + (j + 1) * LANES].astype(F32)
            if roped:
                x = rope(x)
            if scale is not None:
                x = x * scale
            out_ref[0, 2 * j] = x[:, :HEAD_DIM].astype(out_ref.dtype)
            out_ref[0, 2 * j + 1] = x[:, HEAD_DIM:].astype(out_ref.dtype)

    low = lane < HEAD_DIM

    def put_wide(out_ref, col0, nheads, scale, tail):
        for j in range(nheads // 2):
            x = rope(u_ref[0, :, col0 + j * LANES:col0 + (j + 1) * LANES].astype(F32))
            if scale is not None:
                x = x * scale
            out_ref[0, 2 * j] = jnp.where(low, x, tail).astype(out_ref.dtype)
            out_ref[0, 2 * j + 1] = jnp.where(low, pltpu.roll(x, HEAD_DIM, 1), tail).astype(out_ref.dtype)

    def put_t(out_ref, col0, nheads):
        extra = out_ref.shape[2] - HEAD_DIM
        ones_row = jnp.where(lax.broadcasted_iota(jnp.int32, (extra, cos.shape[0]), 0) == 0, 1.0, 0.0)
        for j in range(nheads // 2):
            xt = u_ref[0, :, col0 + j * LANES:col0 + (j + 1) * LANES].astype(F32).T
            for i, part in enumerate((xt[:HEAD_DIM, :], xt[HEAD_DIM:, :])):
                out_ref[0, 2 * j + i, :HEAD_DIM, :] = part.astype(out_ref.dtype)
                out_ref[0, 2 * j + i, HEAD_DIM:, :] = ones_row.astype(out_ref.dtype)

    pos = pl.program_id(1) * cos.shape[0] + lax.broadcasted_iota(jnp.int32, cos.shape, 0)
    blk_onehot = jnp.where(lane - HEAD_DIM == pos // SEL_BLOCK, 1.0, 0.0)
    put_wide(q_o, 0, NSA_HEADS, HEAD_DIM ** -0.5 * LOG2_E, 0.0)
    put(kc_o, NS_KC, NSA_KV_HEADS, True)
    put(vc_o, NS_KC + NSA_KV, NSA_KV_HEADS, False)
    put_wide(ks_o, NS_KC + 2 * NSA_KV, NSA_KV_HEADS, None, blk_onehot)
    put_t(vs_o, NS_KC + 3 * NSA_KV, NSA_KV_HEADS)
    put_wide(kw_o, NS_KC + 4 * NSA_KV, NSA_KV_HEADS, None, 0.0)
    put_t(vw_o, NS_KC + 5 * NSA_KV, NSA_KV_HEADS)
    gates_t = jax.nn.sigmoid(u_ref[0, :, NS_GATE:NS_GATE + LANES].astype(F32)).T
    for gi in range(NSA_KV_HEADS):
        gate_o[0, gi] = gates_t[gi * GATE_PAD:(gi + 1) * GATE_PAD, :]


def _nsa_prep_call(u_ns, cos_t, sin_t, tt):
    B, T, _ = u_ns.shape
    G, H, N = NSA_KV_HEADS, NSA_HEADS, HEAD_DIM
    assert T // SEL_BLOCK <= LANES - N
    kv = jax.ShapeDtypeStruct((B, G, T, N), F32)
    kv_wide = jax.ShapeDtypeStruct((B, G, T, LANES), BF16)
    kv_t = jax.ShapeDtypeStruct((B, G, VT_ROWS, T), BF16)
    kv_spec = pl.BlockSpec((1, G, tt, N), lambda b, t: (b, 0, t, 0))
    kvw_spec = pl.BlockSpec((1, G, tt, LANES), lambda b, t: (b, 0, t, 0))
    kvt_spec = pl.BlockSpec((1, G, VT_ROWS, tt), lambda b, t: (b, 0, 0, t))
    return pl.pallas_call(
        _nsa_prep_kernel,
        out_shape=[jax.ShapeDtypeStruct((B, H, T, LANES), BF16), kv, kv, kv_wide, kv_t,
                   kv_wide, kv_t, jax.ShapeDtypeStruct((B, G, GATE_PAD, T), F32)],
        grid=(B, T // tt),
        in_specs=[
            pl.BlockSpec((1, tt, NS_COLS), lambda b, t: (b, t, 0)),
            pl.BlockSpec((tt, LANES), lambda b, t: (t, 0)),
            pl.BlockSpec((tt, LANES), lambda b, t: (t, 0)),
        ],
        out_specs=[pl.BlockSpec((1, H, tt, LANES), lambda b, t: (b, 0, t, 0)), kv_spec, kv_spec, kvw_spec, kvt_spec,
                   kvw_spec, kvt_spec, pl.BlockSpec((1, G, GATE_PAD, tt), lambda b, t: (b, 0, 0, t))],
        compiler_params=_cparams(("parallel", "parallel")),
        name="nsa_prep",
    )(u_ns, cos_t, sin_t)


def _compress_kernel(x_ref, pos_ref, w1_ref, w2_ref, o_ref):
    x = x_ref[0, 0]
    half = x.shape[1]
    y_top = _dot(x + pos_ref[0, 0:1, :], w1_ref[0, :half, :], "x3")
    y_bot = _dot(x + pos_ref[0, 1:2, :], w1_ref[0, half:, :], "x3")
    pre = y_top + pltpu.roll(y_bot, x.shape[0] - 1, 0)
    o_ref[0, 0] = _dot(jax.nn.gelu(pre), w2_ref[0], "x3")


def _compress_call(kv, pos, w1, w2, which):
    B, G, T, N = kv.shape
    n_half = T // CMP_STRIDE
    width = CMP_STRIDE * N
    hidden = w1.shape[-1]
    x = kv.reshape(B, G, n_half, width)
    return pl.pallas_call(
        _compress_kernel,
        out_shape=jax.ShapeDtypeStruct((B, G, n_half, w2.shape[-1]), F32),
        grid=(B, G),
        in_specs=[
            pl.BlockSpec((1, 1, n_half, width), lambda b, g: (b, g, 0, 0)),
            pl.BlockSpec((1, 2, width), lambda b, g: (which, 0, 0)),
            pl.BlockSpec((1, 2 * width, hidden), lambda b, g: (which, 0, 0)),
            pl.BlockSpec((1, hidden, w2.shape[-1]), lambda b, g: (which, 0, 0)),
        ],
        out_specs=pl.BlockSpec((1, 1, n_half, w2.shape[-1]), lambda b, g: (b, g, 0, 0)),
        compiler_params=_cparams(("parallel", "parallel")),
        name="nsa_compress",
    )(x, pos, w1, w2)


def _nsa_attn_kernel(q_ref, kc_ref, vc_ref, ks_ref, vst_ref, kw_ref, vwt_ref, gate_ref, ov_ref, put_ref, wo_ref,
                     o_ref, wob_ref, s0_scr, s1_scr, *, tq, tk):
    HPG, N = HEADS_PER_GROUP, HEAD_DIM
    qi = pl.program_id(2)
    q0 = qi * tq
    wob_ref[...] = wo_ref[0].astype(BF16)

    wq = HPG * tq
    q_win = q_ref[0].reshape(wq, LANES)
    rel = lax.broadcasted_iota(jnp.int32, (tk, tq), 1) - lax.broadcasted_iota(jnp.int32, (tk, tq), 0)
    init = (jnp.full((1, wq), NEG, F32), jnp.zeros((VT_ROWS, wq), F32))

    def scores(q_all, k_ref, k0, dst):
        dst[...] = _dot_nt(k_ref[0, 0, pl.ds(k0, tk), :], q_all)

    def update(carry, src, vt_ref, k0, bias):
        m_i, acc = carry
        s = src[...]
        if bias is not None:
            s = s + (bias if bias.ndim == 0 else jnp.concatenate([bias] * HPG, axis=1))
        m_n = jnp.maximum(m_i, jnp.max(s, axis=0, keepdims=True))
        p = jnp.exp2(s - m_n)
        return m_n, jnp.exp2(m_i - m_n) * acc + _dot(vt_ref[0, 0, :, pl.ds(k0, tk)], p.astype(BF16))

    def split(o):
        return [o[:, h * tq:(h + 1) * tq] for h in range(HPG)]

    def finish(carry):
        return split(carry[1][:N] / carry[1][N:N + 1])

    def window_tile(e):
        koff = e * tk - WINDOW
        lo, hi = -(tk - 1) - koff, (tq - 1) - koff
        bias = None
        if lo < 0 or hi >= WINDOW:
            dist = rel - koff
            bias = jnp.where((dist >= 0) & (dist < WINDOW), 0.0, NEG)
        if koff < 0:
            bias = jnp.where(q0 + koff >= 0, 0.0 if bias is None else bias, NEG)
        return pl.multiple_of(jnp.maximum(q0 + koff, 0), tk), bias

    win = [window_tile(e) for e in range((WINDOW + tq) // tk)]
    assert len(win) in (2, 3)
    bufs = (s0_scr, s1_scr)
    scores(q_win, kw_ref, win[0][0], bufs[0])

    n_cmp = kc_ref.shape[2]
    c_c = lax.broadcasted_iota(jnp.int32, (n_cmp, tq), 0)
    t_c = q0 + lax.broadcasted_iota(jnp.int32, (n_cmp, tq), 1)
    cmask = jnp.concatenate([c_c * CMP_STRIDE + (CMP_BLOCK - 1) <= t_c] * HPG, axis=1)
    s = jnp.where(cmask, _dot_nt(kc_ref[0, 0].astype(BF16), q_win), NEG)
    scores(q_win, kw_ref, win[1][0], bufs[1])
    m = jnp.max(s, axis=0, keepdims=True)
    e = jnp.where(cmask, jnp.exp2(s - m), 0.0)
    l = jnp.sum(e, axis=0, keepdims=True)
    p = e / jnp.maximum(l, 1e-30)
    o_cmp = split(_dot(vc_ref[0, 0].T[:N].astype(BF16), p.astype(BF16)))
    psum = p[:, :tq]
    for h in range(1, HPG):
        psum = psum + p[:, h * tq:(h + 1) * tq]

    n_sel = ov_ref.shape[0]
    imp = _dot(ov_ref[...], psum, "r3")
    carry = update(init, bufs[0], vwt_ref, *win[0])
    if len(win) == 3:
        scores(q_win, kw_ref, win[2][0], bufs[0])
    blk = lax.broadcasted_iota(jnp.int32, (n_sel, tq), 0)
    t_s = q0 + lax.broadcasted_iota(jnp.int32, (n_sel, tq), 1)
    cur = t_s // SEL_BLOCK
    forced = (blk == 0) | (blk == cur) | (blk == cur - 1)
    val = jnp.where(forced, FORCE, jnp.where(blk * SEL_BLOCK <= t_s, imp, -1.0))
    rank = jnp.zeros((n_sel, tq), F32)
    for i in range(n_sel):
        vi = val[i:i + 1, :]
        ahead = (vi > val) | ((vi == val) & (blk > i))
        rank = rank + jnp.where(ahead, 1.0, 0.0)
    unsel_t = jnp.where(rank < float(min(N_SELECT, n_sel)), 0.0, NEG)
    q_bias = lax.dot_general(unsel_t.astype(BF16), put_ref[...], (((0,), (0,)), ((), ())),
                             preferred_element_type=F32)
    carry = update(carry, bufs[1], vwt_ref, *win[1])
    q_sel = jnp.concatenate([(q_ref[0, h].astype(F32) + q_bias).astype(BF16) for h in range(HPG)], axis=0)
    o_win = finish(update(carry, bufs[0], vwt_ref, *win[2]) if len(win) == 3 else carry)

    def key0(j):
        return pl.multiple_of(j * tk, tk)

    scores(q_sel, ks_ref, key0(0), s0_scr)

    def pair(jj, carry):
        scores(q_sel, ks_ref, key0(2 * jj + 1), s1_scr)
        carry = update(carry, s0_scr, vst_ref, key0(2 * jj), None)
        scores(q_sel, ks_ref, key0(2 * jj + 2), s0_scr)
        return update(carry, s1_scr, vst_ref, key0(2 * jj + 1), None)

    carry = lax.fori_loop(0, qi // 2, pair, init)
    causal = jnp.where(rel >= 0, 0.0, NEG)

    def tail_even(c):
        return update(c, s0_scr, vst_ref, key0(qi), causal)

    def tail_odd(c):
        scores(q_sel, ks_ref, key0(qi), s1_scr)
        c = update(c, s0_scr, vst_ref, key0(qi - 1), None)
        return update(c, s1_scr, vst_ref, key0(qi), causal)

    o_slc = finish(lax.cond(qi % 2 == 1, tail_odd, tail_even, carry))

    gate = gate_ref[0, 0]
    outs = [gate[3 * h:3 * h + 1, :] * o_cmp[h] + gate[3 * h + 1:3 * h + 2, :] * o_slc[h]
            + gate[3 * h + 2:3 * h + 3, :] * o_win[h] for h in range(HPG)]
    o_ref[0] = jnp.concatenate(outs, axis=0).T.astype(o_ref.dtype)


def _nsa_attn_call(q, kc, vc, ks, vs_t, kw, vw_t, gates_t, ov_t, put, w_out, layer, tq, tk):
    B, H, T, _ = q.shape
    N = HEAD_DIM
    G, HPG = NSA_KV_HEADS, HEADS_PER_GROUP
    assert WINDOW % tk == 0 and tq == tk
    n_half = kc.shape[2]
    cmp_spec = pl.BlockSpec((1, 1, n_half, LANES), lambda b, g, t: (b, g, 0, 0))
    kv_spec = pl.BlockSpec((1, 1, T, LANES), lambda b, g, t: (b, g, 0, 0))
    kvt_spec = pl.BlockSpec((1, 1, VT_ROWS, T), lambda b, g, t: (b, g, 0, 0))
    nq = T // tq
    _, wo_rows, wo_cols = w_out.shape
    rows = wo_rows // (B * G * nq)
    assert rows * B * G * nq == wo_rows and rows % 16 == 0
    slab = lambda b, g, t: (b * G + g) * nq + t
    return pl.pallas_call(
        functools.partial(_nsa_attn_kernel, tq=tq, tk=tk),
        out_shape=[jax.ShapeDtypeStruct((B, T, D_NSA), BF16),
                   jax.ShapeDtypeStruct((wo_rows, wo_cols), BF16)],
        grid=(B, G, T // tq),
        in_specs=[
            pl.BlockSpec((1, HPG, tq, LANES), lambda b, g, t: (b, g, t, 0)),
            cmp_spec, cmp_spec, kv_spec, kvt_spec, kv_spec, kvt_spec,
            pl.BlockSpec((1, 1, GATE_PAD, tq), lambda b, g, t: (b, g, 0, t)),
            pl.BlockSpec(ov_t.shape, lambda b, g, t: (0, 0)),
            pl.BlockSpec(put.shape, lambda b, g, t: (0, 0)),
            pl.BlockSpec((1, rows, wo_cols), lambda b, g, t: (layer, slab(b, g, t), 0)),
        ],
        out_specs=[pl.BlockSpec((1, tq, HPG * N), lambda b, g, t: (b, t, g)),
                   pl.BlockSpec((rows, wo_cols), lambda b, g, t: (slab(b, g, t), 0))],
        scratch_shapes=[pltpu.VMEM((tk, HPG * tq), F32), pltpu.VMEM((tk, HPG * tq), F32)],
        compiler_params=_cparams(("parallel", "parallel", "arbitrary")),
        name="nsa_attn",
    )(q, kc, vc, ks, vs_t, kw, vw_t, gates_t, ov_t, put, w_out)


def _outproj_kernel(orw_ref, ons_ref, x_ref, gt_ref, w_ref, g_ref, b_ref, o_ref, *, alpha):
    half = orw_ref.shape[-1]
    rows = orw_ref.shape[1] // 2
    ys = [_dot(orw_ref[0, sl, :], w_ref[:half, :]) + _dot(ons_ref[0, sl, :], w_ref[half:, :])
          for sl in (slice(0, rows), slice(rows, 2 * rows))]
    for sl, y in zip((slice(0, rows), slice(rows, 2 * rows)), ys):
        z = alpha * x_ref[0, sl, :] + (1.0 + gt_ref[0]) * y
        o_ref[0, sl, :] = _layer_norm_rows(z, g_ref[0], b_ref[0])


def _outproj_call(o_rw, o_ns, x, mod3, row0, w_bf16, ln_g, ln_b, layer, alpha, tm):
    B, T, D = x.shape
    half = o_rw.shape[-1]
    return pl.pallas_call(
        functools.partial(_outproj_kernel, alpha=alpha),
        out_shape=jax.ShapeDtypeStruct((B, T, D), F32),
        grid=(B, T // tm),
        in_specs=[
            pl.BlockSpec((1, tm, half), lambda b, m: (b, m, 0)),
            pl.BlockSpec((1, tm, half), lambda b, m: (b, m, 0)),
            pl.BlockSpec((1, tm, D), lambda b, m: (b, m, 0)),
            pl.BlockSpec((1, 1, D), lambda b, m: (row0 + 6 * b + 2, 0, 0)),
            pl.BlockSpec(w_bf16.shape, lambda b, m: (0, 0)), _layer_spec(ln_g, layer), _layer_spec(ln_b, layer),
        ],
        out_specs=pl.BlockSpec((1, tm, D), lambda b, m: (b, m, 0)),
        compiler_params=_cparams(("parallel", "parallel")),
        name="outproj_ln",
    )(o_rw, o_ns, x, mod3, w_bf16, ln_g, ln_b)


def _mlp_kernel(x_ref, sc_ref, sh_ref, gt_ref, w1_ref, w2_ref, g_ref, b_ref, o_ref, h_scr, acc_scr, *, alpha):
    f = pl.program_id(2)

    @pl.when(f == 0)
    def _():
        h = x_ref[0] * (1.0 + sc_ref[0]) + sh_ref[0]
        h_scr[...] = h.astype(BF16)
        acc_scr[...] = jnp.zeros_like(acc_scr)

    a = jnp.maximum(_dot(h_scr[...], w1_ref[...]), 0.0)
    acc_scr[...] += _dot((a * a).astype(BF16), w2_ref[...])

    @pl.when(f == pl.num_programs(2) - 1)
    def _():
        z = alpha * x_ref[0] + (1.0 + gt_ref[0]) * acc_scr[...]
        o_ref[0] = _layer_norm_rows(z, g_ref[0], b_ref[0])


def _mlp_call(x, mod3, row0, w1_bf16, w2_bf16, ln_g, ln_b, layer, alpha, tm, tf):
    B, T, D = x.shape
    FF = w1_bf16.shape[1]
    modspec = lambda j: pl.BlockSpec((1, 1, D), lambda b, m, f: (row0 + 6 * b + j, 0, 0))
    return pl.pallas_call(
        functools.partial(_mlp_kernel, alpha=alpha),
        out_shape=jax.ShapeDtypeStruct((B, T, D), F32),
        grid=(B, T // tm, FF // tf),
        in_specs=[
            pl.BlockSpec((1, tm, D), lambda b, m, f: (b, m, 0)),
            modspec(4), modspec(3), modspec(5),
            pl.BlockSpec((D, tf), lambda b, m, f: (0, f)),
            pl.BlockSpec((tf, D), lambda b, m, f: (f, 0)),
            _layer_spec(ln_g, layer), _layer_spec(ln_b, layer),
        ],
        out_specs=pl.BlockSpec((1, tm, D), lambda b, m, f: (b, m, 0)),
        scratch_shapes=[pltpu.VMEM((tm, D), BF16), pltpu.VMEM((tm, D), F32)],
        compiler_params=_cparams(("parallel", "parallel", "arbitrary")),
        name="mlp_ln",
    )(x, mod3, mod3, mod3, w1_bf16, w2_bf16, ln_g, ln_b)


def _pad_last(w, n):
    return jnp.pad(w, [(0, 0)] * (w.ndim - 1) + [(0, n - w.shape[-1])])


def _pad_rows(w, n):
    return jnp.pad(w, [(0, 0)] * (w.ndim - 2) + [(0, n - w.shape[-2]), (0, 0)])


def _split_w_in(w_in):
    c = np.cumsum([0, D_RWKV, D_RWKV, D_RWKV, DECAY_LORA, AAA_LORA, GATE_LORA]).tolist()
    w_in = w_in.astype(BF16)
    rw = jnp.concatenate([w_in[..., c[0]:c[3]], _pad_last(w_in[..., c[3]:c[4]], LORA_PAD),
                          _pad_last(w_in[..., c[4]:c[5]], LORA_PAD), w_in[..., c[5]:c[6]]], axis=-1)
    ng = 3 * HEADS_PER_GROUP
    lead = w_in.shape[:-1]
    gates = _pad_last(w_in[..., c[6] + NS_GATE:].reshape(lead + (NSA_KV_HEADS, ng)), GATE_PAD)
    gates = _pad_last(gates.reshape(lead + (NSA_KV_HEADS * GATE_PAD,)), LANES)
    ns = jnp.concatenate([w_in[..., c[6]:c[6] + NS_GATE], gates], axis=-1)
    return jnp.concatenate([rw, ns], axis=-1)


def _pad_mu(mu):
    c = np.cumsum([0, 3 * D_RWKV, DECAY_LORA, AAA_LORA, GATE_LORA]).tolist()
    parts = [mu[:, c[0]:c[1]], _pad_last(mu[:, c[1]:c[2]], LORA_PAD), _pad_last(mu[:, c[2]:c[3]], LORA_PAD),
             mu[:, c[3]:c[4]]]
    return jnp.concatenate(parts, axis=-1)[:, None, :]


def _rope_tables(T):
    half = HEAD_DIM // 2
    inv = ROPE_THETA ** (-jnp.arange(half, dtype=F32) / half)
    ang = jnp.arange(T, dtype=F32)[:, None] * inv[None]
    cos, sin = jnp.cos(ang), jnp.sin(ang)
    cos_t = jnp.tile(cos, (1, LANES // half))
    sin_t = jnp.tile(jnp.concatenate([-sin, sin], axis=1), (1, LANES // HEAD_DIM))
    return cos_t, sin_t


def _selection_constants(T):
    n_half = T // CMP_STRIDE
    n_cmp = (T - CMP_BLOCK) // CMP_STRIDE + 1
    n_sel = T // SEL_BLOCK
    pos = np.arange(n_cmp)[:, None] * CMP_STRIDE + np.arange(CMP_BLOCK)[None]
    ov = ((pos // SEL_BLOCK)[..., None] == np.arange(n_sel)).sum(1) / CMP_BLOCK
    ov_t = np.zeros((n_sel, n_half), np.float32)
    ov_t[:, :n_cmp] = ov.T
    put = (np.arange(LANES)[None, :] == HEAD_DIM + np.arange(n_sel)[:, None]).astype(np.float32)
    return jnp.asarray(ov_t), jnp.asarray(put, BF16)


def kernel(x, c, w_ada, b_ada, w_in, rwkv_mu, rwkv_w0, rwkv_w2, rwkv_a0, rwkv_a2, rwkv_g2, rwkv_k_k, rwkv_k_a, rwkv_r_k, rwkv_lnx_g, rwkv_lnx_b, rwkv_v0, rwkv_v1, rwkv_v2, nsa_cmp_pos, nsa_cmp_w1, nsa_cmp_w2, w_out, ln1_g, ln1_b, mlp_w1, mlp_w2, ln2_g, ln2_b):
    B, T, D = x.shape
    L = w_ada.shape[0]
    alpha = (2 * L) ** 0.25

    c_pad = jnp.pad(c, ((0, -B % 8), (0, 0)))
    mod = _ada_call(c_pad, w_ada, b_ada)[:, :B]
    mod3 = mod.reshape(L * B * 6, 1, D)
    cos_t, sin_t = _rope_tables(T)
    ov_t, sel_put = _selection_constants(T)

    rows = lambda z: z.reshape(z.shape[0], 1, -1)
    w_in_b = _split_w_in(w_in)
    p = {
        "mu": _pad_mu(rwkv_mu), "w0": rows(rwkv_w0), "a0": rows(rwkv_a0),
        "w2": _pad_rows(rwkv_w2, LORA_PAD).astype(BF16), "a2": _pad_rows(rwkv_a2, LORA_PAD).astype(BF16),
        "g2": rwkv_g2.astype(BF16), "k_k": rows(rwkv_k_k), "k_a": rows(rwkv_k_a),
        "r_k": rows(rwkv_r_k), "lnx_g": rows(rwkv_lnx_g), "lnx_b": rows(rwkv_lnx_b),
        "v0": rows(rwkv_v0), "v1": _pad_last(rwkv_v1, LORA_PAD).astype(BF16),
        "v2": _pad_rows(rwkv_v2, LORA_PAD).astype(BF16),
    }
    half_block = CMP_STRIDE * HEAD_DIM
    cmp_pos = nsa_cmp_pos.reshape(2 * L, 2, half_block)
    cmp_w1 = nsa_cmp_w1.reshape((2 * L,) + nsa_cmp_w1.shape[2:])
    cmp_w2 = _pad_last(nsa_cmp_w2.reshape((2 * L,) + nsa_cmp_w2.shape[2:]), LANES)
    ln1 = (rows(ln1_g), rows(ln1_b))
    ln2 = (rows(ln2_g), rows(ln2_b))

    v_first = None
    for i in range(L):
        row0 = i * B * 6
        u_rw, u_ns = _modmm_call(x, mod3, row0, w_in_b, i, RW_COLS, TILES["proj_m"], PROJ_TN)

        r, ld, cs, k, v, kk, a, g = _rwkv_prep_call(u_rw, p, i, v_first, TILES["prep"])
        if i == 0:
            v_first = v
        o_rw, w1_b, w2_b = _rwkv_scan_call(r, ld, cs, k, v, kk, a, g, p, mlp_w1, mlp_w2, i, TILES["scan_rows"],
                                           TILES["scan_heads"])

        q, kc_in, vc_in, ks, vs_t, kw, vw_t, gates_t = _nsa_prep_call(u_ns, cos_t, sin_t, TILES["prep"])
        kc = _compress_call(kc_in, cmp_pos, cmp_w1, cmp_w2, 2 * i)
        vc = _compress_call(vc_in, cmp_pos, cmp_w1, cmp_w2, 2 * i + 1)
        o_ns, w_out_b = _nsa_attn_call(q, kc, vc, ks, vs_t, kw, vw_t, gates_t, ov_t, sel_put, w_out, i,
                                       TILES["attn_q"], TILES["attn_k"])

        x = _outproj_call(o_rw, o_ns, x, mod3, row0, w_out_b, *ln1, i, alpha, TILES["out_m"])
        x = _mlp_call(x, mod3, row0, w1_b, w2_b, *ln2, i, alpha, TILES["mlp_m"], TILES["mlp_f"])
    return x
```
